```python
import jax, jax.numpy as jnp
from jax import lax
import numpy as np

D_MODEL = 2048
BATCH = 4
SEQ = 2048
DEPTH = 1

CHUNK = 64
SGU_BLOCK = 128
SGU_WIDTH = 2048
SGU_GROUPS = 8
SGU_GROUP_DIM = SGU_WIDTH // SGU_GROUPS
MLSTM_WIDTH = 2048
MLSTM_HEADS = 4
MLSTM_HEAD_DIM = MLSTM_WIDTH // MLSTM_HEADS
CONV_WIDTH = 4
N_EXPERTS = 32
TOP_K = 4
D_FF = 2048
SWIGLU_LIMIT = 7.0
SWIGLU_ALPHA = 1.702
EPS = 1e-6

COL_SIZES = (SGU_WIDTH, SGU_WIDTH,
             MLSTM_WIDTH, MLSTM_WIDTH, MLSTM_WIDTH,
             MLSTM_WIDTH,
             MLSTM_HEADS, MLSTM_HEADS,
             D_MODEL, D_MODEL)
IN_COLS = 2 * SGU_WIDTH + 4 * MLSTM_WIDTH + 2 * MLSTM_HEADS + 2 * D_MODEL
F_GATE_OFF = 2 * SGU_WIDTH + 4 * MLSTM_WIDTH + MLSTM_HEADS

kernel_name = "hybrid_gmlp_mlstm_moe_block"


def rmsnorm(x, g):
    xf = x.astype(jnp.float32)
    y = xf * lax.rsqrt(jnp.mean(xf * xf, axis=-1, keepdims=True) + EPS)
    return (y * g.astype(jnp.float32)).astype(x.dtype)


def layernorm(x, g, b):
    xf = x.astype(jnp.float32)
    mu = jnp.mean(xf, axis=-1, keepdims=True)
    var = jnp.mean(jnp.square(xf - mu), axis=-1, keepdims=True)
    y = (xf - mu) * lax.rsqrt(var + EPS)
    return (y * g.astype(jnp.float32) + b.astype(jnp.float32)).astype(x.dtype)


def split_columns(z):
    idx = np.cumsum(np.array(COL_SIZES))[:-1].tolist()
    return jnp.split(z, idx, axis=-1)


def causal_conv(x, w, b):
    s = x.shape[1]
    xp = jnp.pad(x, ((0, 0), (CONV_WIDTH - 1, 0), (0, 0)))
    y = xp[:, 0:s] * w[0]
    for j in range(1, CONV_WIDTH):
        y = y + xp[:, j:j + s] * w[j]
    return y + b


def spatial_gating(u, v, w_s, b_s, g_ln, b_ln):
    bsz, s, _ = v.shape
    v = layernorm(v, g_ln, b_ln)
    nb = s // SGU_BLOCK
    vb = v.reshape(bsz, nb, SGU_BLOCK, SGU_GROUPS, SGU_GROUP_DIM)
    chunk_id = jnp.arange(SGU_BLOCK) // CHUNK
    mask = chunk_id[None, :] <= chunk_id[:, None]
    w = jnp.where(mask[None], w_s, 0)
    mixed = jnp.einsum('gts,bnsgd->bntgd', w, vb) + b_s.T[None, None, :, :, None]
    return u * mixed.reshape(bsz, s, SGU_WIDTH)


def mlstm_chunkwise(q, k, v, ig, fg):
    out_dtype = q.dtype
    q, k, v, ig, fg = (a.astype(jnp.float32) for a in (q, k, v, ig, fg))
    bsz, nh, s, dh = q.shape
    nc = s // CHUNK
    q = q * (dh ** -0.5)
    logf = jax.nn.log_sigmoid(fg)

    def to_chunks(a):
        a = a.reshape(a.shape[:2] + (nc, CHUNK) + a.shape[3:])
        return jnp.moveaxis(a, 2, 0)

    xs = (to_chunks(q), to_chunks(k), to_chunks(v), to_chunks(ig), to_chunks(logf))
    tril = jnp.tril(jnp.ones((CHUNK, CHUNK), dtype=bool))

    def step(carry, inp):
        c_st, n_st, m_st = carry
        qi, ki, vi, igi, lfi = inp
        bcum = jnp.cumsum(lfi, axis=-1)
        dlog = bcum[..., :, None] - bcum[..., None, :] + igi[..., None, :]
        dlog = jnp.where(tril, dlog, -jnp.inf)
        m_inter = bcum + m_st[..., None]
        m_t = jnp.maximum(m_inter, jnp.max(dlog, axis=-1))
        sc = jnp.einsum('bhtd,bhsd->bhts', qi, ki) * jnp.exp(dlog - m_t[..., None])
        inter = jnp.exp(m_inter - m_t)
        num = jnp.einsum('bhts,bhsd->bhtd', sc, vi) + inter[..., None] * jnp.einsum('bhtk,bhvk->bhtv', qi, c_st)
        den = jnp.sum(sc, axis=-1) + inter * jnp.einsum('bhtk,bhk->bht', qi, n_st)
        h = num / jnp.maximum(jnp.abs(den), jnp.exp(-m_t))[..., None]
        g_tot = bcum[..., -1]
        a = g_tot[..., None] - bcum + igi
        m_new = jnp.maximum(g_tot + m_st, jnp.max(a, axis=-1))
        wa = jnp.exp(a - m_new[..., None])
        decay = jnp.exp(g_tot + m_st - m_new)
        c_new = decay[..., None, None] * c_st + jnp.einsum('bhs,bhsv,bhsk->bhvk', wa, vi, ki)
        n_new = decay[..., None] * n_st + jnp.einsum('bhs,bhsk->bhk', wa, ki)
        return (c_new, n_new, m_new), h

    init = (jnp.zeros((bsz, nh, dh, dh), jnp.float32),
            jnp.zeros((bsz, nh, dh), jnp.float32),
            jnp.zeros((bsz, nh), jnp.float32))
    _, hs = lax.scan(step, init, xs)
    h = jnp.moveaxis(hs, 0, 2).reshape(bsz, nh, s, dh)
    return h.astype(out_dtype)


def hybrid_mixer(h, w_in, b_in, w_conv, b_conv, w_sgu, b_sgu, g_sgu_ln, b_sgu_ln,
                 g_mlstm_norm, w_proj_sgu, w_proj_mlstm, w_out):
    bsz, s, _ = h.shape
    z = h @ w_in + b_in
    u, v, qm, km, vm, om, ig, fg, ga, gm = split_columns(z)
    y_a = spatial_gating(jax.nn.gelu(u, approximate=False), jax.nn.gelu(v, approximate=False),
                         w_sgu, b_sgu, g_sgu_ln, b_sgu_ln)
    qk = jax.nn.silu(causal_conv(jnp.concatenate([qm, km], axis=-1), w_conv, b_conv))
    qm, km = jnp.split(qk, 2, axis=-1)

    def heads(a):
        return a.reshape(bsz, s, MLSTM_HEADS, MLSTM_HEAD_DIM).transpose(0, 2, 1, 3)

    hm = mlstm_chunkwise(heads(qm), heads(km), heads(vm), ig.transpose(0, 2, 1), fg.transpose(0, 2, 1))
    hm = rmsnorm(hm.transpose(0, 2, 1, 3), g_mlstm_norm.reshape(MLSTM_HEADS, MLSTM_HEAD_DIM))
    y_m = hm.reshape(bsz, s, MLSTM_WIDTH) * jax.nn.sigmoid(om)
    merged = jax.nn.sigmoid(ga) * (y_a @ w_proj_sgu) + jax.nn.sigmoid(gm) * (y_m @ w_proj_mlstm)
    return merged @ w_out


def moe_ffn(h, w_router, b_router, w_gate_up, b_gate_up, w_down, b_down):
    bsz, s, d = h.shape
    t = h.reshape(bsz * s, d)
    logits = (t @ w_router + b_router).astype(jnp.float32)
    top_vals, top_idx = lax.top_k(logits, TOP_K)
    probs = jax.nn.softmax(top_vals, axis=-1)
    gates = jnp.sum(jax.nn.one_hot(top_idx, N_EXPERTS, dtype=jnp.float32) * probs[..., None], axis=1)
    out = jnp.zeros((bsz * s, d), jnp.float32)
    for e in range(N_EXPERTS):
        gu = t @ w_gate_up[e] + b_gate_up[e]
        gate, up = gu[:, :D_FF], gu[:, D_FF:]
        gate = jnp.minimum(gate, SWIGLU_LIMIT)
        up = jnp.clip(up, -SWIGLU_LIMIT, SWIGLU_LIMIT)
        act = (up + 1) * (gate * jax.nn.sigmoid(SWIGLU_ALPHA * gate))
        y_e = act @ w_down[e] + b_down[e]
        out = out + gates[:, e:e + 1] * y_e.astype(jnp.float32)
    return out.reshape(bsz, s, d).astype(h.dtype)


def setup_inputs(seed: int = 0) -> dict:
    key = jax.random.key(seed)
    ks = jax.random.split(key, 24)
    L, D, E, F = DEPTH, D_MODEL, N_EXPERTS, D_FF

    def nrm(k, shape, scale):
        return jax.random.normal(k, shape, jnp.float32) * scale

    b_in = nrm(ks[3], (L, IN_COLS), 0.02)
    b_in = b_in.at[:, F_GATE_OFF:F_GATE_OFF + MLSTM_HEADS].add(jnp.linspace(3.0, 6.0, MLSTM_HEADS))
    return {
        "x": nrm(ks[0], (BATCH, SEQ, D), 1.0),
        "g_norm_mix": 1.0 + nrm(ks[1], (L, D), 0.02),
        "w_in": nrm(ks[2], (L, D, IN_COLS), D ** -0.5),
        "b_in": b_in,
        "w_conv": nrm(ks[4], (L, CONV_WIDTH, 2 * MLSTM_WIDTH), CONV_WIDTH ** -0.5),
        "b_conv": nrm(ks[5], (L, 2 * MLSTM_WIDTH), 0.02),
        "w_sgu": nrm(ks[6], (L, SGU_GROUPS, SGU_BLOCK, SGU_BLOCK), SGU_BLOCK ** -0.5),
        "b_sgu": 1.0 + nrm(ks[7], (L, SGU_GROUPS, SGU_BLOCK), 0.02),
        "g_sgu_ln": 1.0 + nrm(ks[8], (L, SGU_WIDTH), 0.02),
        "b_sgu_ln": nrm(ks[9], (L, SGU_WIDTH), 0.02),
        "g_mlstm_norm": 1.0 + nrm(ks[10], (L, MLSTM_WIDTH), 0.02),
        "w_proj_sgu": nrm(ks[11], (L, SGU_WIDTH, D), SGU_WIDTH ** -0.5),
        "w_proj_mlstm": nrm(ks[12], (L, MLSTM_WIDTH, D), MLSTM_WIDTH ** -0.5),
        "w_out": nrm(ks[13], (L, D, D), D ** -0.5),
        "g_norm_moe": 1.0 + nrm(ks[14], (L, D), 0.02),
        "w_router": nrm(ks[15], (L, D, E), D ** -0.5),
        "b_router": nrm(ks[16], (L, E), 0.01),
        "w_gate_up": nrm(ks[17], (L, E, D, 2 * F), D ** -0.5),
        "b_gate_up": nrm(ks[18], (L, E, 2 * F), 0.02),
        "w_down": nrm(ks[19], (L, E, F, D), F ** -0.5),
        "b_down": nrm(ks[20], (L, E, D), 0.02),
        "g_final": 1.0 + nrm(ks[21], (D,), 0.02),
    }


def reference(x, g_norm_mix, w_in, b_in, w_conv, b_conv, w_sgu, b_sgu, g_sgu_ln, b_sgu_ln,
              g_mlstm_norm, w_proj_sgu, w_proj_mlstm, w_out, g_norm_moe, w_router, b_router,
              w_gate_up, b_gate_up, w_down, b_down, g_final):
    for l in range(DEPTH):
        h = rmsnorm(x, g_norm_mix[l])
        x = x + hybrid_mixer(h, w_in[l], b_in[l], w_conv[l], b_conv[l], w_sgu[l], b_sgu[l],
                             g_sgu_ln[l], b_sgu_ln[l], g_mlstm_norm[l], w_proj_sgu[l],
                             w_proj_mlstm[l], w_out[l]).astype(x.dtype)
        h = rmsnorm(x, g_norm_moe[l])
        x = x + moe_ffn(h, w_router[l], b_router[l], w_gate_up[l], b_gate_up[l],
                        w_down[l], b_down[l]).astype(x.dtype)
    return rmsnorm(x, g_final)
```

```python
import functools

import jax
import jax.numpy as jnp
from jax import lax
from jax.experimental import pallas as pl
from jax.experimental.pallas import tpu as pltpu

F32 = jnp.float32
BF16 = jnp.bfloat16
I32 = jnp.int32

D_MODEL = 2048
CHUNK = 64
SGU_BLOCK = 128
SGU_GROUPS = 8
SGU_GROUP_DIM = 256
HEADS = 4
HEAD_DIM = 512
CONV_WIDTH = 4
N_EXPERTS = 32
TOP_K = 4
D_FF = 2048
SWIGLU_LIMIT = 7.0
SWIGLU_ALPHA = 1.702
EPS = 1e-6

LANES = 128
SUBLANES = 8
VMEM_LIMIT = 56 * 1024 * 1024

ZB_U, ZB_V, ZB_Q, ZB_K, ZB_VM, ZB_O, ZB_GA, ZB_GM = range(8)
Z_COLS = 8 * D_MODEL

INPROJ_TM = 512
INPROJ_TN = 1024
SGU_TM = 256
MLSTM_L = 64
MERGE_TM = 256
ROUTE_TM = 256
MOE_SUB = 256
MOE_CAP = 1024
MOE_TF = 256
COMBINE_TM = 128


def _cparams(sem):
    return pltpu.CompilerParams(dimension_semantics=sem, vmem_limit_bytes=VMEM_LIMIT)


def _gelu(x):
    return 0.5 * x * (1.0 + lax.erf(x * (2.0 ** -0.5)))


def _sigmoid(x):
    return 1.0 / (1.0 + jnp.exp(-x))


def _inproj_kernel(x_ref, g_ref, w_ref, b_ref, wif_ref, bif_ref, z_ref, gates_ref, h_scr,
                   *, n_gelu, n_plain):
    j = pl.program_id(1)

    @pl.when(j == 0)
    def _():
        x = x_ref[...]
        ms = jnp.mean(x * x, axis=-1, keepdims=True)
        h = (x * lax.rsqrt(ms + EPS) * g_ref[...]).astype(BF16)
        h_scr[...] = h
        gates_ref[...] = jnp.dot(h, wif_ref[...], preferred_element_type=F32) + bif_ref[...]

    acc = jnp.dot(h_scr[...], w_ref[...], preferred_element_type=F32) + b_ref[...]

    @pl.when(j < n_gelu)
    def _():
        z_ref[...] = _gelu(acc).astype(z_ref.dtype)

    @pl.when((j >= n_gelu) & (j < n_gelu + n_plain))
    def _():
        z_ref[...] = acc.astype(z_ref.dtype)

    @pl.when(j >= n_gelu + n_plain)
    def _():
        z_ref[...] = _sigmoid(acc).astype(z_ref.dtype)


def _in_proj(x, g, w, b, wif, bif):
    t = x.shape[0]
    tm, tn = min(INPROJ_TM, t), INPROJ_TN
    n_gelu = 2 * D_MODEL // tn
    n_plain = 3 * D_MODEL // tn
    return pl.pallas_call(
        functools.partial(_inproj_kernel, n_gelu=n_gelu, n_plain=n_plain),
        grid=(t // tm, Z_COLS // tn),
        in_specs=[
            pl.BlockSpec((tm, D_MODEL), lambda i, j: (i, 0)),
            pl.BlockSpec((1, D_MODEL), lambda i, j: (0, 0)),
            pl.BlockSpec((D_MODEL, tn), lambda i, j: (0, j)),
            pl.BlockSpec((1, tn), lambda i, j: (0, j)),
            pl.BlockSpec((D_MODEL, LANES), lambda i, j: (0, 0)),
            pl.BlockSpec((1, LANES), lambda i, j: (0, 0)),
        ],
        out_specs=[
            pl.BlockSpec((tm, tn), lambda i, j: (i, j)),
            pl.BlockSpec((tm, LANES), lambda i, j: (i, 0)),
        ],
        out_shape=[
            jax.ShapeDtypeStruct((t, Z_COLS), BF16),
            jax.ShapeDtypeStruct((t, LANES), F32),
        ],
        scratch_shapes=[pltpu.VMEM((tm, D_MODEL), BF16)],
        compiler_params=_cparams(("arbitrary", "arbitrary")),
        name="in_proj",
    )(x, g, w, b, wif, bif)


def _sgu_kernel(gu_ref, gv_ref, sga_ref, ws_ref, bsb_ref, lng_ref, lnb_ref, pa_ref, out_ref,
                ya_scr):
    tm = gu_ref.shape[0]
    v = gv_ref[...].astype(F32)
    mu = jnp.mean(v, axis=-1, keepdims=True)
    vc = v - mu
    var = jnp.mean(vc * vc, axis=-1, keepdims=True)
    vln = (vc * lax.rsqrt(var + EPS) * lng_ref[...] + lnb_ref[...]).astype(BF16)

    t_id = lax.broadcasted_iota(I32, (SGU_BLOCK, SGU_BLOCK), 0) // CHUNK
    s_id = lax.broadcasted_iota(I32, (SGU_BLOCK, SGU_BLOCK), 1) // CHUNK
    causal = s_id <= t_id
    for g in range(SGU_GROUPS):
        w = jnp.where(causal, ws_ref[g], 0.0).astype(BF16)
        cols = slice(g * SGU_GROUP_DIM, (g + 1) * SGU_GROUP_DIM)
        bias = bsb_ref[:, cols]
        for blk in range(tm // SGU_BLOCK):
            rows = slice(blk * SGU_BLOCK, (blk + 1) * SGU_BLOCK)
            mixed = jnp.dot(w, vln[rows, cols], preferred_element_type=F32) + bias
            ya_scr[rows, cols] = (gu_ref[rows, cols].astype(F32) * mixed).astype(BF16)

    proj = jnp.dot(ya_scr[...], pa_ref[...], preferred_element_type=F32)
    out_ref[...] = (sga_ref[...].astype(F32) * proj).astype(out_ref.dtype)


def _sgu(z, ws, bsb, lng, lnb, pa):
    t = z.shape[0]
    tm = SGU_TM
    const2 = lambda i: (0, 0)
    return pl.pallas_call(
        _sgu_kernel,
        grid=(t // tm,),
        in_specs=[
            pl.BlockSpec((tm, D_MODEL), lambda i: (i, ZB_U)),
            pl.BlockSpec((tm, D_MODEL), lambda i: (i, ZB_V)),
            pl.BlockSpec((tm, D_MODEL), lambda i: (i, ZB_GA)),
            pl.BlockSpec((SGU_GROUPS, SGU_BLOCK, SGU_BLOCK), lambda i: (0, 0, 0)),
            pl.BlockSpec((SGU_BLOCK, D_MODEL), const2),
            pl.BlockSpec((1, D_MODEL), const2),
            pl.BlockSpec((1, D_MODEL), const2),
            pl.BlockSpec((D_MODEL, D_MODEL), const2),
        ],
        out_specs=pl.BlockSpec((tm, D_MODEL), lambda i: (i, 0)),
        out_shape=jax.ShapeDtypeStruct((t, D_MODEL), BF16),
        scratch_shapes=[pltpu.VMEM((tm, D_MODEL), BF16)],
        compiler_params=_cparams(("arbitrary",)),
        name="sgu",
    )(z, z, z, ws, bsb, lng, lnb, pa)


def _mlstm_kernel(q_ref, k_ref, v_ref, so_ref, gates_ref, wc_ref, bc_ref, gn_ref, out_ref,
                  hist, ct_scr, n_scr, m_scr):
    L = q_ref.shape[0]
    c = pl.program_id(1)

    @pl.when(c == 0)
    def _():
        hist[0:SUBLANES, :] = jnp.zeros((SUBLANES, 2 * D_MODEL), F32)
        ct_scr[...] = jnp.zeros_like(ct_scr)
        n_scr[...] = jnp.zeros_like(n_scr)
        m_scr[...] = jnp.zeros_like(m_scr)

    hist[SUBLANES:SUBLANES + L, 0:D_MODEL] = q_ref[...].astype(F32)
    hist[SUBLANES:SUBLANES + L, D_MODEL:2 * D_MODEL] = k_ref[...].astype(F32)
    qk = bc_ref[...] + wc_ref[CONV_WIDTH - 1:CONV_WIDTH, :] * hist[SUBLANES:SUBLANES + L, :]
    for j in range(CONV_WIDTH - 1):
        off = SUBLANES - (CONV_WIDTH - 1) + j
        qk = qk + wc_ref[j:j + 1, :] * hist[off:off + L, :]
    qk = qk * _sigmoid(qk)
    hist[0:SUBLANES, :] = hist[L:L + SUBLANES, :]

    gts = gates_ref[...]
    lf = jnp.minimum(gts, 0.0) - jnp.log(1.0 + jnp.exp(-jnp.abs(gts)))
    row = lax.broadcasted_iota(I32, (L, L), 0)
    col = lax.broadcasted_iota(I32, (L, L), 1)
    tril = col <= row
    tri = jnp.where(tril, 1.0, 0.0).astype(BF16)
    lf_hi = lf.astype(BF16)
    lf_lo = (lf - lf_hi.astype(F32)).astype(BF16)
    bcum = (jnp.dot(tri, lf_hi, preferred_element_type=F32)
            + jnp.dot(tri, lf_lo, preferred_element_type=F32))
    bcum_t = bcum.T
    gts_t = gts.T

    scale = HEAD_DIM ** -0.5
    for h in range(HEADS):
        cols = slice(h * HEAD_DIM, (h + 1) * HEAD_DIM)
        q = qk[:, h * HEAD_DIM:(h + 1) * HEAD_DIM] * scale
        k = qk[:, D_MODEL + h * HEAD_DIM:D_MODEL + (h + 1) * HEAD_DIM]
        qb = q.astype(BF16)
        kb = k.astype(BF16)
        vb = v_ref[:, cols]
        bc = bcum[:, HEADS + h:HEADS + h + 1]
        bc_r = bcum_t[HEADS + h:HEADS + h + 1, :]
        ig_c = gts[:, h:h + 1]
        ig_r = gts_t[h:h + 1, :]
        m_st = m_scr[h]
        n_st = n_scr[h]

        dlog = jnp.where(tril, bc - bc_r + ig_r, -jnp.inf)
        m_inter = bc + m_st
        m_t = jnp.maximum(m_inter, jnp.max(dlog, axis=-1, keepdims=True))
        p = jnp.exp(dlog - m_t)
        s = lax.dot_general(qb, kb, (((1,), (1,)), ((), ())), preferred_element_type=F32)
        sc = s * p
        inter = jnp.exp(m_inter - m_t)
        ctb = ct_scr[h].astype(BF16)
        num = (jnp.dot(sc.astype(BF16), vb, preferred_element_type=F32)
               + inter * jnp.dot(qb, ctb, preferred_element_type=F32))
        den = (jnp.sum(sc, axis=-1, keepdims=True)
               + inter * jnp.sum(q * n_st, axis=-1, keepdims=True))
        hh = num / jnp.maximum(jnp.abs(den), jnp.exp(-m_t))
        hn = hh * lax.rsqrt(jnp.mean(hh * hh, axis=-1, keepdims=True) + EPS) * gn_ref[:, cols]
        out_ref[:, cols] = (hn * so_ref[:, cols].astype(F32)).astype(out_ref.dtype)

        g_tot = bc[L - 1:L, :]
        a = g_tot - bc + ig_c
        m_new = jnp.maximum(g_tot + m_st, jnp.max(a, axis=0, keepdims=True))
        wa = jnp.exp(a - m_new)
        decay = jnp.exp(g_tot + m_st - m_new)
        wv = (wa * vb.astype(F32)).astype(BF16)
        upd = jnp.dot(k.T.astype(BF16), wv, preferred_element_type=F32)
        ct_scr[h] = decay * ct_scr[h] + upd
        n_scr[h] = decay * n_st + jnp.sum(wa * k, axis=0, keepdims=True)
        m_scr[h] = m_new


def _mlstm(z, gates, wc, bc, gn, bsz, seq):
    t = z.shape[0]
    L = MLSTM_L
    nc = seq // L
    zspec = lambda blk: pl.BlockSpec((L, D_MODEL), lambda b, c: (b * nc + c, blk))
    const2 = lambda b, c: (0, 0)
    return pl.pallas_call(
        _mlstm_kernel,
        grid=(bsz, nc),
        in_specs=[
            zspec(ZB_Q), zspec(ZB_K), zspec(ZB_VM), zspec(ZB_O),
            pl.BlockSpec((L, LANES), lambda b, c: (b * nc + c, 0)),
            pl.BlockSpec((CONV_WIDTH, 2 * D_MODEL), const2),
            pl.BlockSpec((1, 2 * D_MODEL), const2),
            pl.BlockSpec((1, D_MODEL), const2),
        ],
        out_specs=pl.BlockSpec((L, D_MODEL), lambda b, c: (b * nc + c, 0)),
        out_shape=jax.ShapeDtypeStruct((t, D_MODEL), BF16),
        scratch_shapes=[
            pltpu.VMEM((L + 2 * SUBLANES, 2 * D_MODEL), F32),
            pltpu.VMEM((HEADS, HEAD_DIM, HEAD_DIM), F32),
            pltpu.VMEM((HEADS, 1, HEAD_DIM), F32),
            pltpu.VMEM((HEADS, 1, 1), F32),
        ],
        compiler_params=_cparams(("arbitrary", "arbitrary")),
        name="mlstm",
    )(z, z, z, z, gates, wc, bc, gn)


def _merge_kernel(ym_ref, a_ref, sgm_ref, x_ref, pm_ref, wo_ref, g2_ref, wrh_ref, wrl_ref, br_ref,
                  x1_ref, h2_ref, lg_ref):
    proj = jnp.dot(ym_ref[...], pm_ref[...], preferred_element_type=F32)
    merged = a_ref[...].astype(F32) + sgm_ref[...].astype(F32) * proj
    x1 = x_ref[...] + jnp.dot(merged.astype(BF16), wo_ref[...], preferred_element_type=F32)
    x1_ref[...] = x1
    h2 = x1 * lax.rsqrt(jnp.mean(x1 * x1, axis=-1, keepdims=True) + EPS) * g2_ref[...]
    h2_ref[...] = h2
    h_hi = h2.astype(BF16)
    h_lo = (h2 - h_hi.astype(F32)).astype(BF16)
    lg_ref[...] = (jnp.dot(h_hi, wrh_ref[...], preferred_element_type=F32)
                   + (jnp.dot(h_lo, wrh_ref[...], preferred_element_type=F32)
                      + jnp.dot(h_hi, wrl_ref[...], preferred_element_type=F32))
                   + br_ref[...])


def _merge(ym, a, z, x, pm, wo, g2, wr_hi, wr_lo, br):
    t = x.shape[0]
    tm = MERGE_TM
    const2 = lambda i: (0, 0)
    row = lambda i: (i, 0)
    return pl.pallas_call(
        _merge_kernel,
        grid=(t // tm,),
        in_specs=[
            pl.BlockSpec((tm, D_MODEL), row),
            pl.BlockSpec((tm, D_MODEL), row),
            pl.BlockSpec((tm, D_MODEL), lambda i: (i, ZB_GM)),
            pl.BlockSpec((tm, D_MODEL), row),
            pl.BlockSpec((D_MODEL, D_MODEL), const2),
            pl.BlockSpec((D_MODEL, D_MODEL), const2),
            pl.BlockSpec((1, D_MODEL), const2),
            pl.BlockSpec((D_MODEL, LANES), const2),
            pl.BlockSpec((D_MODEL, LANES), const2),
            pl.BlockSpec((1, LANES), const2),
        ],
        out_specs=[
            pl.BlockSpec((tm, D_MODEL), row),
            pl.BlockSpec((tm, D_MODEL), row),
            pl.BlockSpec((tm, LANES), row),
        ],
        out_shape=[
            jax.ShapeDtypeStruct((t, D_MODEL), F32),
            jax.ShapeDtypeStruct((t, D_MODEL), F32),
            jax.ShapeDtypeStruct((t, LANES), F32),
        ],
        compiler_params=_cparams(("arbitrary",)),
        name="merge",
    )(ym, a, z, x, pm, wo, g2, wr_hi, wr_lo, br)


def _route_kernel(lg_ref, info_ref, gate_ref, cnt_ref, cnt_scr):
    tm = lg_ref.shape[0]
    i = pl.program_id(0)

    @pl.when(i == 0)
    def _():
        cnt_scr[...] = jnp.zeros_like(cnt_scr)

    lane = lax.broadcasted_iota(I32, (tm, LANES), 1)
    lane_f = lane.astype(F32)
    vals = jnp.where(lane < N_EXPERTS, lg_ref[...], -jnp.inf)
    tops, ids, hots = [], [], []
    for _ in range(TOP_K):
        m = jnp.max(vals, axis=-1, keepdims=True)
        idx_f = jnp.min(jnp.where(vals == m, lane_f, float(LANES)), axis=-1, keepdims=True)
        idx = idx_f.astype(I32)
        hot = lane == idx
        tops.append(m)
        ids.append(idx)
        hots.append(hot)
        vals = jnp.where(hot, -jnp.inf, vals)

    exps = [jnp.exp(tv - tops[0]) for tv in tops]
    denom = exps[0] + exps[1] + exps[2] + exps[3]

    sel = jnp.zeros((tm, LANES), F32)
    for hot in hots:
        sel = sel + jnp.where(hot, 1.0, 0.0)
    r = lax.broadcasted_iota(I32, (tm, tm), 0)
    c = lax.broadcasted_iota(I32, (tm, tm), 1)
    strict = jnp.where(c < r, 1.0, 0.0).astype(BF16)
    ahead = jnp.dot(strict, sel.astype(BF16), preferred_element_type=F32) + cnt_scr[...]

    info = jnp.zeros((tm, LANES), I32)
    gate = jnp.zeros((tm, LANES), F32)
    for k in range(TOP_K):
        pos = jnp.sum(jnp.where(hots[k], ahead, 0.0), axis=-1, keepdims=True).astype(I32)
        info = jnp.where(lane == k, ids[k], info)
        info = jnp.where(lane == TOP_K + k, pos, info)
        gate = jnp.where(lane == k, exps[k] / denom, gate)
    info_ref[...] = info
    gate_ref[...] = gate
    cnt_scr[...] = cnt_scr[...] + jnp.sum(sel, axis=0, keepdims=True)
    cnt_ref[...] = jnp.broadcast_to(cnt_scr[...], cnt_ref.shape)


def _route(logits):
    t = logits.shape[0]
    tm = ROUTE_TM
    row = lambda i: (i, 0)
    return pl.pallas_call(
        _route_kernel,
        grid=(t // tm,),
        in_specs=[pl.BlockSpec((tm, LANES), row)],
        out_specs=[
            pl.BlockSpec((tm, LANES), row),
            pl.BlockSpec((tm, LANES), row),
            pl.BlockSpec((SUBLANES, LANES), lambda i: (0, 0)),
        ],
        out_shape=[
            jax.ShapeDtypeStruct((t, LANES), I32),
            jax.ShapeDtypeStruct((t, LANES), F32),
            jax.ShapeDtypeStruct((SUBLANES, LANES), F32),
        ],
        scratch_shapes=[pltpu.VMEM((1, LANES), F32)],
        compiler_params=_cparams(("arbitrary",)),
        name="route",
    )(logits)


def _invperm_kernel(dest_ref, cnt_ref, start_ref, out_ref, *, n_assign):
    def pad_expert(e, carry):
        def pad_row(r, c2):
            out_ref[r] = 0
            return c2

        return lax.fori_loop(start_ref[e] + cnt_ref[e], start_ref[e + 1], pad_row, carry)

    lax.fori_loop(0, N_EXPERTS, pad_expert, 0)

    def place(a, carry):
        out_ref[dest_ref[a]] = a // TOP_K
        return carry

    lax.fori_loop(0, n_assign, place, 0)


def _invperm(dest, counts, starts, n_rows):
    smem = pl.BlockSpec(memory_space=pltpu.SMEM)
    return pl.pallas_call(
        functools.partial(_invperm_kernel, n_assign=dest.shape[0]),
        in_specs=[smem, smem, smem],
        out_specs=smem,
        out_shape=jax.ShapeDtypeStruct((n_rows,), I32),
        name="invperm",
    )(dest, counts, starts)


def _experts_kernel(tile_e_ref, tile_occ_ref, row_tok_ref,
                    h2_hbm, wg_ref, wu_ref, bg_ref, bu_ref, wd_ref, bd_ref, out_ref,
                    xbuf, sem, *, n_tiles):
    i = pl.program_id(0)
    f = pl.program_id(1)
    occ = tile_occ_ref[i]
    slot = i % 2

    def row_copy(tile, s, sub, r):
        tok = row_tok_ref[tile * MOE_CAP + sub * MOE_SUB + r]
        return pltpu.make_async_copy(h2_hbm.at[pl.ds(tok, 1)],
                                     xbuf.at[s, pl.ds(sub * MOE_SUB + r, 1)], sem.at[s])

    def start_gather(tile, s):
        def per_sub(sub, carry):
            def per_row(r, c2):
                row_copy(tile, s, sub, r).start()
                return c2
            return lax.fori_loop(0, MOE_SUB, per_row, carry)
        lax.fori_loop(0, tile_occ_ref[tile], per_sub, 0)

    def wait_gather(tile, s):
        def per_sub(sub, carry):
            def per_row(r, c2):
                row_copy(tile, s, sub, r).wait()
                return c2
            return lax.fori_loop(0, MOE_SUB, per_row, carry)
        lax.fori_loop(0, tile_occ_ref[tile], per_sub, 0)

    @pl.when(f == 0)
    def _():
        @pl.when(i == 0)
        def _():
            start_gather(0, 0)

        wait_gather(i, slot)

        @pl.when(i + 1 < n_tiles)
        def _():
            start_gather(i + 1, 1 - slot)

    wg = wg_ref[...].astype(BF16)
    wu = wu_ref[...].astype(BF16)
    wd = wd_ref[...].astype(BF16)

    def per_sub(sub, carry):
        rows = pl.ds(pl.multiple_of(sub * MOE_SUB, MOE_SUB), MOE_SUB)
        xb = xbuf[slot, rows, :].astype(BF16)
        gate = jnp.dot(xb, wg, preferred_element_type=F32) + bg_ref[...]
        up = jnp.dot(xb, wu, preferred_element_type=F32) + bu_ref[...]
        gate = jnp.minimum(gate, SWIGLU_LIMIT)
        up = jnp.clip(up, -SWIGLU_LIMIT, SWIGLU_LIMIT)
        act = (up + 1.0) * (gate * _sigmoid(SWIGLU_ALPHA * gate))
        y = jnp.dot(act.astype(BF16), wd, preferred_element_type=F32)

        @pl.when(f == 0)
        def _():
            out_ref[rows, :] = y + bd_ref[...]

        @pl.when(f > 0)
        def _():
            out_ref[rows, :] = out_ref[rows, :] + y

        return carry

    lax.fori_loop(0, occ, per_sub, 0)

    @pl.when(f == 0)
    def _():
        def zero_sub(sub, carry):
            rows = pl.ds(pl.multiple_of(sub * MOE_SUB, MOE_SUB), MOE_SUB)
            out_ref[rows, :] = jnp.zeros((MOE_SUB, D_MODEL), F32)
            return carry
        lax.fori_loop(occ, MOE_CAP // MOE_SUB, zero_sub, 0)


def _experts(tile_e, tile_occ, row_tok, h2, wgu, bgu, wd, bd, n_tiles):
    nfb = D_FF // MOE_TF
    grid_spec = pltpu.PrefetchScalarGridSpec(
        num_scalar_prefetch=3,
        grid=(n_tiles, nfb),
        in_specs=[
            pl.BlockSpec(memory_space=pl.ANY),
            pl.BlockSpec((None, D_MODEL, MOE_TF), lambda i, f, te, to, rt: (te[i], 0, f)),
            pl.BlockSpec((None, D_MODEL, MOE_TF), lambda i, f, te, to, rt: (te[i], 0, nfb + f)),
            pl.BlockSpec((None, 1, MOE_TF), lambda i, f, te, to, rt: (te[i], 0, f)),
            pl.BlockSpec((None, 1, MOE_TF), lambda i, f, te, to, rt: (te[i], 0, nfb + f)),
            pl.BlockSpec((None, MOE_TF, D_MODEL), lambda i, f, te, to, rt: (te[i], f, 0)),
            pl.BlockSpec((None, 1, D_MODEL), lambda i, f, te, to, rt: (te[i], 0, 0)),
        ],
        out_specs=pl.BlockSpec((MOE_CAP, D_MODEL), lambda i, f, te, to, rt: (i, 0)),
        scratch_shapes=[
            pltpu.VMEM((2, MOE_CAP, D_MODEL), F32),
            pltpu.SemaphoreType.DMA((2,)),
        ],
    )
    return pl.pallas_call(
        functools.partial(_experts_kernel, n_tiles=n_tiles),
        grid_spec=grid_spec,
        out_shape=jax.ShapeDtypeStruct((n_tiles * MOE_CAP, D_MODEL), F32),
        compiler_params=_cparams(("arbitrary", "arbitrary")),
        name="experts",
    )(tile_e, tile_occ, row_tok, h2, wgu, wgu, bgu, bgu, wd, bd)


def _combine_kernel(dest_ref, ys_hbm, x1_ref, gate_ref, gf_ref, out_ref, buf, sem, *, n_tiles):
    tm = x1_ref.shape[0]
    i = pl.program_id(0)
    slot = i % 2

    def row_copy(tile, s, r, k):
        d = dest_ref[(tile * tm + r) * TOP_K + k]
        return pltpu.make_async_copy(ys_hbm.at[pl.ds(d, 1)], buf.at[s, k, pl.ds(r, 1)], sem.at[s])

    def start_gather(tile, s):
        def per_row(r, carry):
            for k in range(TOP_K):
                row_copy(tile, s, r, k).start()
            return carry
        lax.fori_loop(0, tm, per_row, 0)

    def wait_gather(tile, s):
        def per_row(r, carry):
            for k in range(TOP_K):
                row_copy(tile, s, r, k).wait()
            return carry
        lax.fori_loop(0, tm, per_row, 0)

    @pl.when(i == 0)
    def _():
        start_gather(0, 0)

    wait_gather(i, slot)

    @pl.when(i + 1 < n_tiles)
    def _():
        start_gather(i + 1, 1 - slot)

    x2 = x1_ref[...]
    for k in range(TOP_K):
        x2 = x2 + gate_ref[:, k:k + 1] * buf[slot, k]
    out_ref[...] = x2 * lax.rsqrt(jnp.mean(x2 * x2, axis=-1, keepdims=True) + EPS) * gf_ref[...]


def _combine(dest, ys, x1, gate, gf):
    t = x1.shape[0]
    tm = COMBINE_TM
    n_tiles = t // tm
    grid_spec = pltpu.PrefetchScalarGridSpec(
        num_scalar_prefetch=1,
        grid=(n_tiles,),
        in_specs=[
            pl.BlockSpec(memory_space=pl.ANY),
            pl.BlockSpec((tm, D_MODEL), lambda i, d: (i, 0)),
            pl.BlockSpec((tm, LANES), lambda i, d: (i, 0)),
            pl.BlockSpec((1, D_MODEL), lambda i, d: (0, 0)),
        ],
        out_specs=pl.BlockSpec((tm, D_MODEL), lambda i, d: (i, 0)),
        scratch_shapes=[
            pltpu.VMEM((2, TOP_K, tm, D_MODEL), F32),
            pltpu.SemaphoreType.DMA((2,)),
        ],
    )
    return pl.pallas_call(
        functools.partial(_combine_kernel, n_tiles=n_tiles),
        grid_spec=grid_spec,
        out_shape=jax.ShapeDtypeStruct((t, D_MODEL), F32),
        compiler_params=_cparams(("arbitrary",)),
        name="combine",
    )(dest, ys, x1, gate, gf)


def _moe_tiles(n_tokens):
    return N_EXPERTS + (n_tokens * TOP_K) // MOE_CAP


def _layer(x2d, bsz, seq, g_norm_mix, w_in, b_in, w_conv, b_conv, w_sgu, b_sgu, g_sgu_ln, b_sgu_ln,
           g_mlstm_norm, w_proj_sgu, w_proj_mlstm, w_out, g_norm_moe, w_router, b_router,
           w_gate_up, b_gate_up, w_down, b_down, g_out):
    t = x2d.shape[0]
    n_main = 6 * D_MODEL
    gate_off = n_main + 2 * HEADS

    w_all = jnp.concatenate([w_in[:, :n_main], w_in[:, gate_off:]], axis=1).astype(BF16)
    b_all = jnp.concatenate([b_in[:n_main], b_in[gate_off:]])[None, :]
    pad = LANES - 2 * HEADS
    w_if = jnp.pad(w_in[:, n_main:gate_off], ((0, 0), (0, pad))).astype(BF16)
    b_if = jnp.pad(b_in[n_main:gate_off], (0, pad))[None, :]
    bsb = jnp.broadcast_to(b_sgu.T[:, :, None], (SGU_BLOCK, SGU_GROUPS, SGU_GROUP_DIM))
    bsb = bsb.reshape(SGU_BLOCK, D_MODEL)
    w_r = jnp.pad(w_router, ((0, 0), (0, LANES - N_EXPERTS)))
    w_r_hi = w_r.astype(BF16)
    w_r_lo = (w_r - w_r_hi.astype(F32)).astype(BF16)
    b_r = jnp.pad(b_router, (0, LANES - N_EXPERTS))[None, :]

    z, gates = _in_proj(x2d, g_norm_mix[None, :], w_all, b_all, w_if, b_if)
    a = _sgu(z, w_sgu, bsb, g_sgu_ln[None, :], b_sgu_ln[None, :], w_proj_sgu.astype(BF16))
    ym = _mlstm(z, gates, w_conv, b_conv[None, :], g_mlstm_norm[None, :], bsz, seq)
    x1, h2, logits = _merge(ym, a, z, x2d, w_proj_mlstm.astype(BF16), w_out.astype(BF16),
                            g_norm_moe[None, :], w_r_hi, w_r_lo, b_r)
    info, gate, cnt = _route(logits)

    n_tiles = _moe_tiles(t)
    counts = cnt[0, :N_EXPERTS].astype(I32)
    tiles_per = (counts + MOE_CAP - 1) // MOE_CAP
    tile_end = jnp.cumsum(tiles_per)
    tile_start = tile_end - tiles_per
    row_start = tile_start * MOE_CAP
    tile_ids = jnp.arange(n_tiles, dtype=I32)
    n_active = tile_end[-1]
    last = jnp.maximum(n_active - 1, 0)
    tile_e_all = jnp.searchsorted(tile_end, tile_ids, side="right").astype(I32)
    tile_e = jnp.where(tile_ids < n_active, tile_e_all, tile_e_all[last])
    tile_e = jnp.minimum(tile_e, N_EXPERTS - 1)
    rows_in = counts[tile_e] - (tile_ids - tile_start[tile_e]) * MOE_CAP
    occ = jnp.clip((rows_in + MOE_SUB - 1) // MOE_SUB, 0, MOE_CAP // MOE_SUB)
    tile_occ = jnp.where(tile_ids < n_active, occ, 0).astype(I32)
    dest = (row_start[info[:, :TOP_K]] + info[:, TOP_K:2 * TOP_K]).reshape(-1)

    n_rows = n_tiles * MOE_CAP
    row_bounds = jnp.concatenate([row_start, jnp.full((1,), n_rows, I32)]).astype(I32)
    row_tok = _invperm(dest, counts, row_bounds, n_rows)
    ys = _experts(tile_e, tile_occ, row_tok, h2, w_gate_up, b_gate_up[:, None, :],
                  w_down, b_down[:, None, :], n_tiles)
    return _combine(dest, ys, x1, gate, g_out[None, :])


def kernel(x, g_norm_mix, w_in, b_in, w_conv, b_conv, w_sgu, b_sgu, g_sgu_ln, b_sgu_ln, g_mlstm_norm,
           w_proj_sgu, w_proj_mlstm, w_out, g_norm_moe, w_router, b_router, w_gate_up, b_gate_up,
           w_down, b_down, g_final):
    bsz, seq, d = x.shape
    assert d == D_MODEL and w_in.shape[0] == 1, "single-layer block with d_model 2048"
    out = _layer(x.reshape(bsz * seq, d), bsz, seq, g_norm_mix[0], w_in[0], b_in[0], w_conv[0],
                 b_conv[0], w_sgu[0], b_sgu[0], g_sgu_ln[0], b_sgu_ln[0], g_mlstm_norm[0],
                 w_proj_sgu[0], w_proj_mlstm[0], w_out[0], g_norm_moe[0], w_router[0], b_router[0],
                 w_gate_up[0], b_gate_up[0], w_down[0], b_down[0], g_final)
    return out.reshape(bsz, seq, d)
```

```python
import functools

import jax
import jax.numpy as jnp
from jax import lax
from jax.experimental import pallas as pl
from jax.experimental.pallas import tpu as pltpu

F32 = jnp.float32
BF16 = jnp.bfloat16
I32 = jnp.int32

D_MODEL = 2048
CHUNK = 64
SGU_BLOCK = 128
SGU_GROUPS = 8
SGU_GROUP_DIM = 256
HEADS = 4
HEAD_DIM = 512
CONV_WIDTH = 4
N_EXPERTS = 32
TOP_K = 4
D_FF = 2048
SWIGLU_LIMIT = 7.0
SWIGLU_ALPHA = 1.702
EPS = 1e-6

LANES = 128
SUBLANES = 8
VMEM_LIMIT = 56 * 1024 * 1024

ZB_U, ZB_V, ZB_Q, ZB_K, ZB_VM, ZB_O = range(6)
ZG_A, ZG_M = range(2)
N_MAIN = 6 * D_MODEL
GATE_OFF = N_MAIN + 2 * HEADS

NORM_TM = 512
INPROJ_TM = 1024
INPROJ_TN = 1024
SGU_TM = 256
MLSTM_L = 64
MERGE_TM = 256
ROUTE_TM = 256
DISPATCH_TM = 256
MOE_SUB = 256
MOE_CAP = 1024
MOE_SUBS = MOE_CAP // MOE_SUB
MOE_TF = 256
MOE_NF = D_FF // MOE_TF
MOE_TN = 512
MOE_NN = D_MODEL // MOE_TN
COMBINE_TM = 128


def _cparams(sem):
    return pltpu.CompilerParams(dimension_semantics=sem, vmem_limit_bytes=VMEM_LIMIT)


def _gelu(x):
    return 0.5 * x * (1.0 + lax.erf(x * (2.0 ** -0.5)))


def _sigmoid(x):
    return 1.0 / (1.0 + jnp.exp(-x))


def _norm_gates_kernel(x_ref, g_ref, wif_ref, bif_ref, h_ref, gates_ref):
    x = x_ref[...]
    ms = jnp.mean(x * x, axis=-1, keepdims=True)
    h = (x * lax.rsqrt(ms + EPS) * g_ref[...]).astype(BF16)
    h_ref[...] = h
    gates_ref[...] = jnp.dot(h, wif_ref[...], preferred_element_type=F32) + bif_ref[...]


def _norm_gates(x, g, wif, bif):
    t = x.shape[0]
    tm = min(NORM_TM, t)
    row = lambda i: (i, 0)
    const = lambda i: (0, 0)
    return pl.pallas_call(
        _norm_gates_kernel,
        grid=(t // tm,),
        in_specs=[
            pl.BlockSpec((tm, D_MODEL), row),
            pl.BlockSpec((1, D_MODEL), const),
            pl.BlockSpec((D_MODEL, LANES), const),
            pl.BlockSpec((1, LANES), const),
        ],
        out_specs=[pl.BlockSpec((tm, D_MODEL), row), pl.BlockSpec((tm, LANES), row)],
        out_shape=[jax.ShapeDtypeStruct((t, D_MODEL), BF16), jax.ShapeDtypeStruct((t, LANES), F32)],
        compiler_params=_cparams(("arbitrary",)),
        name="norm_gates",
    )(x, g, wif, bif)


def _inproj_kernel(h_ref, w_ref, b_ref, z_ref, w_scr, *, n_gelu, n_plain):
    j = pl.program_id(0)

    @pl.when(pl.program_id(1) == 0)
    def _():
        w_scr[...] = w_ref[...].astype(BF16)

    acc = jnp.dot(h_ref[...], w_scr[...], preferred_element_type=F32) + b_ref[...]

    @pl.when(j < n_gelu)
    def _():
        z_ref[...] = _gelu(acc).astype(z_ref.dtype)

    @pl.when((j >= n_gelu) & (j < n_gelu + n_plain))
    def _():
        z_ref[...] = acc.astype(z_ref.dtype)

    @pl.when(j >= n_gelu + n_plain)
    def _():
        z_ref[...] = _sigmoid(acc).astype(z_ref.dtype)


def _in_proj(h, w, b, n_cols, n_gelu_cols, n_plain_cols, name):
    t = h.shape[0]
    tm, tn = min(INPROJ_TM, t), INPROJ_TN
    return pl.pallas_call(
        functools.partial(_inproj_kernel, n_gelu=n_gelu_cols // tn, n_plain=n_plain_cols // tn),
        grid=(n_cols // tn, t // tm),
        in_specs=[
            pl.BlockSpec((tm, D_MODEL), lambda j, i: (i, 0)),
            pl.BlockSpec((D_MODEL, tn), lambda j, i: (0, j)),
            pl.BlockSpec((1, tn), lambda j, i: (0, j)),
        ],
        out_specs=pl.BlockSpec((tm, tn), lambda j, i: (i, j)),
        out_shape=jax.ShapeDtypeStruct((t, n_cols), BF16),
        scratch_shapes=[pltpu.VMEM((D_MODEL, tn), BF16)],
        compiler_params=_cparams(("arbitrary", "arbitrary")),
        name=name,
    )(h, w, b)


def _sgu_kernel(gu_ref, gv_ref, sga_ref, ws_ref, bsb_ref, lng_ref, lnb_ref, pa_ref, out_ref,
                ya_scr):
    tm = gu_ref.shape[0]
    v = gv_ref[...].astype(F32)
    mu = jnp.mean(v, axis=-1, keepdims=True)
    vc = v - mu
    var = jnp.mean(vc * vc, axis=-1, keepdims=True)
    vln = (vc * lax.rsqrt(var + EPS) * lng_ref[...] + lnb_ref[...]).astype(BF16)

    t_id = lax.broadcasted_iota(I32, (SGU_BLOCK, SGU_BLOCK), 0) // CHUNK
    s_id = lax.broadcasted_iota(I32, (SGU_BLOCK, SGU_BLOCK), 1) // CHUNK
    causal = s_id <= t_id
    for g in range(SGU_GROUPS):
        w = jnp.where(causal, ws_ref[g], 0.0).astype(BF16)
        cols = slice(g * SGU_GROUP_DIM, (g + 1) * SGU_GROUP_DIM)
        bias = bsb_ref[:, cols]
        for blk in range(tm // SGU_BLOCK):
            rows = slice(blk * SGU_BLOCK, (blk + 1) * SGU_BLOCK)
            mixed = jnp.dot(w, vln[rows, cols], preferred_element_type=F32) + bias
            ya_scr[rows, cols] = (gu_ref[rows, cols].astype(F32) * mixed).astype(BF16)

    proj = jnp.dot(ya_scr[...], pa_ref[...], preferred_element_type=F32)
    out_ref[...] = (sga_ref[...].astype(F32) * proj).astype(out_ref.dtype)


def _sgu(zm, zg, ws, bsb, lng, lnb, pa):
    t = zm.shape[0]
    tm = SGU_TM
    const2 = lambda i: (0, 0)
    return pl.pallas_call(
        _sgu_kernel,
        grid=(t // tm,),
        in_specs=[
            pl.BlockSpec((tm, D_MODEL), lambda i: (i, ZB_U)),
            pl.BlockSpec((tm, D_MODEL), lambda i: (i, ZB_V)),
            pl.BlockSpec((tm, D_MODEL), lambda i: (i, ZG_A)),
            pl.BlockSpec((SGU_GROUPS, SGU_BLOCK, SGU_BLOCK), lambda i: (0, 0, 0)),
            pl.BlockSpec((SGU_BLOCK, D_MODEL), const2),
            pl.BlockSpec((1, D_MODEL), const2),
            pl.BlockSpec((1, D_MODEL), const2),
            pl.BlockSpec((D_MODEL, D_MODEL), const2),
        ],
        out_specs=pl.BlockSpec((tm, D_MODEL), lambda i: (i, 0)),
        out_shape=jax.ShapeDtypeStruct((t, D_MODEL), BF16),
        scratch_shapes=[pltpu.VMEM((tm, D_MODEL), BF16)],
        compiler_params=_cparams(("arbitrary",)),
        name="sgu",
    )(zm, zm, zg, ws, bsb, lng, lnb, pa)


def _mlstm_kernel(q_ref, k_ref, v_ref, so_ref, gates_ref, wc_ref, bc_ref, gn_ref, out_ref,
                  hist, ct_scr, n_scr, m_scr):
    L = q_ref.shape[0]
    c = pl.program_id(1)

    @pl.when(c == 0)
    def _():
        hist[0:SUBLANES, :] = jnp.zeros((SUBLANES, 2 * D_MODEL), F32)
        ct_scr[...] = jnp.zeros_like(ct_scr)
        n_scr[...] = jnp.zeros_like(n_scr)
        m_scr[...] = jnp.zeros_like(m_scr)

    hist[SUBLANES:SUBLANES + L, 0:D_MODEL] = q_ref[...].astype(F32)
    hist[SUBLANES:SUBLANES + L, D_MODEL:2 * D_MODEL] = k_ref[...].astype(F32)
    qk = bc_ref[...] + wc_ref[CONV_WIDTH - 1:CONV_WIDTH, :] * hist[SUBLANES:SUBLANES + L, :]
    for j in range(CONV_WIDTH - 1):
        off = SUBLANES - (CONV_WIDTH - 1) + j
        qk = qk + wc_ref[j:j + 1, :] * hist[off:off + L, :]
    qk = qk * _sigmoid(qk)
    hist[0:SUBLANES, :] = hist[L:L + SUBLANES, :]

    gts = gates_ref[...]
    lf = jnp.minimum(gts, 0.0) - jnp.log(1.0 + jnp.exp(-jnp.abs(gts)))
    row = lax.broadcasted_iota(I32, (L, L), 0)
    col = lax.broadcasted_iota(I32, (L, L), 1)
    tril = col <= row
    tri = jnp.where(tril, 1.0, 0.0).astype(BF16)
    lf_hi = lf.astype(BF16)
    lf_lo = (lf - lf_hi.astype(F32)).astype(BF16)
    bcum = (jnp.dot(tri, lf_hi, preferred_element_type=F32)
            + jnp.dot(tri, lf_lo, preferred_element_type=F32))
    bcum_t = bcum.T
    gts_t = gts.T

    scale = HEAD_DIM ** -0.5
    for h in range(HEADS):
        cols = slice(h * HEAD_DIM, (h + 1) * HEAD_DIM)
        q = qk[:, h * HEAD_DIM:(h + 1) * HEAD_DIM] * scale
        k = qk[:, D_MODEL + h * HEAD_DIM:D_MODEL + (h + 1) * HEAD_DIM]
        qb = q.astype(BF16)
        kb = k.astype(BF16)
        vb = v_ref[:, cols]
        bc = bcum[:, HEADS + h:HEADS + h + 1]
        bc_r = bcum_t[HEADS + h:HEADS + h + 1, :]
        ig_c = gts[:, h:h + 1]
        ig_r = gts_t[h:h + 1, :]
        m_st = m_scr[h]
        n_st = n_scr[h]

        dlog = jnp.where(tril, bc - bc_r + ig_r, -jnp.inf)
        m_inter = bc + m_st
        m_t = jnp.maximum(m_inter, jnp.max(dlog, axis=-1, keepdims=True))
        p = jnp.exp(dlog - m_t)
        s = lax.dot_general(qb, kb, (((1,), (1,)), ((), ())), preferred_element_type=F32)
        sc = s * p
        inter = jnp.exp(m_inter - m_t)
        ctb = ct_scr[h].astype(BF16)
        num = (jnp.dot(sc.astype(BF16), vb, preferred_element_type=F32)
               + inter * jnp.dot(qb, ctb, preferred_element_type=F32))
        den = (jnp.sum(sc, axis=-1, keepdims=True)
               + inter * jnp.sum(q * n_st, axis=-1, keepdims=True))
        hh = num / jnp.maximum(jnp.abs(den), jnp.exp(-m_t))
        hn = hh * lax.rsqrt(jnp.mean(hh * hh, axis=-1, keepdims=True) + EPS) * gn_ref[:, cols]
        out_ref[:, cols] = (hn * so_ref[:, cols].astype(F32)).astype(out_ref.dtype)

        g_tot = bc[L - 1:L, :]
        a = g_tot - bc + ig_c
        m_new = jnp.maximum(g_tot + m_st, jnp.max(a, axis=0, keepdims=True))
        wa = jnp.exp(a - m_new)
        decay = jnp.exp(g_tot + m_st - m_new)
        wv = (wa * vb.astype(F32)).astype(BF16)
        upd = jnp.dot(k.T.astype(BF16), wv, preferred_element_type=F32)
        ct_scr[h] = decay * ct_scr[h] + upd
        n_scr[h] = decay * n_st + jnp.sum(wa * k, axis=0, keepdims=True)
        m_scr[h] = m_new


def _mlstm(zm, gates, wc, bc, gn, bsz, seq):
    t = zm.shape[0]
    L = MLSTM_L
    nc = seq // L
    zspec = lambda blk: pl.BlockSpec((L, D_MODEL), lambda b, c: (b * nc + c, blk))
    const2 = lambda b, c: (0, 0)
    return pl.pallas_call(
        _mlstm_kernel,
        grid=(bsz, nc),
        in_specs=[
            zspec(ZB_Q), zspec(ZB_K), zspec(ZB_VM), zspec(ZB_O),
            pl.BlockSpec((L, LANES), lambda b, c: (b * nc + c, 0)),
            pl.BlockSpec((CONV_WIDTH, 2 * D_MODEL), const2),
            pl.BlockSpec((1, 2 * D_MODEL), const2),
            pl.BlockSpec((1, D_MODEL), const2),
        ],
        out_specs=pl.BlockSpec((L, D_MODEL), lambda b, c: (b * nc + c, 0)),
        out_shape=jax.ShapeDtypeStruct((t, D_MODEL), BF16),
        scratch_shapes=[
            pltpu.VMEM((L + 2 * SUBLANES, 2 * D_MODEL), F32),
            pltpu.VMEM((HEADS, HEAD_DIM, HEAD_DIM), F32),
            pltpu.VMEM((HEADS, 1, HEAD_DIM), F32),
            pltpu.VMEM((HEADS, 1, 1), F32),
        ],
        compiler_params=_cparams(("arbitrary", "arbitrary")),
        name="mlstm",
    )(zm, zm, zm, zm, gates, wc, bc, gn)


def _merge_kernel(ym_ref, a_ref, sgm_ref, x_ref, pm_ref, wo_ref, g2_ref, wrh_ref, wrl_ref, br_ref,
                  x1_ref, h2_ref, lg_ref):
    proj = jnp.dot(ym_ref[...], pm_ref[...], preferred_element_type=F32)
    merged = a_ref[...].astype(F32) + sgm_ref[...].astype(F32) * proj
    x1 = x_ref[...] + jnp.dot(merged.astype(BF16), wo_ref[...], preferred_element_type=F32)
    x1_ref[...] = x1
    h2 = x1 * lax.rsqrt(jnp.mean(x1 * x1, axis=-1, keepdims=True) + EPS) * g2_ref[...]
    h2_ref[...] = h2
    h_hi = h2.astype(BF16)
    h_lo = (h2 - h_hi.astype(F32)).astype(BF16)
    lg_ref[...] = (jnp.dot(h_hi, wrh_ref[...], preferred_element_type=F32)
                   + (jnp.dot(h_lo, wrh_ref[...], preferred_element_type=F32)
                      + jnp.dot(h_hi, wrl_ref[...], preferred_element_type=F32))
                   + br_ref[...])


def _merge(ym, a, zg, x, pm, wo, g2, wr_hi, wr_lo, br):
    t = x.shape[0]
    tm = MERGE_TM
    const2 = lambda i: (0, 0)
    row = lambda i: (i, 0)
    return pl.pallas_call(
        _merge_kernel,
        grid=(t // tm,),
        in_specs=[
            pl.BlockSpec((tm, D_MODEL), row),
            pl.BlockSpec((tm, D_MODEL), row),
            pl.BlockSpec((tm, D_MODEL), lambda i: (i, ZG_M)),
            pl.BlockSpec((tm, D_MODEL), row),
            pl.BlockSpec((D_MODEL, D_MODEL), const2),
            pl.BlockSpec((D_MODEL, D_MODEL), const2),
            pl.BlockSpec((1, D_MODEL), const2),
            pl.BlockSpec((D_MODEL, LANES), const2),
            pl.BlockSpec((D_MODEL, LANES), const2),
            pl.BlockSpec((1, LANES), const2),
        ],
        out_specs=[
            pl.BlockSpec((tm, D_MODEL), row),
            pl.BlockSpec((tm, D_MODEL), row),
            pl.BlockSpec((tm, LANES), row),
        ],
        out_shape=[
            jax.ShapeDtypeStruct((t, D_MODEL), F32),
            jax.ShapeDtypeStruct((t, D_MODEL), F32),
            jax.ShapeDtypeStruct((t, LANES), F32),
        ],
        compiler_params=_cparams(("arbitrary",)),
        name="merge",
    )(ym, a, zg, x, pm, wo, g2, wr_hi, wr_lo, br)


def _route_kernel(lg_ref, info_ref, gate_ref, cnt_ref, cnt_scr):
    tm = lg_ref.shape[0]
    i = pl.program_id(0)

    @pl.when(i == 0)
    def _():
        cnt_scr[...] = jnp.zeros_like(cnt_scr)

    lane = lax.broadcasted_iota(I32, (tm, LANES), 1)
    lane_f = lane.astype(F32)
    vals = jnp.where(lane < N_EXPERTS, lg_ref[...], -jnp.inf)
    tops, ids, hots = [], [], []
    for _ in range(TOP_K):
        m = jnp.max(vals, axis=-1, keepdims=True)
        idx_f = jnp.min(jnp.where(vals == m, lane_f, float(LANES)), axis=-1, keepdims=True)
        idx = idx_f.astype(I32)
        hot = lane == idx
        tops.append(m)
        ids.append(idx)
        hots.append(hot)
        vals = jnp.where(hot, -jnp.inf, vals)

    exps = [jnp.exp(tv - tops[0]) for tv in tops]
    denom = exps[0] + exps[1] + exps[2] + exps[3]

    sel = jnp.zeros((tm, LANES), F32)
    for hot in hots:
        sel = sel + jnp.where(hot, 1.0, 0.0)
    r = lax.broadcasted_iota(I32, (tm, tm), 0)
    c = lax.broadcasted_iota(I32, (tm, tm), 1)
    strict = jnp.where(c < r, 1.0, 0.0).astype(BF16)
    ahead = jnp.dot(strict, sel.astype(BF16), preferred_element_type=F32) + cnt_scr[...]

    info = jnp.zeros((tm, LANES), I32)
    gate = jnp.zeros((tm, LANES), F32)
    for k in range(TOP_K):
        pos = jnp.sum(jnp.where(hots[k], ahead, 0.0), axis=-1, keepdims=True).astype(I32)
        info = jnp.where(lane == k, ids[k], info)
        info = jnp.where(lane == TOP_K + k, pos, info)
        gate = jnp.where(lane == k, exps[k] / denom, gate)
    info_ref[...] = info
    gate_ref[...] = gate
    cnt_scr[...] = cnt_scr[...] + jnp.sum(sel, axis=0, keepdims=True)
    cnt_ref[...] = jnp.broadcast_to(cnt_scr[...], cnt_ref.shape)


def _route(logits):
    t = logits.shape[0]
    tm = ROUTE_TM
    row = lambda i: (i, 0)
    return pl.pallas_call(
        _route_kernel,
        grid=(t // tm,),
        in_specs=[pl.BlockSpec((tm, LANES), row)],
        out_specs=[
            pl.BlockSpec((tm, LANES), row),
            pl.BlockSpec((tm, LANES), row),
            pl.BlockSpec((SUBLANES, LANES), lambda i: (0, 0)),
        ],
        out_shape=[
            jax.ShapeDtypeStruct((t, LANES), I32),
            jax.ShapeDtypeStruct((t, LANES), F32),
            jax.ShapeDtypeStruct((SUBLANES, LANES), F32),
        ],
        scratch_shapes=[pltpu.VMEM((1, LANES), F32)],
        compiler_params=_cparams(("arbitrary",)),
        name="route",
    )(logits)


def _dispatch_kernel(dest_ref, zero_row_ref, h2_ref, xs_hbm, zbuf, sem, zsem, *, n_zero):
    tm = h2_ref.shape[0]
    i = pl.program_id(0)

    def zero_copy(j):
        row0 = pl.multiple_of(zero_row_ref[j], MOE_SUB)
        return pltpu.make_async_copy(zbuf, xs_hbm.at[pl.ds(row0, MOE_SUB)], zsem)

    @pl.when(i == 0)
    def _():
        zbuf[...] = jnp.zeros_like(zbuf)

        def start(j, carry):
            @pl.when(zero_row_ref[j] >= 0)
            def _():
                zero_copy(j).start()
            return carry

        def wait(j, carry):
            @pl.when(zero_row_ref[j] >= 0)
            def _():
                zero_copy(j).wait()
            return carry

        lax.fori_loop(0, n_zero, start, 0)
        lax.fori_loop(0, n_zero, wait, 0)

    def row_copy(r, k):
        d = dest_ref[(i * tm + r) * TOP_K + k]
        return pltpu.make_async_copy(h2_ref.at[pl.ds(r, 1)], xs_hbm.at[pl.ds(d, 1)], sem)

    def start_row(r, carry):
        for k in range(TOP_K):
            row_copy(r, k).start()
        return carry

    def wait_row(r, carry):
        for k in range(TOP_K):
            row_copy(r, k).wait()
        return carry

    lax.fori_loop(0, tm, start_row, 0)
    lax.fori_loop(0, tm, wait_row, 0)


def _dispatch(dest, zero_rows, h2, n_rows):
    t = h2.shape[0]
    tm = DISPATCH_TM
    grid_spec = pltpu.PrefetchScalarGridSpec(
        num_scalar_prefetch=2,
        grid=(t // tm,),
        in_specs=[pl.BlockSpec((tm, D_MODEL), lambda i, d, z: (i, 0))],
        out_specs=pl.BlockSpec(memory_space=pl.ANY),
        scratch_shapes=[
            pltpu.VMEM((MOE_SUB, D_MODEL), F32),
            pltpu.SemaphoreType.DMA,
            pltpu.SemaphoreType.DMA,
        ],
    )
    return pl.pallas_call(
        functools.partial(_dispatch_kernel, n_zero=zero_rows.shape[0]),
        grid_spec=grid_spec,
        out_shape=jax.ShapeDtypeStruct((n_rows, D_MODEL), F32),
        compiler_params=_cparams(("arbitrary",)),
        name="dispatch",
    )(dest, zero_rows, h2)


def _experts_kernel(tile_e_ref, tile_occ_ref, tile_row_ref,
                    xs_hbm, wg_ref, wu_ref, bg_ref, bu_ref, wd_ref, bd_ref, out_ref,
                    xbuf, xb_scr, act_scr, sem, *, n_tiles):
    del tile_e_ref
    i = pl.program_id(0)
    j = pl.program_id(1)
    occ = tile_occ_ref[i]

    def sub_rows(sub):
        return pl.ds(pl.multiple_of(sub * MOE_SUB, MOE_SUB), MOE_SUB)

    def sub_copy(tile, sub):
        row0 = pl.multiple_of(tile_row_ref[tile] + sub * MOE_SUB, MOE_SUB)
        return pltpu.make_async_copy(xs_hbm.at[pl.ds(row0, MOE_SUB)], xbuf.at[sub_rows(sub)], sem)

    def start_load(tile):
        def body(sub, carry):
            sub_copy(tile, sub).start()
            return carry
        lax.fori_loop(0, tile_occ_ref[tile], body, 0)

    @pl.when(j == 0)
    def _():
        @pl.when(i == 0)
        def _():
            start_load(0)

        def wait_sub(sub, carry):
            sub_copy(i, sub).wait()
            return carry

        def round_sub(sub, carry):
            xb_scr[sub_rows(sub), :] = xbuf[sub_rows(sub), :].astype(BF16)
            return carry

        lax.fori_loop(0, occ, wait_sub, 0)
        lax.fori_loop(0, occ, round_sub, 0)

        @pl.when(i + 1 < n_tiles)
        def _():
            start_load(i + 1)

    @pl.when((j < MOE_NF) & (occ > 0))
    def _():
        wg = wg_ref[...].astype(BF16)
        wu = wu_ref[...].astype(BF16)

        def per_sub(sub, carry):
            rows = sub_rows(sub)
            xb = xb_scr[rows, :]
            gate = jnp.dot(xb, wg, preferred_element_type=F32) + bg_ref[...]
            up = jnp.dot(xb, wu, preferred_element_type=F32) + bu_ref[...]
            gate = jnp.minimum(gate, SWIGLU_LIMIT)
            up = jnp.clip(up, -SWIGLU_LIMIT, SWIGLU_LIMIT)
            act = (up + 1.0) * (gate * _sigmoid(SWIGLU_ALPHA * gate))
            act_scr[j, rows, :] = act.astype(BF16)
            return carry

        lax.fori_loop(0, occ, per_sub, 0)

    @pl.when((j >= MOE_NF) & (occ > 0))
    def _():
        wd = wd_ref[...].astype(BF16)

        def per_sub(sub, carry):
            rows = sub_rows(sub)
            y = bd_ref[...] + jnp.dot(act_scr[0, rows, :], wd[0:MOE_TF, :],
                                      preferred_element_type=F32)
            for c in range(1, MOE_NF):
                y = y + jnp.dot(act_scr[c, rows, :], wd[c * MOE_TF:(c + 1) * MOE_TF, :],
                                preferred_element_type=F32)
            out_ref[rows, :] = y
            return carry

        lax.fori_loop(0, occ, per_sub, 0)

    @pl.when(j >= MOE_NF)
    def _():
        def zero_sub(sub, carry):
            out_ref[sub_rows(sub), :] = jnp.zeros((MOE_SUB, MOE_TN), F32)
            return carry

        lax.fori_loop(occ, MOE_SUBS, zero_sub, 0)


def _experts(tile_e, tile_occ, tile_row, xs, wgu, bgu, wd, bd, n_tiles):
    up_off = D_FF // MOE_TF

    def fa(i, j, to):
        return jnp.where(to[i] > 0, jnp.minimum(j, MOE_NF - 1), MOE_NF - 1)

    def fb(i, j, to):
        return jnp.where(to[i] > 0, jnp.maximum(j - MOE_NF, 0), MOE_NN - 1)

    def gate_map(i, j, te, to, tr):
        return (te[i], 0, fa(i, j, to))

    def up_map(i, j, te, to, tr):
        return (te[i], 0, up_off + fa(i, j, to))

    def down_map(i, j, te, to, tr):
        return (te[i], 0, fb(i, j, to))

    def out_map(i, j, te, to, tr):
        return (i, jnp.maximum(j - MOE_NF, 0))

    grid_spec = pltpu.PrefetchScalarGridSpec(
        num_scalar_prefetch=3,
        grid=(n_tiles, MOE_NF + MOE_NN),
        in_specs=[
            pl.BlockSpec(memory_space=pl.ANY),
            pl.BlockSpec((None, D_MODEL, MOE_TF), gate_map),
            pl.BlockSpec((None, D_MODEL, MOE_TF), up_map),
            pl.BlockSpec((None, 1, MOE_TF), gate_map),
            pl.BlockSpec((None, 1, MOE_TF), up_map),
            pl.BlockSpec((None, D_FF, MOE_TN), down_map),
            pl.BlockSpec((None, 1, MOE_TN), down_map),
        ],
        out_specs=pl.BlockSpec((MOE_CAP, MOE_TN), out_map),
        scratch_shapes=[
            pltpu.VMEM((MOE_CAP, D_MODEL), F32),
            pltpu.VMEM((MOE_CAP, D_MODEL), BF16),
            pltpu.VMEM((MOE_NF, MOE_CAP, MOE_TF), BF16),
            pltpu.SemaphoreType.DMA,
        ],
    )
    return pl.pallas_call(
        functools.partial(_experts_kernel, n_tiles=n_tiles),
        grid_spec=grid_spec,
        out_shape=jax.ShapeDtypeStruct((n_tiles * MOE_CAP, D_MODEL), F32),
        compiler_params=_cparams(("arbitrary", "arbitrary")),
        name="experts",
    )(tile_e, tile_occ, tile_row, xs, wgu, wgu, bgu, bgu, wd, bd)


def _combine_kernel(dest_ref, ys_hbm, x1_ref, gate_ref, gf_ref, out_ref, buf, sem, *, n_tiles):
    tm = x1_ref.shape[0]
    i = pl.program_id(0)
    slot = i % 2

    def row_copy(tile, s, r, k):
        d = dest_ref[(tile * tm + r) * TOP_K + k]
        return pltpu.make_async_copy(ys_hbm.at[pl.ds(d, 1)], buf.at[s, k, pl.ds(r, 1)], sem.at[s])

    def start_gather(tile, s):
        def per_row(r, carry):
            for k in range(TOP_K):
                row_copy(tile, s, r, k).start()
            return carry
        lax.fori_loop(0, tm, per_row, 0)

    def wait_gather(tile, s):
        def per_row(r, carry):
            for k in range(TOP_K):
                row_copy(tile, s, r, k).wait()
            return carry
        lax.fori_loop(0, tm, per_row, 0)

    @pl.when(i == 0)
    def _():
        start_gather(0, 0)

    wait_gather(i, slot)

    @pl.when(i + 1 < n_tiles)
    def _():
        start_gather(i + 1, 1 - slot)

    x2 = x1_ref[...]
    for k in range(TOP_K):
        x2 = x2 + gate_ref[:, k:k + 1] * buf[slot, k]
    out_ref[...] = x2 * lax.rsqrt(jnp.mean(x2 * x2, axis=-1, keepdims=True) + EPS) * gf_ref[...]


def _combine(dest, ys, x1, gate, gf):
    t = x1.shape[0]
    tm = COMBINE_TM
    n_tiles = t // tm
    grid_spec = pltpu.PrefetchScalarGridSpec(
        num_scalar_prefetch=1,
        grid=(n_tiles,),
        in_specs=[
            pl.BlockSpec(memory_space=pl.ANY),
            pl.BlockSpec((tm, D_MODEL), lambda i, d: (i, 0)),
            pl.BlockSpec((tm, LANES), lambda i, d: (i, 0)),
            pl.BlockSpec((1, D_MODEL), lambda i, d: (0, 0)),
        ],
        out_specs=pl.BlockSpec((tm, D_MODEL), lambda i, d: (i, 0)),
        scratch_shapes=[
            pltpu.VMEM((2, TOP_K, tm, D_MODEL), F32),
            pltpu.SemaphoreType.DMA((2,)),
        ],
    )
    return pl.pallas_call(
        functools.partial(_combine_kernel, n_tiles=n_tiles),
        grid_spec=grid_spec,
        out_shape=jax.ShapeDtypeStruct((t, D_MODEL), F32),
        compiler_params=_cparams(("arbitrary",)),
        name="combine",
    )(dest, ys, x1, gate, gf)


def _routing_tables(counts, n_tokens):
    n_tiles = N_EXPERTS + (n_tokens * TOP_K) // MOE_CAP
    n_subs = N_EXPERTS + (n_tokens * TOP_K) // MOE_SUB
    subs_per = (counts + MOE_SUB - 1) // MOE_SUB
    sub_end = jnp.cumsum(subs_per)
    x_start = (sub_end - subs_per) * MOE_SUB
    tiles_per = (counts + MOE_CAP - 1) // MOE_CAP
    tile_end = jnp.cumsum(tiles_per)
    tile_start = tile_end - tiles_per
    y_start = tile_start * MOE_CAP

    tile_ids = jnp.arange(n_tiles, dtype=I32)
    n_active = tile_end[-1]
    last = jnp.maximum(n_active - 1, 0)
    active = tile_ids < n_active
    tile_eff = jnp.minimum(tile_ids, last)
    tile_e = jnp.sum((tile_eff[:, None] >= tile_end[None, :]).astype(I32), axis=1)
    tile_e = jnp.minimum(tile_e, N_EXPERTS - 1)
    in_expert = tile_eff - tile_start[tile_e]
    rows_in = counts[tile_e] - in_expert * MOE_CAP
    occ = jnp.clip((rows_in + MOE_SUB - 1) // MOE_SUB, 0, MOE_SUBS)
    tile_occ = jnp.where(active, occ, 0).astype(I32)
    tile_row = (x_start[tile_e] + in_expert * MOE_CAP).astype(I32)

    last_sub = jnp.where(counts > 0, x_start + (subs_per - 1) * MOE_SUB, -1)
    tail_ids = sub_end[-1] + jnp.arange(N_EXPERTS, dtype=I32)
    tail = jnp.where(tail_ids < n_subs, tail_ids * MOE_SUB, -1)
    zero_rows = jnp.concatenate([last_sub, tail]).astype(I32)
    return (n_tiles, n_subs * MOE_SUB, tile_e.astype(I32), tile_occ, tile_row,
            x_start, y_start, zero_rows)


def _layer(x2d, bsz, seq, g_norm_mix, w_in, b_in, w_conv, b_conv, w_sgu, b_sgu, g_sgu_ln, b_sgu_ln,
           g_mlstm_norm, w_proj_sgu, w_proj_mlstm, w_out, g_norm_moe, w_router, b_router,
           w_gate_up, b_gate_up, w_down, b_down, g_out):
    t = x2d.shape[0]

    pad = LANES - 2 * HEADS
    w_if = jnp.pad(w_in[:, N_MAIN:GATE_OFF], ((0, 0), (0, pad))).astype(BF16)
    b_if = jnp.pad(b_in[N_MAIN:GATE_OFF], (0, pad))[None, :]
    w_mg = w_in[:, GATE_OFF:]
    bsb = jnp.broadcast_to(b_sgu.T[:, :, None], (SGU_BLOCK, SGU_GROUPS, SGU_GROUP_DIM))
    bsb = bsb.reshape(SGU_BLOCK, D_MODEL)
    w_r = jnp.pad(w_router, ((0, 0), (0, LANES - N_EXPERTS)))
    w_r_hi = w_r.astype(BF16)
    w_r_lo = (w_r - w_r_hi.astype(F32)).astype(BF16)
    b_r = jnp.pad(b_router, (0, LANES - N_EXPERTS))[None, :]

    h, gates = _norm_gates(x2d, g_norm_mix[None, :], w_if, b_if)
    zm = _in_proj(h, w_in, b_in[None, :], N_MAIN, 2 * D_MODEL, 3 * D_MODEL, "in_proj_main")
    zg = _in_proj(h, w_mg, b_in[None, GATE_OFF:], 2 * D_MODEL, 0, 0, "in_proj_gate")
    a = _sgu(zm, zg, w_sgu, bsb, g_sgu_ln[None, :], b_sgu_ln[None, :], w_proj_sgu.astype(BF16))
    ym = _mlstm(zm, gates, w_conv, b_conv[None, :], g_mlstm_norm[None, :], bsz, seq)
    x1, h2, logits = _merge(ym, a, zg, x2d, w_proj_mlstm.astype(BF16), w_out.astype(BF16),
                            g_norm_moe[None, :], w_r_hi, w_r_lo, b_r)
    info, gate, cnt = _route(logits)

    counts = cnt[0, :N_EXPERTS].astype(I32)
    (n_tiles, n_xrows, tile_e, tile_occ, tile_row, x_start, y_start,
     zero_rows) = _routing_tables(counts, t)
    expert_ids, pos = info[:, :TOP_K], info[:, TOP_K:2 * TOP_K]
    dest_x = (x_start[expert_ids] + pos).reshape(-1)
    dest_y = (y_start[expert_ids] + pos).reshape(-1)

    xs = _dispatch(dest_x, zero_rows, h2, n_xrows)
    ys = _experts(tile_e, tile_occ, tile_row, xs, w_gate_up, b_gate_up[:, None, :],
                  w_down, b_down[:, None, :], n_tiles)
    return _combine(dest_y, ys, x1, gate, g_out[None, :])


def kernel(x, g_norm_mix, w_in, b_in, w_conv, b_conv, w_sgu, b_sgu, g_sgu_ln, b_sgu_ln, g_mlstm_norm,
           w_proj_sgu, w_proj_mlstm, w_out, g_norm_moe, w_router, b_router, w_gate_up, b_gate_up,
           w_down, b_down, g_final):
    bsz, seq, d = x.shape
    assert d == D_MODEL and w_in.shape[0] == 1, "single-layer block with d_model 2048"
    out = _layer(x.reshape(bsz * seq, d), bsz, seq, g_norm_mix[0], w_in[0], b_in[0], w_conv[0],
                 b_conv[0], w_sgu[0], b_sgu[0], g_sgu_ln[0], b_sgu_ln[0], g_mlstm_norm[0],
                 w_proj_sgu[0], w_proj_mlstm[0], w_out[0], g_norm_moe[0], w_router[0], b_router[0],
                 w_gate_up[0], b_gate_up[0], w_down[0], b_down[0], g_final)
    return out.reshape(bsz, seq, d)
```

```python
import functools

import jax
import jax.numpy as jnp
from jax import lax
from jax.experimental import pallas as pl
from jax.experimental.pallas import tpu as pltpu

F32 = jnp.float32
BF16 = jnp.bfloat16
I32 = jnp.int32

D_MODEL = 2048
CHUNK = 64
SGU_BLOCK = 128
SGU_GROUPS = 8
SGU_GROUP_DIM = 256
HEADS = 4
HEAD_DIM = 512
CONV_WIDTH = 4
N_EXPERTS = 32
TOP_K = 4
D_FF = 2048
SWIGLU_LIMIT = 7.0
SWIGLU_ALPHA = 1.702
EPS = 1e-6

LANES = 128
SUBLANES = 8
VMEM_LIMIT = 56 * 1024 * 1024

ZB_U, ZB_V, ZB_Q, ZB_K, ZB_VM, ZB_O = range(6)
ZG_A, ZG_M = range(2)
N_MAIN = 6 * D_MODEL
GATE_OFF = N_MAIN + 2 * HEADS

NORM_TM = 512
INPROJ_TM = 1024
INPROJ_TN = 1024
SGU_TM = 256
MLSTM_L = 256
MERGE_TM = 256
ROUTE_TM = 256
DISPATCH_TM = 256
MOE_SUB = 256
MOE_CAP = 1280
MOE_SUBS = MOE_CAP // MOE_SUB
MOE_TF = 512
MOE_NF = D_FF // MOE_TF
MOE_TN = 512
MOE_NN = D_MODEL // MOE_TN
COMBINE_TM = 128


def _cparams(sem, vmem_limit=VMEM_LIMIT):
    return pltpu.CompilerParams(dimension_semantics=sem, vmem_limit_bytes=vmem_limit)


def _gelu(x):
    return 0.5 * x * (1.0 + lax.erf(x * (2.0 ** -0.5)))


def _sigmoid(x):
    return 1.0 / (1.0 + jnp.exp(-x))


def _norm_gates_kernel(x_ref, g_ref, wif_ref, bif_ref, h_ref, gates_ref):
    x = x_ref[...]
    ms = jnp.mean(x * x, axis=-1, keepdims=True)
    h = (x * lax.rsqrt(ms + EPS) * g_ref[...]).astype(BF16)
    h_ref[...] = h
    gates_ref[...] = jnp.dot(h, wif_ref[...], preferred_element_type=F32) + bif_ref[...]


def _norm_gates(x, g, wif, bif):
    t = x.shape[0]
    tm = min(NORM_TM, t)
    row = lambda i: (i, 0)
    const = lambda i: (0, 0)
    return pl.pallas_call(
        _norm_gates_kernel,
        grid=(t // tm,),
        in_specs=[
            pl.BlockSpec((tm, D_MODEL), row),
            pl.BlockSpec((1, D_MODEL), const),
            pl.BlockSpec((D_MODEL, LANES), const),
            pl.BlockSpec((1, LANES), const),
        ],
        out_specs=[pl.BlockSpec((tm, D_MODEL), row), pl.BlockSpec((tm, LANES), row)],
        out_shape=[jax.ShapeDtypeStruct((t, D_MODEL), BF16), jax.ShapeDtypeStruct((t, LANES), F32)],
        compiler_params=_cparams(("arbitrary",)),
        name="norm_gates",
    )(x, g, wif, bif)


def _inproj_kernel(h_ref, w_ref, b_ref, z_ref, w_scr, *, n_gelu, n_plain):
    j = pl.program_id(0)

    @pl.when(pl.program_id(1) == 0)
    def _():
        w_scr[...] = w_ref[...].astype(BF16)

    acc = lax.dot_general(h_ref[...], w_scr[...], (((1,), (1,)), ((), ())),
                          preferred_element_type=F32) + b_ref[...]

    @pl.when(j < n_gelu)
    def _():
        z_ref[...] = _gelu(acc).astype(z_ref.dtype)

    @pl.when((j >= n_gelu) & (j < n_gelu + n_plain))
    def _():
        z_ref[...] = acc.astype(z_ref.dtype)

    @pl.when(j >= n_gelu + n_plain)
    def _():
        z_ref[...] = _sigmoid(acc).astype(z_ref.dtype)


def _in_proj(h, w_t, b, n_cols, n_gelu_cols, n_plain_cols, name):
    t = h.shape[0]
    tm, tn = min(INPROJ_TM, t), INPROJ_TN
    return pl.pallas_call(
        functools.partial(_inproj_kernel, n_gelu=n_gelu_cols // tn, n_plain=n_plain_cols // tn),
        grid=(n_cols // tn, t // tm),
        in_specs=[
            pl.BlockSpec((tm, D_MODEL), lambda j, i: (i, 0)),
            pl.BlockSpec((tn, D_MODEL), lambda j, i: (j, 0)),
            pl.BlockSpec((1, tn), lambda j, i: (0, j)),
        ],
        out_specs=pl.BlockSpec((tm, tn), lambda j, i: (i, j)),
        out_shape=jax.ShapeDtypeStruct((t, n_cols), BF16),
        scratch_shapes=[pltpu.VMEM((tn, D_MODEL), BF16)],
        compiler_params=_cparams(("arbitrary", "arbitrary")),
        name=name,
    )(h, w_t, b)


def _sgu_kernel(gu_ref, gv_ref, sga_ref, ws_ref, bsb_ref, lng_ref, lnb_ref, pa_ref, out_ref,
                ya_scr):
    tm = gu_ref.shape[0]
    v = gv_ref[...].astype(F32)
    mu = jnp.mean(v, axis=-1, keepdims=True)
    vc = v - mu
    var = jnp.mean(vc * vc, axis=-1, keepdims=True)
    vln = (vc * lax.rsqrt(var + EPS) * lng_ref[...] + lnb_ref[...]).astype(BF16)

    t_id = lax.broadcasted_iota(I32, (SGU_BLOCK, SGU_BLOCK), 0) // CHUNK
    s_id = lax.broadcasted_iota(I32, (SGU_BLOCK, SGU_BLOCK), 1) // CHUNK
    causal = s_id <= t_id
    for g in range(SGU_GROUPS):
        w = jnp.where(causal, ws_ref[g], 0.0).astype(BF16)
        cols = slice(g * SGU_GROUP_DIM, (g + 1) * SGU_GROUP_DIM)
        bias = bsb_ref[:, cols]
        for blk in range(tm // SGU_BLOCK):
            rows = slice(blk * SGU_BLOCK, (blk + 1) * SGU_BLOCK)
            mixed = jnp.dot(w, vln[rows, cols], preferred_element_type=F32) + bias
            ya_scr[rows, cols] = (gu_ref[rows, cols].astype(F32) * mixed).astype(BF16)

    proj = jnp.dot(ya_scr[...], pa_ref[...], preferred_element_type=F32)
    out_ref[...] = (sga_ref[...].astype(F32) * proj).astype(out_ref.dtype)


def _sgu(zm, zg, ws, bsb, lng, lnb, pa):
    t = zm.shape[0]
    tm = SGU_TM
    const2 = lambda i: (0, 0)
    return pl.pallas_call(
        _sgu_kernel,
        grid=(t // tm,),
        in_specs=[
            pl.BlockSpec((tm, D_MODEL), lambda i: (i, ZB_U)),
            pl.BlockSpec((tm, D_MODEL), lambda i: (i, ZB_V)),
            pl.BlockSpec((tm, D_MODEL), lambda i: (i, ZG_A)),
            pl.BlockSpec((SGU_GROUPS, SGU_BLOCK, SGU_BLOCK), lambda i: (0, 0, 0)),
            pl.BlockSpec((SGU_BLOCK, D_MODEL), const2),
            pl.BlockSpec((1, D_MODEL), const2),
            pl.BlockSpec((1, D_MODEL), const2),
            pl.BlockSpec((D_MODEL, D_MODEL), const2),
        ],
        out_specs=pl.BlockSpec((tm, D_MODEL), lambda i: (i, 0)),
        out_shape=jax.ShapeDtypeStruct((t, D_MODEL), BF16),
        scratch_shapes=[pltpu.VMEM((tm, D_MODEL), BF16)],
        compiler_params=_cparams(("arbitrary",)),
        name="sgu",
    )(zm, zm, zg, ws, bsb, lng, lnb, pa)


def _mlstm_kernel(q_ref, k_ref, v_ref, so_ref, gates_ref, wc_ref, bc_ref, gn_ref, out_ref,
                  hist, ct_scr, n_scr, m_scr):
    L = q_ref.shape[0]
    c = pl.program_id(1)

    @pl.when(c == 0)
    def _():
        hist[0:SUBLANES, :] = jnp.zeros((SUBLANES, 2 * D_MODEL), F32)
        ct_scr[...] = jnp.zeros_like(ct_scr)
        n_scr[...] = jnp.zeros_like(n_scr)
        m_scr[...] = jnp.zeros_like(m_scr)

    hist[SUBLANES:SUBLANES + L, 0:D_MODEL] = q_ref[...].astype(F32)
    hist[SUBLANES:SUBLANES + L, D_MODEL:2 * D_MODEL] = k_ref[...].astype(F32)
    qk = bc_ref[...] + wc_ref[CONV_WIDTH - 1:CONV_WIDTH, :] * hist[SUBLANES:SUBLANES + L, :]
    for j in range(CONV_WIDTH - 1):
        off = SUBLANES - (CONV_WIDTH - 1) + j
        qk = qk + wc_ref[j:j + 1, :] * hist[off:off + L, :]
    qk = qk * _sigmoid(qk)
    hist[0:SUBLANES, :] = hist[L:L + SUBLANES, :]

    gts = gates_ref[...]
    lf = jnp.minimum(gts, 0.0) - jnp.log(1.0 + jnp.exp(-jnp.abs(gts)))
    row = lax.broadcasted_iota(I32, (L, L), 0)
    col = lax.broadcasted_iota(I32, (L, L), 1)
    tril = col <= row
    tri = jnp.where(tril, 1.0, 0.0).astype(BF16)
    lf_hi = lf.astype(BF16)
    lf_lo = (lf - lf_hi.astype(F32)).astype(BF16)
    bcum = (jnp.dot(tri, lf_hi, preferred_element_type=F32)
            + jnp.dot(tri, lf_lo, preferred_element_type=F32))
    bcum_t = bcum.T
    gts_t = gts.T

    scale = HEAD_DIM ** -0.5
    for h in range(HEADS):
        cols = slice(h * HEAD_DIM, (h + 1) * HEAD_DIM)
        q = qk[:, h * HEAD_DIM:(h + 1) * HEAD_DIM] * scale
        k = qk[:, D_MODEL + h * HEAD_DIM:D_MODEL + (h + 1) * HEAD_DIM]
        qb = q.astype(BF16)
        kb = k.astype(BF16)
        vb = v_ref[:, cols]
        bc = bcum[:, HEADS + h:HEADS + h + 1]
        bc_r = bcum_t[HEADS + h:HEADS + h + 1, :]
        ig_c = gts[:, h:h + 1]
        ig_r = gts_t[h:h + 1, :]
        m_st = m_scr[h]
        n_st = n_scr[h]

        dlog = jnp.where(tril, bc - bc_r + ig_r, -jnp.inf)
        m_inter = bc + m_st
        m_t = jnp.maximum(m_inter, jnp.max(dlog, axis=-1, keepdims=True))
        p = jnp.exp(dlog - m_t)
        s = lax.dot_general(qb, kb, (((1,), (1,)), ((), ())), preferred_element_type=F32)
        sc = s * p
        inter = jnp.exp(m_inter - m_t)
        ctb = ct_scr[h].astype(BF16)
        num = (jnp.dot(sc.astype(BF16), vb, preferred_element_type=F32)
               + inter * jnp.dot(qb, ctb, preferred_element_type=F32))
        den = (jnp.sum(sc, axis=-1, keepdims=True)
               + inter * jnp.sum(q * n_st, axis=-1, keepdims=True))
        hh = num / jnp.maximum(jnp.abs(den), jnp.exp(-m_t))
        hn = hh * lax.rsqrt(jnp.mean(hh * hh, axis=-1, keepdims=True) + EPS) * gn_ref[:, cols]
        out_ref[:, cols] = (hn * so_ref[:, cols].astype(F32)).astype(out_ref.dtype)

        g_tot = bc[L - 1:L, :]
        a = g_tot - bc + ig_c
        m_new = jnp.maximum(g_tot + m_st, jnp.max(a, axis=0, keepdims=True))
        wa = jnp.exp(a - m_new)
        decay = jnp.exp(g_tot + m_st - m_new)
        wv = (wa * vb.astype(F32)).astype(BF16)
        upd = jnp.dot(k.T.astype(BF16), wv, preferred_element_type=F32)
        ct_scr[h] = decay * ct_scr[h] + upd
        n_scr[h] = decay * n_st + jnp.sum(wa * k, axis=0, keepdims=True)
        m_scr[h] = m_new


def _mlstm(zm, gates, wc, bc, gn, bsz, seq):
    t = zm.shape[0]
    L = MLSTM_L
    nc = seq // L
    zspec = lambda blk: pl.BlockSpec((L, D_MODEL), lambda b, c: (b * nc + c, blk))
    const2 = lambda b, c: (0, 0)
    return pl.pallas_call(
        _mlstm_kernel,
        grid=(bsz, nc),
        in_specs=[
            zspec(ZB_Q), zspec(ZB_K), zspec(ZB_VM), zspec(ZB_O),
            pl.BlockSpec((L, LANES), lambda b, c: (b * nc + c, 0)),
            pl.BlockSpec((CONV_WIDTH, 2 * D_MODEL), const2),
            pl.BlockSpec((1, 2 * D_MODEL), const2),
            pl.BlockSpec((1, D_MODEL), const2),
        ],
        out_specs=pl.BlockSpec((L, D_MODEL), lambda b, c: (b * nc + c, 0)),
        out_shape=jax.ShapeDtypeStruct((t, D_MODEL), BF16),
        scratch_shapes=[
            pltpu.VMEM((L + 2 * SUBLANES, 2 * D_MODEL), F32),
            pltpu.VMEM((HEADS, HEAD_DIM, HEAD_DIM), F32),
            pltpu.VMEM((HEADS, 1, HEAD_DIM), F32),
            pltpu.VMEM((HEADS, 1, 1), F32),
        ],
        compiler_params=_cparams(("arbitrary", "arbitrary")),
        name="mlstm",
    )(zm, zm, zm, zm, gates, wc, bc, gn)


def _merge_kernel(ym_ref, a_ref, sgm_ref, x_ref, pm_ref, wo_ref, g2_ref, wrh_ref, wrl_ref, br_ref,
                  x1_ref, h2_ref, lg_ref):
    proj = jnp.dot(ym_ref[...], pm_ref[...], preferred_element_type=F32)
    merged = a_ref[...].astype(F32) + sgm_ref[...].astype(F32) * proj
    x1 = x_ref[...] + jnp.dot(merged.astype(BF16), wo_ref[...], preferred_element_type=F32)
    x1_ref[...] = x1
    h2 = x1 * lax.rsqrt(jnp.mean(x1 * x1, axis=-1, keepdims=True) + EPS) * g2_ref[...]
    h2_ref[...] = h2
    h_hi = h2.astype(BF16)
    h_lo = (h2 - h_hi.astype(F32)).astype(BF16)
    lg_ref[...] = (jnp.dot(h_hi, wrh_ref[...], preferred_element_type=F32)
                   + (jnp.dot(h_lo, wrh_ref[...], preferred_element_type=F32)
                      + jnp.dot(h_hi, wrl_ref[...], preferred_element_type=F32))
                   + br_ref[...])


def _merge(ym, a, zg, x, pm, wo, g2, wr_hi, wr_lo, br):
    t = x.shape[0]
    tm = MERGE_TM
    const2 = lambda i: (0, 0)
    row = lambda i: (i, 0)
    return pl.pallas_call(
        _merge_kernel,
        grid=(t // tm,),
        in_specs=[
            pl.BlockSpec((tm, D_MODEL), row),
            pl.BlockSpec((tm, D_MODEL), row),
            pl.BlockSpec((tm, D_MODEL), lambda i: (i, ZG_M)),
            pl.BlockSpec((tm, D_MODEL), row),
            pl.BlockSpec((D_MODEL, D_MODEL), const2),
            pl.BlockSpec((D_MODEL, D_MODEL), const2),
            pl.BlockSpec((1, D_MODEL), const2),
            pl.BlockSpec((D_MODEL, LANES), const2),
            pl.BlockSpec((D_MODEL, LANES), const2),
            pl.BlockSpec((1, LANES), const2),
        ],
        out_specs=[
            pl.BlockSpec((tm, D_MODEL), row),
            pl.BlockSpec((tm, D_MODEL), row),
            pl.BlockSpec((tm, LANES), row),
        ],
        out_shape=[
            jax.ShapeDtypeStruct((t, D_MODEL), F32),
            jax.ShapeDtypeStruct((t, D_MODEL), F32),
            jax.ShapeDtypeStruct((t, LANES), F32),
        ],
        compiler_params=_cparams(("arbitrary",)),
        name="merge",
    )(ym, a, zg, x, pm, wo, g2, wr_hi, wr_lo, br)


def _route_kernel(lg_ref, info_ref, gate_ref, cnt_ref, cnt_scr):
    tm = lg_ref.shape[0]
    i = pl.program_id(0)

    @pl.when(i == 0)
    def _():
        cnt_scr[...] = jnp.zeros_like(cnt_scr)

    lane = lax.broadcasted_iota(I32, (tm, LANES), 1)
    lane_f = lane.astype(F32)
    vals = jnp.where(lane < N_EXPERTS, lg_ref[...], -jnp.inf)
    tops, ids, hots = [], [], []
    for _ in range(TOP_K):
        m = jnp.max(vals, axis=-1, keepdims=True)
        idx_f = jnp.min(jnp.where(vals == m, lane_f, float(LANES)), axis=-1, keepdims=True)
        idx = idx_f.astype(I32)
        hot = lane == idx
        tops.append(m)
        ids.append(idx)
        hots.append(hot)
        vals = jnp.where(hot, -jnp.inf, vals)

    exps = [jnp.exp(tv - tops[0]) for tv in tops]
    denom = exps[0] + exps[1] + exps[2] + exps[3]

    sel = jnp.zeros((tm, LANES), F32)
    for hot in hots:
        sel = sel + jnp.where(hot, 1.0, 0.0)
    r = lax.broadcasted_iota(I32, (tm, tm), 0)
    c = lax.broadcasted_iota(I32, (tm, tm), 1)
    strict = jnp.where(c < r, 1.0, 0.0).astype(BF16)
    ahead = jnp.dot(strict, sel.astype(BF16), preferred_element_type=F32) + cnt_scr[...]

    info = jnp.zeros((tm, LANES), I32)
    gate = jnp.zeros((tm, LANES), F32)
    for k in range(TOP_K):
        pos = jnp.sum(jnp.where(hots[k], ahead, 0.0), axis=-1, keepdims=True).astype(I32)
        info = jnp.where(lane == k, ids[k], info)
        info = jnp.where(lane == TOP_K + k, pos, info)
        gate = jnp.where(lane == k, exps[k] / denom, gate)
    info_ref[...] = info
    gate_ref[...] = gate
    cnt_scr[...] = cnt_scr[...] + jnp.sum(sel, axis=0, keepdims=True)
    cnt_ref[...] = jnp.broadcast_to(cnt_scr[...], cnt_ref.shape)


def _route(logits):
    t = logits.shape[0]
    tm = ROUTE_TM
    row = lambda i: (i, 0)
    return pl.pallas_call(
        _route_kernel,
        grid=(t // tm,),
        in_specs=[pl.BlockSpec((tm, LANES), row)],
        out_specs=[
            pl.BlockSpec((tm, LANES), row),
            pl.BlockSpec((tm, LANES), row),
            pl.BlockSpec((SUBLANES, LANES), lambda i: (0, 0)),
        ],
        out_shape=[
            jax.ShapeDtypeStruct((t, LANES), I32),
            jax.ShapeDtypeStruct((t, LANES), F32),
            jax.ShapeDtypeStruct((SUBLANES, LANES), F32),
        ],
        scratch_shapes=[pltpu.VMEM((1, LANES), F32)],
        compiler_params=_cparams(("arbitrary",)),
        name="route",
    )(logits)


def _grouped_row(route_ref, start_ref, token, k):
    base = token * (2 * TOP_K)
    return start_ref[route_ref[base + k]] + route_ref[base + TOP_K + k]


def _dispatch_kernel(route_ref, start_ref, zero_row_ref, h2_ref, xs_hbm, zbuf, sem, zsem, *,
                     n_zero):
    tm = h2_ref.shape[0]
    i = pl.program_id(0)

    def zero_copy(j):
        row0 = pl.multiple_of(zero_row_ref[j], MOE_SUB)
        return pltpu.make_async_copy(zbuf, xs_hbm.at[pl.ds(row0, MOE_SUB)], zsem)

    @pl.when(i == 0)
    def _():
        zbuf[...] = jnp.zeros_like(zbuf)

        def start(j, carry):
            @pl.when(zero_row_ref[j] >= 0)
            def _():
                zero_copy(j).start()
            return carry

        def wait(j, carry):
            @pl.when(zero_row_ref[j] >= 0)
            def _():
                zero_copy(j).wait()
            return carry

        lax.fori_loop(0, n_zero, start, 0)
        lax.fori_loop(0, n_zero, wait, 0)

    def row_copy(r, k):
        d = _grouped_row(route_ref, start_ref, i * tm + r, k)
        return pltpu.make_async_copy(h2_ref.at[pl.ds(r, 1)], xs_hbm.at[pl.ds(d, 1)], sem)

    def start_row(r, carry):
        for k in range(TOP_K):
            row_copy(r, k).start()
        return carry

    def wait_row(r, carry):
        for k in range(TOP_K):
            row_copy(r, k).wait()
        return carry

    lax.fori_loop(0, tm, start_row, 0)
    lax.fori_loop(0, tm, wait_row, 0)


def _dispatch(route, starts, zero_rows, h2, n_rows):
    t = h2.shape[0]
    tm = DISPATCH_TM
    grid_spec = pltpu.PrefetchScalarGridSpec(
        num_scalar_prefetch=3,
        grid=(t // tm,),
        in_specs=[pl.BlockSpec((tm, D_MODEL), lambda i, r, s, z: (i, 0))],
        out_specs=pl.BlockSpec(memory_space=pl.ANY),
        scratch_shapes=[
            pltpu.VMEM((MOE_SUB, D_MODEL), F32),
            pltpu.SemaphoreType.DMA,
            pltpu.SemaphoreType.DMA,
        ],
    )
    return pl.pallas_call(
        functools.partial(_dispatch_kernel, n_zero=zero_rows.shape[0]),
        grid_spec=grid_spec,
        out_shape=jax.ShapeDtypeStruct((n_rows, D_MODEL), F32),
        compiler_params=_cparams(("arbitrary",)),
        name="dispatch",
    )(route, starts, zero_rows, h2)


def _experts_kernel(tile_e_ref, tile_occ_ref, tile_row_ref,
                    xs_hbm, wg_ref, wu_ref, bg_ref, bu_ref, wd_ref, bd_ref, out_ref,
                    xbuf, xb_scr, act_scr, sem, *, n_tiles):
    del tile_e_ref
    i = pl.program_id(0)
    j = pl.program_id(1)
    occ = tile_occ_ref[i]

    def sub_rows(sub):
        return pl.ds(pl.multiple_of(sub * MOE_SUB, MOE_SUB), MOE_SUB)

    def sub_copy(tile, sub):
        row0 = pl.multiple_of(tile_row_ref[tile] + sub * MOE_SUB, MOE_SUB)
        return pltpu.make_async_copy(xs_hbm.at[pl.ds(row0, MOE_SUB)], xbuf.at[sub_rows(sub)], sem)

    def start_load(tile):
        def body(sub, carry):
            sub_copy(tile, sub).start()
            return carry
        lax.fori_loop(0, tile_occ_ref[tile], body, 0)

    @pl.when(j == 0)
    def _():
        @pl.when(i == 0)
        def _():
            start_load(0)

        def wait_sub(sub, carry):
            sub_copy(i, sub).wait()
            return carry

        def round_sub(sub, carry):
            xb_scr[sub_rows(sub), :] = xbuf[sub_rows(sub), :].astype(BF16)
            return carry

        lax.fori_loop(0, occ, wait_sub, 0)
        lax.fori_loop(0, occ, round_sub, 0)

        @pl.when(i + 1 < n_tiles)
        def _():
            start_load(i + 1)

    def for_each_sub(fn):
        def pair(p, carry):
            fn(2 * p)
            fn(2 * p + 1)
            return carry

        lax.fori_loop(0, occ // 2, pair, 0)

        @pl.when(occ % 2 == 1)
        def _():
            fn(occ - 1)

    @pl.when((j < MOE_NF) & (occ > 0))
    def _():
        wg = wg_ref[...].astype(BF16)
        wu = wu_ref[...].astype(BF16)

        def gate_up(sub):
            rows = sub_rows(sub)
            xb = xb_scr[rows, :]
            gate = jnp.dot(xb, wg, preferred_element_type=F32) + bg_ref[...]
            up = jnp.dot(xb, wu, preferred_element_type=F32) + bu_ref[...]
            gate = jnp.minimum(gate, SWIGLU_LIMIT)
            up = jnp.clip(up, -SWIGLU_LIMIT, SWIGLU_LIMIT)
            act = (up + 1.0) * (gate * _sigmoid(SWIGLU_ALPHA * gate))
            act_scr[j, rows, :] = act.astype(BF16)

        for_each_sub(gate_up)

    @pl.when((j >= MOE_NF) & (occ > 0))
    def _():
        wd = wd_ref[...].astype(BF16)

        def down(sub):
            rows = sub_rows(sub)
            y = bd_ref[...] + jnp.dot(act_scr[0, rows, :], wd[0:MOE_TF, :],
                                      preferred_element_type=F32)
            for c in range(1, MOE_NF):
                y = y + jnp.dot(act_scr[c, rows, :], wd[c * MOE_TF:(c + 1) * MOE_TF, :],
                                preferred_element_type=F32)
            out_ref[rows, :] = y

        for_each_sub(down)

    @pl.when(j >= MOE_NF)
    def _():
        def zero_sub(sub, carry):
            out_ref[sub_rows(sub), :] = jnp.zeros((MOE_SUB, MOE_TN), F32)
            return carry

        lax.fori_loop(occ, MOE_SUBS, zero_sub, 0)


def _experts(tile_e, tile_occ, tile_row, xs, wgu, bgu, wd, bd, n_tiles):
    up_off = D_FF // MOE_TF

    def fa(i, j, to):
        return jnp.where(to[i] > 0, jnp.minimum(j, MOE_NF - 1), MOE_NF - 1)

    def fb(i, j, to):
        return jnp.where(to[i] > 0, jnp.maximum(j - MOE_NF, 0), MOE_NN - 1)

    def gate_map(i, j, te, to, tr):
        return (te[i], 0, fa(i, j, to))

    def up_map(i, j, te, to, tr):
        return (te[i], 0, up_off + fa(i, j, to))

    def down_map(i, j, te, to, tr):
        return (te[i], 0, fb(i, j, to))

    def out_map(i, j, te, to, tr):
        return (i, jnp.maximum(j - MOE_NF, 0))

    grid_spec = pltpu.PrefetchScalarGridSpec(
        num_scalar_prefetch=3,
        grid=(n_tiles, MOE_NF + MOE_NN),
        in_specs=[
            pl.BlockSpec(memory_space=pl.ANY),
            pl.BlockSpec((None, D_MODEL, MOE_TF), gate_map),
            pl.BlockSpec((None, D_MODEL, MOE_TF), up_map),
            pl.BlockSpec((None, 1, MOE_TF), gate_map),
            pl.BlockSpec((None, 1, MOE_TF), up_map),
            pl.BlockSpec((None, D_FF, MOE_TN), down_map),
            pl.BlockSpec((None, 1, MOE_TN), down_map),
        ],
        out_specs=pl.BlockSpec((MOE_CAP, MOE_TN), out_map),
        scratch_shapes=[
            pltpu.VMEM((MOE_CAP, D_MODEL), F32),
            pltpu.VMEM((MOE_CAP, D_MODEL), BF16),
            pltpu.VMEM((MOE_NF, MOE_CAP, MOE_TF), BF16),
            pltpu.SemaphoreType.DMA,
        ],
    )
    return pl.pallas_call(
        functools.partial(_experts_kernel, n_tiles=n_tiles),
        grid_spec=grid_spec,
        out_shape=jax.ShapeDtypeStruct((n_tiles * MOE_CAP, D_MODEL), F32),
        compiler_params=_cparams(("arbitrary", "arbitrary")),
        name="experts",
    )(tile_e, tile_occ, tile_row, xs, wgu, wgu, bgu, bgu, wd, bd)


def _combine_kernel(route_ref, start_ref, ys_hbm, x1_ref, gate_ref, gf_ref, out_ref, buf, sem, *,
                    n_tiles):
    tm = x1_ref.shape[0]
    i = pl.program_id(0)
    slot = i % 2

    def row_copy(tile, s, r, k):
        d = _grouped_row(route_ref, start_ref, tile * tm + r, k)
        return pltpu.make_async_copy(ys_hbm.at[pl.ds(d, 1)], buf.at[s, k, pl.ds(r, 1)], sem.at[s])

    def start_gather(tile, s):
        def per_row(r, carry):
            for k in range(TOP_K):
                row_copy(tile, s, r, k).start()
            return carry
        lax.fori_loop(0, tm, per_row, 0)

    def wait_gather(tile, s):
        def per_row(r, carry):
            for k in range(TOP_K):
                row_copy(tile, s, r, k).wait()
            return carry
        lax.fori_loop(0, tm, per_row, 0)

    @pl.when(i == 0)
    def _():
        start_gather(0, 0)

    wait_gather(i, slot)

    @pl.when(i + 1 < n_tiles)
    def _():
        start_gather(i + 1, 1 - slot)

    x2 = x1_ref[...]
    for k in range(TOP_K):
        x2 = x2 + gate_ref[:, k:k + 1] * buf[slot, k]
    out_ref[...] = x2 * lax.rsqrt(jnp.mean(x2 * x2, axis=-1, keepdims=True) + EPS) * gf_ref[...]


def _combine(route, starts, ys, x1, gate, gf):
    t = x1.shape[0]
    tm = COMBINE_TM
    n_tiles = t // tm
    grid_spec = pltpu.PrefetchScalarGridSpec(
        num_scalar_prefetch=2,
        grid=(n_tiles,),
        in_specs=[
            pl.BlockSpec(memory_space=pl.ANY),
            pl.BlockSpec((tm, D_MODEL), lambda i, r, s: (i, 0)),
            pl.BlockSpec((tm, LANES), lambda i, r, s: (i, 0)),
            pl.BlockSpec((1, D_MODEL), lambda i, r, s: (0, 0)),
        ],
        out_specs=pl.BlockSpec((tm, D_MODEL), lambda i, r, s: (i, 0)),
        scratch_shapes=[
            pltpu.VMEM((2, TOP_K, tm, D_MODEL), F32),
            pltpu.SemaphoreType.DMA((2,)),
        ],
    )
    return pl.pallas_call(
        functools.partial(_combine_kernel, n_tiles=n_tiles),
        grid_spec=grid_spec,
        out_shape=jax.ShapeDtypeStruct((t, D_MODEL), F32),
        compiler_params=_cparams(("arbitrary",)),
        name="combine",
    )(route, starts, ys, x1, gate, gf)


def _routing_tables(counts, n_tokens):
    n_tiles = N_EXPERTS + (n_tokens * TOP_K) // MOE_CAP
    n_subs = N_EXPERTS + (n_tokens * TOP_K) // MOE_SUB
    subs_per = (counts + MOE_SUB - 1) // MOE_SUB
    sub_end = jnp.cumsum(subs_per)
    x_start = (sub_end - subs_per) * MOE_SUB
    tiles_per = (counts + MOE_CAP - 1) // MOE_CAP
    tile_end = jnp.cumsum(tiles_per)
    tile_start = tile_end - tiles_per
    y_start = tile_start * MOE_CAP

    tile_ids = jnp.arange(n_tiles, dtype=I32)
    n_active = tile_end[-1]
    last = jnp.maximum(n_active - 1, 0)
    active = tile_ids < n_active
    tile_eff = jnp.minimum(tile_ids, last)
    tile_e = jnp.sum((tile_eff[:, None] >= tile_end[None, :]).astype(I32), axis=1)
    tile_e = jnp.minimum(tile_e, N_EXPERTS - 1)
    in_expert = tile_eff - tile_start[tile_e]
    rows_in = counts[tile_e] - in_expert * MOE_CAP
    occ = jnp.clip((rows_in + MOE_SUB - 1) // MOE_SUB, 0, MOE_SUBS)
    tile_occ = jnp.where(active, occ, 0).astype(I32)
    tile_row = (x_start[tile_e] + in_expert * MOE_CAP).astype(I32)

    last_sub = jnp.where(counts > 0, x_start + (subs_per - 1) * MOE_SUB, -1)
    tail_ids = sub_end[-1] + jnp.arange(N_EXPERTS, dtype=I32)
    tail = jnp.where(tail_ids < n_subs, tail_ids * MOE_SUB, -1)
    zero_rows = jnp.concatenate([last_sub, tail]).astype(I32)
    return (n_tiles, n_subs * MOE_SUB, tile_e.astype(I32), tile_occ, tile_row,
            x_start, y_start, zero_rows)


def _layer(x2d, bsz, seq, g_norm_mix, w_in, b_in, w_conv, b_conv, w_sgu, b_sgu, g_sgu_ln, b_sgu_ln,
           g_mlstm_norm, w_proj_sgu, w_proj_mlstm, w_out, g_norm_moe, w_router, b_router,
           w_gate_up, b_gate_up, w_down, b_down, g_out):
    t = x2d.shape[0]

    pad = LANES - 2 * HEADS
    w_if = jnp.pad(w_in[:, N_MAIN:GATE_OFF], ((0, 0), (0, pad))).astype(BF16)
    b_if = jnp.pad(b_in[N_MAIN:GATE_OFF], (0, pad))[None, :]
    w_in_t = w_in.T
    w_mg_t = w_in_t[GATE_OFF:]
    bsb = jnp.broadcast_to(b_sgu.T[:, :, None], (SGU_BLOCK, SGU_GROUPS, SGU_GROUP_DIM))
    bsb = bsb.reshape(SGU_BLOCK, D_MODEL)
    w_r = jnp.pad(w_router, ((0, 0), (0, LANES - N_EXPERTS)))
    w_r_hi = w_r.astype(BF16)
    w_r_lo = (w_r - w_r_hi.astype(F32)).astype(BF16)
    b_r = jnp.pad(b_router, (0, LANES - N_EXPERTS))[None, :]

    h, gates = _norm_gates(x2d, g_norm_mix[None, :], w_if, b_if)
    zm = _in_proj(h, w_in_t, b_in[None, :], N_MAIN, 2 * D_MODEL, 3 * D_MODEL, "in_proj_main")
    zg = _in_proj(h, w_mg_t, b_in[None, GATE_OFF:], 2 * D_MODEL, 0, 0, "in_proj_gate")
    a = _sgu(zm, zg, w_sgu, bsb, g_sgu_ln[None, :], b_sgu_ln[None, :], w_proj_sgu.astype(BF16))
    ym = _mlstm(zm, gates, w_conv, b_conv[None, :], g_mlstm_norm[None, :], bsz, seq)
    x1, h2, logits = _merge(ym, a, zg, x2d, w_proj_mlstm.astype(BF16), w_out.astype(BF16),
                            g_norm_moe[None, :], w_r_hi, w_r_lo, b_r)
    info, gate, cnt = _route(logits)

    counts = cnt[0, :N_EXPERTS].astype(I32)
    (n_tiles, n_xrows, tile_e, tile_occ, tile_row, x_start, y_start,
     zero_rows) = _routing_tables(counts, t)
    route = info[:, :2 * TOP_K].reshape(-1)

    xs = _dispatch(route, x_start, zero_rows, h2, n_xrows)
    ys = _experts(tile_e, tile_occ, tile_row, xs, w_gate_up, b_gate_up[:, None, :],
                  w_down, b_down[:, None, :], n_tiles)
    return _combine(route, y_start, ys, x1, gate, g_out[None, :])


def kernel(x, g_norm_mix, w_in, b_in, w_conv, b_conv, w_sgu, b_sgu, g_sgu_ln, b_sgu_ln, g_mlstm_norm,
           w_proj_sgu, w_proj_mlstm, w_out, g_norm_moe, w_router, b_router, w_gate_up, b_gate_up,
           w_down, b_down, g_final):
    bsz, seq, d = x.shape
    assert d == D_MODEL and w_in.shape[0] == 1, "single-layer block with d_model 2048"
    out = _layer(x.reshape(bsz * seq, d), bsz, seq, g_norm_mix[0], w_in[0], b_in[0], w_conv[0],
                 b_conv[0], w_sgu[0], b_sgu[0], g_sgu_ln[0], b_sgu_ln[0], g_mlstm_norm[0],
                 w_proj_sgu[0], w_proj_mlstm[0], w_out[0], g_norm_moe[0], w_router[0], b_router[0],
                 w_gate_up[0], b_gate_up[0], w_down[0], b_down[0], g_final)
    return out.reshape(bsz, seq, d)
```

```python
import functools

import jax
import jax.numpy as jnp
from jax import lax
from jax.experimental import pallas as pl
from jax.experimental.pallas import tpu as pltpu

F32 = jnp.float32
BF16 = jnp.bfloat16
I32 = jnp.int32

D_MODEL = 2048
CHUNK = 64
SGU_BLOCK = 128
SGU_GROUPS = 8
SGU_GROUP_DIM = 256
HEADS = 4
HEAD_DIM = 512
CONV_WIDTH = 4
N_EXPERTS = 32
TOP_K = 4
D_FF = 2048
SWIGLU_LIMIT = 7.0
SWIGLU_ALPHA = 1.702
EPS = 1e-6

LANES = 128
SUBLANES = 8
VMEM_LIMIT = 56 * 1024 * 1024

ZB_U, ZB_V, ZB_Q, ZB_K, ZB_VM, ZB_O = range(6)
ZG_A, ZG_M = range(2)
N_MAIN = 6 * D_MODEL
GATE_OFF = N_MAIN + 2 * HEADS

NORM_TM = 512
INPROJ_TM = 1024
INPROJ_TN = 1024
INPROJ_RC = 512
SGU_TM = 256
MLSTM_L = 256
MERGE_TM = 256
ROUTE_TM = 256
DISPATCH_TM = 256
MOE_SUB = 256
MOE_CAP = 1280
MOE_SUBS = MOE_CAP // MOE_SUB
MOE_TF = 512
MOE_NF = D_FF // MOE_TF
MOE_TN = 512
MOE_NN = D_MODEL // MOE_TN
COMBINE_TM = 128


def _cparams(sem, vmem_limit=VMEM_LIMIT):
    return pltpu.CompilerParams(dimension_semantics=sem, vmem_limit_bytes=vmem_limit)


def _gelu(x):
    return 0.5 * x * (1.0 + lax.erf(x * (2.0 ** -0.5)))


def _sigmoid(x):
    return 1.0 / (1.0 + jnp.exp(-x))


def _norm_gates_kernel(x_ref, g_ref, wif_ref, bif_ref, h_ref, gates_ref):
    x = x_ref[...]
    ms = jnp.mean(x * x, axis=-1, keepdims=True)
    h = (x * lax.rsqrt(ms + EPS) * g_ref[...]).astype(BF16)
    h_ref[...] = h
    gates_ref[...] = jnp.dot(h, wif_ref[...], preferred_element_type=F32) + bif_ref[...]


def _norm_gates(x, g, wif, bif):
    t = x.shape[0]
    tm = min(NORM_TM, t)
    row = lambda i: (i, 0)
    const = lambda i: (0, 0)
    return pl.pallas_call(
        _norm_gates_kernel,
        grid=(t // tm,),
        in_specs=[
            pl.BlockSpec((tm, D_MODEL), row),
            pl.BlockSpec((1, D_MODEL), const),
            pl.BlockSpec((D_MODEL, LANES), const),
            pl.BlockSpec((1, LANES), const),
        ],
        out_specs=[pl.BlockSpec((tm, D_MODEL), row), pl.BlockSpec((tm, LANES), row)],
        out_shape=[jax.ShapeDtypeStruct((t, D_MODEL), BF16), jax.ShapeDtypeStruct((t, LANES), F32)],
        compiler_params=_cparams(("arbitrary",)),
        name="norm_gates",
    )(x, g, wif, bif)


def _inproj_kernel(h_ref, w_ref, b_ref, z_ref, w_scr, *, n_gelu, n_plain):
    j = pl.program_id(0)

    @pl.when(pl.program_id(1) == 0)
    def _():
        w_scr[...] = w_ref[...].astype(BF16)

    def project(activation):
        tm = h_ref.shape[0]
        rc = min(INPROJ_RC, tm)
        for c in range(tm // rc):
            rows = slice(c * rc, (c + 1) * rc)
            acc = lax.dot_general(h_ref[rows, :], w_scr[...], (((1,), (1,)), ((), ())),
                                  preferred_element_type=F32) + b_ref[...]
            z_ref[rows, :] = activation(acc).astype(z_ref.dtype)

    @pl.when(j < n_gelu)
    def _():
        project(_gelu)

    @pl.when((j >= n_gelu) & (j < n_gelu + n_plain))
    def _():
        project(lambda acc: acc)

    @pl.when(j >= n_gelu + n_plain)
    def _():
        project(_sigmoid)


def _in_proj(h, w_t, b, n_cols, n_gelu_cols, n_plain_cols, name):
    t = h.shape[0]
    tm, tn = min(INPROJ_TM, t), INPROJ_TN
    return pl.pallas_call(
        functools.partial(_inproj_kernel, n_gelu=n_gelu_cols // tn, n_plain=n_plain_cols // tn),
        grid=(n_cols // tn, t // tm),
        in_specs=[
            pl.BlockSpec((tm, D_MODEL), lambda j, i: (i, 0)),
            pl.BlockSpec((tn, D_MODEL), lambda j, i: (j, 0)),
            pl.BlockSpec((1, tn), lambda j, i: (0, j)),
        ],
        out_specs=pl.BlockSpec((tm, tn), lambda j, i: (i, j)),
        out_shape=jax.ShapeDtypeStruct((t, n_cols), BF16),
        scratch_shapes=[pltpu.VMEM((tn, D_MODEL), BF16)],
        compiler_params=_cparams(("arbitrary", "arbitrary")),
        name=name,
    )(h, w_t, b)


def _sgu_kernel(gu_ref, gv_ref, sga_ref, ws_ref, bsb_ref, lng_ref, lnb_ref, pa_ref, out_ref,
                ya_scr):
    tm = gu_ref.shape[0]
    v = gv_ref[...].astype(F32)
    mu = jnp.mean(v, axis=-1, keepdims=True)
    vc = v - mu
    var = jnp.mean(vc * vc, axis=-1, keepdims=True)
    vln = (vc * lax.rsqrt(var + EPS) * lng_ref[...] + lnb_ref[...]).astype(BF16)

    t_id = lax.broadcasted_iota(I32, (SGU_BLOCK, SGU_BLOCK), 0) // CHUNK
    s_id = lax.broadcasted_iota(I32, (SGU_BLOCK, SGU_BLOCK), 1) // CHUNK
    causal = s_id <= t_id
    for g in range(SGU_GROUPS):
        w = jnp.where(causal, ws_ref[g], 0.0).astype(BF16)
        cols = slice(g * SGU_GROUP_DIM, (g + 1) * SGU_GROUP_DIM)
        bias = bsb_ref[:, cols]
        for blk in range(tm // SGU_BLOCK):
            rows = slice(blk * SGU_BLOCK, (blk + 1) * SGU_BLOCK)
            mixed = jnp.dot(w, vln[rows, cols], preferred_element_type=F32) + bias
            ya_scr[rows, cols] = (gu_ref[rows, cols].astype(F32) * mixed).astype(BF16)

    proj = jnp.dot(ya_scr[...], pa_ref[...], preferred_element_type=F32)
    out_ref[...] = (sga_ref[...].astype(F32) * proj).astype(out_ref.dtype)


def _sgu(zm, zg, ws, bsb, lng, lnb, pa):
    t = zm.shape[0]
    tm = SGU_TM
    const2 = lambda i: (0, 0)
    return pl.pallas_call(
        _sgu_kernel,
        grid=(t // tm,),
        in_specs=[
            pl.BlockSpec((tm, D_MODEL), lambda i: (i, ZB_U)),
            pl.BlockSpec((tm, D_MODEL), lambda i: (i, ZB_V)),
            pl.BlockSpec((tm, D_MODEL), lambda i: (i, ZG_A)),
            pl.BlockSpec((SGU_GROUPS, SGU_BLOCK, SGU_BLOCK), lambda i: (0, 0, 0)),
            pl.BlockSpec((SGU_BLOCK, D_MODEL), const2),
            pl.BlockSpec((1, D_MODEL), const2),
            pl.BlockSpec((1, D_MODEL), const2),
            pl.BlockSpec((D_MODEL, D_MODEL), const2),
        ],
        out_specs=pl.BlockSpec((tm, D_MODEL), lambda i: (i, 0)),
        out_shape=jax.ShapeDtypeStruct((t, D_MODEL), BF16),
        scratch_shapes=[pltpu.VMEM((tm, D_MODEL), BF16)],
        compiler_params=_cparams(("arbitrary",)),
        name="sgu",
    )(zm, zm, zg, ws, bsb, lng, lnb, pa)


def _mlstm_kernel(q_ref, k_ref, v_ref, so_ref, gates_ref, wc_ref, bc_ref, gn_ref, out_ref,
                  hist, ct_scr, n_scr, m_scr):
    L = q_ref.shape[0]
    c = pl.program_id(1)

    @pl.when(c == 0)
    def _():
        hist[0:SUBLANES, :] = jnp.zeros((SUBLANES, 2 * D_MODEL), F32)
        ct_scr[...] = jnp.zeros_like(ct_scr)
        n_scr[...] = jnp.zeros_like(n_scr)
        m_scr[...] = jnp.zeros_like(m_scr)

    hist[SUBLANES:SUBLANES + L, 0:D_MODEL] = q_ref[...].astype(F32)
    hist[SUBLANES:SUBLANES + L, D_MODEL:2 * D_MODEL] = k_ref[...].astype(F32)
    qk = bc_ref[...] + wc_ref[CONV_WIDTH - 1:CONV_WIDTH, :] * hist[SUBLANES:SUBLANES + L, :]
    for j in range(CONV_WIDTH - 1):
        off = SUBLANES - (CONV_WIDTH - 1) + j
        qk = qk + wc_ref[j:j + 1, :] * hist[off:off + L, :]
    qk = qk * _sigmoid(qk)
    hist[0:SUBLANES, :] = hist[L:L + SUBLANES, :]

    gts = gates_ref[...]
    lf = jnp.minimum(gts, 0.0) - jnp.log(1.0 + jnp.exp(-jnp.abs(gts)))
    row = lax.broadcasted_iota(I32, (L, L), 0)
    col = lax.broadcasted_iota(I32, (L, L), 1)
    tril = col <= row
    tri = jnp.where(tril, 1.0, 0.0).astype(BF16)
    lf_hi = lf.astype(BF16)
    lf_lo = (lf - lf_hi.astype(F32)).astype(BF16)
    bcum = (jnp.dot(tri, lf_hi, preferred_element_type=F32)
            + jnp.dot(tri, lf_lo, preferred_element_type=F32))
    bcum_t = bcum.T
    gts_t = gts.T

    scale = HEAD_DIM ** -0.5
    for h in range(HEADS):
        cols = slice(h * HEAD_DIM, (h + 1) * HEAD_DIM)
        q = qk[:, h * HEAD_DIM:(h + 1) * HEAD_DIM] * scale
        k = qk[:, D_MODEL + h * HEAD_DIM:D_MODEL + (h + 1) * HEAD_DIM]
        qb = q.astype(BF16)
        kb = k.astype(BF16)
        vb = v_ref[:, cols]
        bc = bcum[:, HEADS + h:HEADS + h + 1]
        bc_r = bcum_t[HEADS + h:HEADS + h + 1, :]
        ig_c = gts[:, h:h + 1]
        ig_r = gts_t[h:h + 1, :]
        m_st = m_scr[h]
        n_st = n_scr[h]

        dlog = jnp.where(tril, bc - bc_r + ig_r, -jnp.inf)
        m_inter = bc + m_st
        m_t = jnp.maximum(m_inter, jnp.max(dlog, axis=-1, keepdims=True))
        p = jnp.exp(dlog - m_t)
        s = lax.dot_general(qb, kb, (((1,), (1,)), ((), ())), preferred_element_type=F32)
        sc = s * p
        inter = jnp.exp(m_inter - m_t)
        ctb = ct_scr[h].astype(BF16)
        num = (jnp.dot(sc.astype(BF16), vb, preferred_element_type=F32)
               + inter * jnp.dot(qb, ctb, preferred_element_type=F32))
        den = (jnp.sum(sc, axis=-1, keepdims=True)
               + inter * jnp.sum(q * n_st, axis=-1, keepdims=True))
        hh = num / jnp.maximum(jnp.abs(den), jnp.exp(-m_t))
        hn = hh * lax.rsqrt(jnp.mean(hh * hh, axis=-1, keepdims=True) + EPS) * gn_ref[:, cols]
        out_ref[:, cols] = (hn * so_ref[:, cols].astype(F32)).astype(out_ref.dtype)

        g_tot = bc[L - 1:L, :]
        a = g_tot - bc + ig_c
        m_new = jnp.maximum(g_tot + m_st, jnp.max(a, axis=0, keepdims=True))
        wa = jnp.exp(a - m_new)
        decay = jnp.exp(g_tot + m_st - m_new)
        wv = (wa * vb.astype(F32)).astype(BF16)
        upd = jnp.dot(k.T.astype(BF16), wv, preferred_element_type=F32)
        ct_scr[h] = decay * ct_scr[h] + upd
        n_scr[h] = decay * n_st + jnp.sum(wa * k, axis=0, keepdims=True)
        m_scr[h] = m_new


def _mlstm(zm, gates, wc, bc, gn, bsz, seq):
    t = zm.shape[0]
    L = MLSTM_L
    nc = seq // L
    zspec = lambda blk: pl.BlockSpec((L, D_MODEL), lambda b, c: (b * nc + c, blk))
    const2 = lambda b, c: (0, 0)
    return pl.pallas_call(
        _mlstm_kernel,
        grid=(bsz, nc),
        in_specs=[
            zspec(ZB_Q), zspec(ZB_K), zspec(ZB_VM), zspec(ZB_O),
            pl.BlockSpec((L, LANES), lambda b, c: (b * nc + c, 0)),
            pl.BlockSpec((CONV_WIDTH, 2 * D_MODEL), const2),
            pl.BlockSpec((1, 2 * D_MODEL), const2),
            pl.BlockSpec((1, D_MODEL), const2),
        ],
        out_specs=pl.BlockSpec((L, D_MODEL), lambda b, c: (b * nc + c, 0)),
        out_shape=jax.ShapeDtypeStruct((t, D_MODEL), BF16),
        scratch_shapes=[
            pltpu.VMEM((L + 2 * SUBLANES, 2 * D_MODEL), F32),
            pltpu.VMEM((HEADS, HEAD_DIM, HEAD_DIM), F32),
            pltpu.VMEM((HEADS, 1, HEAD_DIM), F32),
            pltpu.VMEM((HEADS, 1, 1), F32),
        ],
        compiler_params=_cparams(("arbitrary", "arbitrary")),
        name="mlstm",
    )(zm, zm, zm, zm, gates, wc, bc, gn)


def _merge_kernel(ym_ref, a_ref, sgm_ref, x_ref, pm_ref, wo_ref, g2_ref, wrh_ref, wrl_ref, br_ref,
                  x1_ref, h2_ref, lg_ref):
    proj = jnp.dot(ym_ref[...], pm_ref[...], preferred_element_type=F32)
    merged = a_ref[...].astype(F32) + sgm_ref[...].astype(F32) * proj
    x1 = x_ref[...] + jnp.dot(merged.astype(BF16), wo_ref[...], preferred_element_type=F32)
    x1_ref[...] = x1
    h2 = x1 * lax.rsqrt(jnp.mean(x1 * x1, axis=-1, keepdims=True) + EPS) * g2_ref[...]
    h2_ref[...] = h2
    h_hi = h2.astype(BF16)
    h_lo = (h2 - h_hi.astype(F32)).astype(BF16)
    lg_ref[...] = (jnp.dot(h_hi, wrh_ref[...], preferred_element_type=F32)
                   + (jnp.dot(h_lo, wrh_ref[...], preferred_element_type=F32)
                      + jnp.dot(h_hi, wrl_ref[...], preferred_element_type=F32))
                   + br_ref[...])


def _merge(ym, a, zg, x, pm, wo, g2, wr_hi, wr_lo, br):
    t = x.shape[0]
    tm = MERGE_TM
    const2 = lambda i: (0, 0)
    row = lambda i: (i, 0)
    return pl.pallas_call(
        _merge_kernel,
        grid=(t // tm,),
        in_specs=[
            pl.BlockSpec((tm, D_MODEL), row),
            pl.BlockSpec((tm, D_MODEL), row),
            pl.BlockSpec((tm, D_MODEL), lambda i: (i, ZG_M)),
            pl.BlockSpec((tm, D_MODEL), row),
            pl.BlockSpec((D_MODEL, D_MODEL), const2),
            pl.BlockSpec((D_MODEL, D_MODEL), const2),
            pl.BlockSpec((1, D_MODEL), const2),
            pl.BlockSpec((D_MODEL, LANES), const2),
            pl.BlockSpec((D_MODEL, LANES), const2),
            pl.BlockSpec((1, LANES), const2),
        ],
        out_specs=[
            pl.BlockSpec((tm, D_MODEL), row),
            pl.BlockSpec((tm, D_MODEL), row),
            pl.BlockSpec((tm, LANES), row),
        ],
        out_shape=[
            jax.ShapeDtypeStruct((t, D_MODEL), F32),
            jax.ShapeDtypeStruct((t, D_MODEL), F32),
            jax.ShapeDtypeStruct((t, LANES), F32),
        ],
        compiler_params=_cparams(("arbitrary",)),
        name="merge",
    )(ym, a, zg, x, pm, wo, g2, wr_hi, wr_lo, br)


def _route_kernel(lg_ref, info_ref, gate_ref, cnt_ref, cnt_scr):
    tm = lg_ref.shape[0]
    i = pl.program_id(0)

    @pl.when(i == 0)
    def _():
        cnt_scr[...] = jnp.zeros_like(cnt_scr)

    lane = lax.broadcasted_iota(I32, (tm, LANES), 1)
    lane_f = lane.astype(F32)
    vals = jnp.where(lane < N_EXPERTS, lg_ref[...], -jnp.inf)
    tops, ids, hots = [], [], []
    for _ in range(TOP_K):
        m = jnp.max(vals, axis=-1, keepdims=True)
        idx_f = jnp.min(jnp.where(vals == m, lane_f, float(LANES)), axis=-1, keepdims=True)
        idx = idx_f.astype(I32)
        hot = lane == idx
        tops.append(m)
        ids.append(idx)
        hots.append(hot)
        vals = jnp.where(hot, -jnp.inf, vals)

    exps = [jnp.exp(tv - tops[0]) for tv in tops]
    denom = exps[0] + exps[1] + exps[2] + exps[3]

    sel = jnp.zeros((tm, LANES), F32)
    for hot in hots:
        sel = sel + jnp.where(hot, 1.0, 0.0)
    r = lax.broadcasted_iota(I32, (tm, tm), 0)
    c = lax.broadcasted_iota(I32, (tm, tm), 1)
    strict = jnp.where(c < r, 1.0, 0.0).astype(BF16)
    ahead = jnp.dot(strict, sel.astype(BF16), preferred_element_type=F32) + cnt_scr[...]

    info = jnp.zeros((tm, LANES), I32)
    gate = jnp.zeros((tm, LANES), F32)
    for k in range(TOP_K):
        pos = jnp.sum(jnp.where(hots[k], ahead, 0.0), axis=-1, keepdims=True).astype(I32)
        info = jnp.where(lane == k, ids[k], info)
        info = jnp.where(lane == TOP_K + k, pos, info)
        gate = jnp.where(lane == k, exps[k] / denom, gate)
    info_ref[...] = info
    gate_ref[...] = gate
    cnt_scr[...] = cnt_scr[...] + jnp.sum(sel, axis=0, keepdims=True)
    cnt_ref[...] = jnp.broadcast_to(cnt_scr[...], cnt_ref.shape)


def _route(logits):
    t = logits.shape[0]
    tm = ROUTE_TM
    row = lambda i: (i, 0)
    return pl.pallas_call(
        _route_kernel,
        grid=(t // tm,),
        in_specs=[pl.BlockSpec((tm, LANES), row)],
        out_specs=[
            pl.BlockSpec((tm, LANES), row),
            pl.BlockSpec((tm, LANES), row),
            pl.BlockSpec((SUBLANES, LANES), lambda i: (0, 0)),
        ],
        out_shape=[
            jax.ShapeDtypeStruct((t, LANES), I32),
            jax.ShapeDtypeStruct((t, LANES), F32),
            jax.ShapeDtypeStruct((SUBLANES, LANES), F32),
        ],
        scratch_shapes=[pltpu.VMEM((1, LANES), F32)],
        compiler_params=_cparams(("arbitrary",)),
        name="route",
    )(logits)


def _grouped_row(route_ref, start_ref, token, k):
    base = token * (2 * TOP_K)
    return start_ref[route_ref[base + k]] + route_ref[base + TOP_K + k]


def _dispatch_kernel(route_ref, start_ref, zero_row_ref, h2_ref, xs_hbm, zbuf, sem, zsem, *,
                     n_zero):
    tm = h2_ref.shape[0]
    i = pl.program_id(0)

    def zero_copy(j):
        row0 = pl.multiple_of(zero_row_ref[j], MOE_SUB)
        return pltpu.make_async_copy(zbuf, xs_hbm.at[pl.ds(row0, MOE_SUB)], zsem)

    @pl.when(i == 0)
    def _():
        zbuf[...] = jnp.zeros_like(zbuf)

        def start(j, carry):
            @pl.when(zero_row_ref[j] >= 0)
            def _():
                zero_copy(j).start()
            return carry

        def wait(j, carry):
            @pl.when(zero_row_ref[j] >= 0)
            def _():
                zero_copy(j).wait()
            return carry

        lax.fori_loop(0, n_zero, start, 0)
        lax.fori_loop(0, n_zero, wait, 0)

    def row_copy(r, k):
        d = _grouped_row(route_ref, start_ref, i * tm + r, k)
        return pltpu.make_async_copy(h2_ref.at[pl.ds(r, 1)], xs_hbm.at[pl.ds(d, 1)], sem)

    def start_row(r, carry):
        for k in range(TOP_K):
            row_copy(r, k).start()
        return carry

    lax.fori_loop(0, tm, start_row, 0)
    for _ in range(TOP_K):
        pltpu.make_async_copy(h2_ref, xs_hbm.at[pl.ds(0, tm)], sem).wait()


def _dispatch(route, starts, zero_rows, h2, n_rows):
    t = h2.shape[0]
    tm = DISPATCH_TM
    grid_spec = pltpu.PrefetchScalarGridSpec(
        num_scalar_prefetch=3,
        grid=(t // tm,),
        in_specs=[pl.BlockSpec((tm, D_MODEL), lambda i, r, s, z: (i, 0))],
        out_specs=pl.BlockSpec(memory_space=pl.ANY),
        scratch_shapes=[
            pltpu.VMEM((MOE_SUB, D_MODEL), F32),
            pltpu.SemaphoreType.DMA,
            pltpu.SemaphoreType.DMA,
        ],
    )
    return pl.pallas_call(
        functools.partial(_dispatch_kernel, n_zero=zero_rows.shape[0]),
        grid_spec=grid_spec,
        out_shape=jax.ShapeDtypeStruct((n_rows, D_MODEL), F32),
        compiler_params=_cparams(("arbitrary",)),
        name="dispatch",
    )(route, starts, zero_rows, h2)


def _experts_kernel(tile_e_ref, tile_occ_ref, tile_row_ref,
                    xs_hbm, wg_ref, wu_ref, bg_ref, bu_ref, wd_ref, bd_ref, out_ref,
                    xbuf, xb_scr, act_scr, sem, *, n_tiles):
    del tile_e_ref
    i = pl.program_id(0)
    j = pl.program_id(1)
    occ = tile_occ_ref[i]

    def sub_rows(sub):
        return pl.ds(pl.multiple_of(sub * MOE_SUB, MOE_SUB), MOE_SUB)

    def sub_copy(tile, sub):
        row0 = pl.multiple_of(tile_row_ref[tile] + sub * MOE_SUB, MOE_SUB)
        return pltpu.make_async_copy(xs_hbm.at[pl.ds(row0, MOE_SUB)], xbuf.at[sub_rows(sub)], sem)

    def start_load(tile):
        def body(sub, carry):
            sub_copy(tile, sub).start()
            return carry
        lax.fori_loop(0, tile_occ_ref[tile], body, 0)

    @pl.when(j == 0)
    def _():
        @pl.when(i == 0)
        def _():
            start_load(0)

        def wait_sub(sub, carry):
            sub_copy(i, sub).wait()
            return carry

        def round_sub(sub, carry):
            xb_scr[sub_rows(sub), :] = xbuf[sub_rows(sub), :].astype(BF16)
            return carry

        lax.fori_loop(0, occ, wait_sub, 0)
        lax.fori_loop(0, occ, round_sub, 0)

        @pl.when(i + 1 < n_tiles)
        def _():
            start_load(i + 1)

    def for_each_sub(weights, fn):
        def pair(p, carry):
            w = weights()
            fn(2 * p, *w)
            fn(2 * p + 1, *w)
            return carry

        lax.fori_loop(0, occ // 2, pair, 0)

        @pl.when(occ % 2 == 1)
        def _():
            fn(occ - 1, *weights())

    @pl.when(j < MOE_NF)
    def _():
        def weights():
            return wg_ref[...].astype(BF16), wu_ref[...].astype(BF16)

        def gate_up(sub, wg, wu):
            rows = sub_rows(sub)
            xb = xb_scr[rows, :]
            gate = jnp.dot(xb, wg, preferred_element_type=F32) + bg_ref[...]
            up = jnp.dot(xb, wu, preferred_element_type=F32) + bu_ref[...]
            gate = jnp.minimum(gate, SWIGLU_LIMIT)
            up = jnp.clip(up, -SWIGLU_LIMIT, SWIGLU_LIMIT)
            act = (up + 1.0) * (gate * _sigmoid(SWIGLU_ALPHA * gate))
            act_scr[j, rows, :] = act.astype(BF16)

        for_each_sub(weights, gate_up)

    @pl.when(j >= MOE_NF)
    def _():
        def weights():
            return (wd_ref[...].astype(BF16),)

        def down(sub, wd):
            rows = sub_rows(sub)
            y = bd_ref[...] + jnp.dot(act_scr[0, rows, :], wd[0:MOE_TF, :],
                                      preferred_element_type=F32)
            for c in range(1, MOE_NF):
                y = y + jnp.dot(act_scr[c, rows, :], wd[c * MOE_TF:(c + 1) * MOE_TF, :],
                                preferred_element_type=F32)
            out_ref[rows, :] = y

        for_each_sub(weights, down)

    @pl.when(j >= MOE_NF)
    def _():
        def zero_sub(sub, carry):
            out_ref[sub_rows(sub), :] = jnp.zeros((MOE_SUB, MOE_TN), F32)
            return carry

        lax.fori_loop(occ, MOE_SUBS, zero_sub, 0)


def _experts(tile_e, tile_occ, tile_row, xs, wgu, bgu, wd, bd, n_tiles):
    up_off = D_FF // MOE_TF

    def fa(i, j, to):
        return jnp.where(to[i] > 0, jnp.minimum(j, MOE_NF - 1), MOE_NF - 1)

    def fb(i, j, to):
        return jnp.where(to[i] > 0, jnp.maximum(j - MOE_NF, 0), MOE_NN - 1)

    def gate_map(i, j, te, to, tr):
        return (te[i], 0, fa(i, j, to))

    def up_map(i, j, te, to, tr):
        return (te[i], 0, up_off + fa(i, j, to))

    def down_map(i, j, te, to, tr):
        return (te[i], 0, fb(i, j, to))

    def out_map(i, j, te, to, tr):
        return (i, jnp.maximum(j - MOE_NF, 0))

    grid_spec = pltpu.PrefetchScalarGridSpec(
        num_scalar_prefetch=3,
        grid=(n_tiles, MOE_NF + MOE_NN),
        in_specs=[
            pl.BlockSpec(memory_space=pl.ANY),
            pl.BlockSpec((None, D_MODEL, MOE_TF), gate_map),
            pl.BlockSpec((None, D_MODEL, MOE_TF), up_map),
            pl.BlockSpec((None, 1, MOE_TF), gate_map),
            pl.BlockSpec((None, 1, MOE_TF), up_map),
            pl.BlockSpec((None, D_FF, MOE_TN), down_map),
            pl.BlockSpec((None, 1, MOE_TN), down_map),
        ],
        out_specs=pl.BlockSpec((MOE_CAP, MOE_TN), out_map),
        scratch_shapes=[
            pltpu.VMEM((MOE_CAP, D_MODEL), F32),
            pltpu.VMEM((MOE_CAP, D_MODEL), BF16),
            pltpu.VMEM((MOE_NF, MOE_CAP, MOE_TF), BF16),
            pltpu.SemaphoreType.DMA,
        ],
    )
    return pl.pallas_call(
        functools.partial(_experts_kernel, n_tiles=n_tiles),
        grid_spec=grid_spec,
        out_shape=jax.ShapeDtypeStruct((n_tiles * MOE_CAP, D_MODEL), F32),
        compiler_params=_cparams(("arbitrary", "arbitrary")),
        name="experts",
    )(tile_e, tile_occ, tile_row, xs, wgu, wgu, bgu, bgu, wd, bd)


def _combine_kernel(route_ref, start_ref, ys_hbm, x1_ref, gate_ref, gf_ref, out_ref, buf, sem, *,
                    n_tiles):
    tm = x1_ref.shape[0]
    i = pl.program_id(0)
    slot = i % 2

    def row_copy(tile, s, r, k):
        d = _grouped_row(route_ref, start_ref, tile * tm + r, k)
        return pltpu.make_async_copy(ys_hbm.at[pl.ds(d, 1)], buf.at[s, k, pl.ds(r, 1)], sem.at[s])

    def start_gather(tile, s):
        def per_row(r, carry):
            for k in range(TOP_K):
                row_copy(tile, s, r, k).start()
            return carry
        lax.fori_loop(0, tm, per_row, 0)

    @pl.when(i == 0)
    def _():
        start_gather(0, 0)

    for k in range(TOP_K):
        pltpu.make_async_copy(ys_hbm.at[pl.ds(0, tm)], buf.at[slot, k], sem.at[slot]).wait()

    @pl.when(i + 1 < n_tiles)
    def _():
        start_gather(i + 1, 1 - slot)

    x2 = x1_ref[...]
    for k in range(TOP_K):
        x2 = x2 + gate_ref[:, k:k + 1] * buf[slot, k]
    out_ref[...] = x2 * lax.rsqrt(jnp.mean(x2 * x2, axis=-1, keepdims=True) + EPS) * gf_ref[...]


def _combine(route, starts, ys, x1, gate, gf):
    t = x1.shape[0]
    tm = COMBINE_TM
    n_tiles = t // tm
    grid_spec = pltpu.PrefetchScalarGridSpec(
        num_scalar_prefetch=2,
        grid=(n_tiles,),
        in_specs=[
            pl.BlockSpec(memory_space=pl.ANY),
            pl.BlockSpec((tm, D_MODEL), lambda i, r, s: (i, 0)),
            pl.BlockSpec((tm, LANES), lambda i, r, s: (i, 0)),
            pl.BlockSpec((1, D_MODEL), lambda i, r, s: (0, 0)),
        ],
        out_specs=pl.BlockSpec((tm, D_MODEL), lambda i, r, s: (i, 0)),
        scratch_shapes=[
            pltpu.VMEM((2, TOP_K, tm, D_MODEL), F32),
            pltpu.SemaphoreType.DMA((2,)),
        ],
    )
    return pl.pallas_call(
        functools.partial(_combine_kernel, n_tiles=n_tiles),
        grid_spec=grid_spec,
        out_shape=jax.ShapeDtypeStruct((t, D_MODEL), F32),
        compiler_params=_cparams(("arbitrary",)),
        name="combine",
    )(route, starts, ys, x1, gate, gf)


def _routing_tables(counts, n_tokens):
    n_tiles = N_EXPERTS + (n_tokens * TOP_K) // MOE_CAP
    n_subs = N_EXPERTS + (n_tokens * TOP_K) // MOE_SUB
    subs_per = (counts + MOE_SUB - 1) // MOE_SUB
    sub_end = jnp.cumsum(subs_per)
    x_start = (sub_end - subs_per) * MOE_SUB
    tiles_per = (counts + MOE_CAP - 1) // MOE_CAP
    tile_end = jnp.cumsum(tiles_per)
    tile_start = tile_end - tiles_per
    y_start = tile_start * MOE_CAP

    tile_ids = jnp.arange(n_tiles, dtype=I32)
    n_active = tile_end[-1]
    last = jnp.maximum(n_active - 1, 0)
    active = tile_ids < n_active
    tile_eff = jnp.minimum(tile_ids, last)
    tile_e = jnp.sum((tile_eff[:, None] >= tile_end[None, :]).astype(I32), axis=1)
    tile_e = jnp.minimum(tile_e, N_EXPERTS - 1)
    in_expert = tile_eff - tile_start[tile_e]
    rows_in = counts[tile_e] - in_expert * MOE_CAP
    occ = jnp.clip((rows_in + MOE_SUB - 1) // MOE_SUB, 0, MOE_SUBS)
    tile_occ = jnp.where(active, occ, 0).astype(I32)
    tile_row = (x_start[tile_e] + in_expert * MOE_CAP).astype(I32)

    last_sub = jnp.where(counts > 0, x_start + (subs_per - 1) * MOE_SUB, -1)
    tail_ids = sub_end[-1] + jnp.arange(N_EXPERTS, dtype=I32)
    tail = jnp.where(tail_ids < n_subs, tail_ids * MOE_SUB, -1)
    zero_rows = jnp.concatenate([last_sub, tail]).astype(I32)
    return (n_tiles, n_subs * MOE_SUB, tile_e.astype(I32), tile_occ, tile_row,
            x_start, y_start, zero_rows)


def _layer(x2d, bsz, seq, g_norm_mix, w_in, b_in, w_conv, b_conv, w_sgu, b_sgu, g_sgu_ln, b_sgu_ln,
           g_mlstm_norm, w_proj_sgu, w_proj_mlstm, w_out, g_norm_moe, w_router, b_router,
           w_gate_up, b_gate_up, w_down, b_down, g_out):
    t = x2d.shape[0]

    pad = LANES - 2 * HEADS
    w_if = jnp.pad(w_in[:, N_MAIN:GATE_OFF], ((0, 0), (0, pad))).astype(BF16)
    b_if = jnp.pad(b_in[N_MAIN:GATE_OFF], (0, pad))[None, :]
    w_in_t = w_in.T
    w_mg_t = w_in_t[GATE_OFF:]
    bsb = jnp.broadcast_to(b_sgu.T[:, :, None], (SGU_BLOCK, SGU_GROUPS, SGU_GROUP_DIM))
    bsb = bsb.reshape(SGU_BLOCK, D_MODEL)
    w_r = jnp.pad(w_router, ((0, 0), (0, LANES - N_EXPERTS)))
    w_r_hi = w_r.astype(BF16)
    w_r_lo = (w_r - w_r_hi.astype(F32)).astype(BF16)
    b_r = jnp.pad(b_router, (0, LANES - N_EXPERTS))[None, :]

    h, gates = _norm_gates(x2d, g_norm_mix[None, :], w_if, b_if)
    zm = _in_proj(h, w_in_t, b_in[None, :], N_MAIN, 2 * D_MODEL, 3 * D_MODEL, "in_proj_main")
    zg = _in_proj(h, w_mg_t, b_in[None, GATE_OFF:], 2 * D_MODEL, 0, 0, "in_proj_gate")
    a = _sgu(zm, zg, w_sgu, bsb, g_sgu_ln[None, :], b_sgu_ln[None, :], w_proj_sgu.astype(BF16))
    ym = _mlstm(zm, gates, w_conv, b_conv[None, :], g_mlstm_norm[None, :], bsz, seq)
    x1, h2, logits = _merge(ym, a, zg, x2d, w_proj_mlstm.astype(BF16), w_out.astype(BF16),
                            g_norm_moe[None, :], w_r_hi, w_r_lo, b_r)
    info, gate, cnt = _route(logits)

    counts = cnt[0, :N_EXPERTS].astype(I32)
    (n_tiles, n_xrows, tile_e, tile_occ, tile_row, x_start, y_start,
     zero_rows) = _routing_tables(counts, t)
    route = info[:, :2 * TOP_K].reshape(-1)

    xs = _dispatch(route, x_start, zero_rows, h2, n_xrows)
    ys = _experts(tile_e, tile_occ, tile_row, xs, w_gate_up, b_gate_up[:, None, :],
                  w_down, b_down[:, None, :], n_tiles)
    return _combine(route, y_start, ys, x1, gate, g_out[None, :])


def kernel(x, g_norm_mix, w_in, b_in, w_conv, b_conv, w_sgu, b_sgu, g_sgu_ln, b_sgu_ln, g_mlstm_norm,
           w_proj_sgu, w_proj_mlstm, w_out, g_norm_moe, w_router, b_router, w_gate_up, b_gate_up,
           w_down, b_down, g_final):
    bsz, seq, d = x.shape
    assert d == D_MODEL and w_in.shape[0] == 1, "single-layer block with d_model 2048"
    out = _layer(x.reshape(bsz * seq, d), bsz, seq, g_norm_mix[0], w_in[0], b_in[0], w_conv[0],
                 b_conv[0], w_sgu[0], b_sgu[0], g_sgu_ln[0], b_sgu_ln[0], g_mlstm_norm[0],
                 w_proj_sgu[0], w_proj_mlstm[0], w_out[0], g_norm_moe[0], w_router[0], b_router[0],
                 w_gate_up[0], b_gate_up[0], w_down[0], b_down[0], g_final)
    return out.reshape(bsz, seq, d)
```

```python
import functools

import jax
import jax.numpy as jnp
from jax import lax
from jax.experimental import pallas as pl
from jax.experimental.pallas import tpu as pltpu

F32 = jnp.float32
BF16 = jnp.bfloat16
I32 = jnp.int32

D_MODEL = 2048
CHUNK = 64
SGU_BLOCK = 128
SGU_GROUPS = 8
SGU_GROUP_DIM = 256
HEADS = 4
HEAD_DIM = 512
CONV_WIDTH = 4
N_EXPERTS = 32
TOP_K = 4
D_FF = 2048
SWIGLU_LIMIT = 7.0
SWIGLU_ALPHA = 1.702
EPS = 1e-6

LANES = 128
SUBLANES = 8
VMEM_LIMIT = 56 * 1024 * 1024

ZB_U, ZB_V, ZB_Q, ZB_K, ZB_VM, ZB_O = range(6)
ZG_A, ZG_M = range(2)
N_MAIN = 6 * D_MODEL
GATE_OFF = N_MAIN + 2 * HEADS

NORM_TM = 512
INPROJ_TM = 1024
INPROJ_TN = 1024
INPROJ_RC = 512
SGU_TM = 256
MLSTM_L = 256
MERGE_TM = 256
MERGE_RC = 256
ROUTE_TM = 256
DISPATCH_TM = 256
MOE_SUB = 256
MOE_CAP = 1280
MOE_SUBS = MOE_CAP // MOE_SUB
MOE_TF = 512
MOE_NF = D_FF // MOE_TF
MOE_TN = 512
MOE_NN = D_MODEL // MOE_TN
COMBINE_TM = 128


def _cparams(sem, vmem_limit=VMEM_LIMIT):
    return pltpu.CompilerParams(dimension_semantics=sem, vmem_limit_bytes=vmem_limit)


def _gelu(x):
    return 0.5 * x * (1.0 + lax.erf(x * (2.0 ** -0.5)))


def _sigmoid(x):
    return 1.0 / (1.0 + jnp.exp(-x))


def _norm_gates_kernel(x_ref, g_ref, wif_ref, bif_ref, h_ref, gates_ref):
    x = x_ref[...]
    ms = jnp.mean(x * x, axis=-1, keepdims=True)
    h = (x * lax.rsqrt(ms + EPS) * g_ref[...]).astype(BF16)
    h_ref[...] = h
    gates_ref[...] = jnp.dot(h, wif_ref[...], preferred_element_type=F32) + bif_ref[...]


def _norm_gates(x, g, wif, bif):
    t = x.shape[0]
    tm = min(NORM_TM, t)
    row = lambda i: (i, 0)
    const = lambda i: (0, 0)
    return pl.pallas_call(
        _norm_gates_kernel,
        grid=(t // tm,),
        in_specs=[
            pl.BlockSpec((tm, D_MODEL), row),
            pl.BlockSpec((1, D_MODEL), const),
            pl.BlockSpec((D_MODEL, LANES), const),
            pl.BlockSpec((1, LANES), const),
        ],
        out_specs=[pl.BlockSpec((tm, D_MODEL), row), pl.BlockSpec((tm, LANES), row)],
        out_shape=[jax.ShapeDtypeStruct((t, D_MODEL), BF16), jax.ShapeDtypeStruct((t, LANES), F32)],
        compiler_params=_cparams(("arbitrary",)),
        name="norm_gates",
    )(x, g, wif, bif)


def _inproj_kernel(h_ref, w_ref, b_ref, z_ref, w_scr, *, n_gelu, n_plain):
    j = pl.program_id(0)

    @pl.when(pl.program_id(1) == 0)
    def _():
        w_scr[...] = w_ref[...].astype(BF16)

    def project(activation):
        tm = h_ref.shape[0]
        rc = min(INPROJ_RC, tm)
        for c in range(tm // rc):
            rows = slice(c * rc, (c + 1) * rc)
            acc = lax.dot_general(h_ref[rows, :], w_scr[...], (((1,), (1,)), ((), ())),
                                  preferred_element_type=F32) + b_ref[...]
            z_ref[rows, :] = activation(acc).astype(z_ref.dtype)

    @pl.when(j < n_gelu)
    def _():
        project(_gelu)

    @pl.when((j >= n_gelu) & (j < n_gelu + n_plain))
    def _():
        project(lambda acc: acc)

    @pl.when(j >= n_gelu + n_plain)
    def _():
        project(_sigmoid)


def _in_proj(h, w_t, b, n_cols, n_gelu_cols, n_plain_cols, name):
    t = h.shape[0]
    tm, tn = min(INPROJ_TM, t), INPROJ_TN
    return pl.pallas_call(
        functools.partial(_inproj_kernel, n_gelu=n_gelu_cols // tn, n_plain=n_plain_cols // tn),
        grid=(n_cols // tn, t // tm),
        in_specs=[
            pl.BlockSpec((tm, D_MODEL), lambda j, i: (i, 0)),
            pl.BlockSpec((tn, D_MODEL), lambda j, i: (j, 0)),
            pl.BlockSpec((1, tn), lambda j, i: (0, j)),
        ],
        out_specs=pl.BlockSpec((tm, tn), lambda j, i: (i, j)),
        out_shape=jax.ShapeDtypeStruct((t, n_cols), BF16),
        scratch_shapes=[pltpu.VMEM((tn, D_MODEL), BF16)],
        compiler_params=_cparams(("arbitrary", "arbitrary")),
        name=name,
    )(h, w_t, b)


def _sgu_kernel(gu_ref, gv_ref, sga_ref, ws_ref, bsb_ref, lng_ref, lnb_ref, pa_ref, out_ref,
                ya_scr):
    tm = gu_ref.shape[0]
    v = gv_ref[...].astype(F32)
    mu = jnp.mean(v, axis=-1, keepdims=True)
    vc = v - mu
    var = jnp.mean(vc * vc, axis=-1, keepdims=True)
    vln = (vc * lax.rsqrt(var + EPS) * lng_ref[...] + lnb_ref[...]).astype(BF16)

    t_id = lax.broadcasted_iota(I32, (SGU_BLOCK, SGU_BLOCK), 0) // CHUNK
    s_id = lax.broadcasted_iota(I32, (SGU_BLOCK, SGU_BLOCK), 1) // CHUNK
    causal = s_id <= t_id
    for g in range(SGU_GROUPS):
        w = jnp.where(causal, ws_ref[g], 0.0).astype(BF16)
        cols = slice(g * SGU_GROUP_DIM, (g + 1) * SGU_GROUP_DIM)
        bias = bsb_ref[:, cols]
        for blk in range(tm // SGU_BLOCK):
            rows = slice(blk * SGU_BLOCK, (blk + 1) * SGU_BLOCK)
            mixed = jnp.dot(w, vln[rows, cols], preferred_element_type=F32) + bias
            ya_scr[rows, cols] = (gu_ref[rows, cols].astype(F32) * mixed).astype(BF16)

    proj = jnp.dot(ya_scr[...], pa_ref[...], preferred_element_type=F32)
    out_ref[...] = (sga_ref[...].astype(F32) * proj).astype(out_ref.dtype)


def _sgu(zm, zg, ws, bsb, lng, lnb, pa):
    t = zm.shape[0]
    tm = SGU_TM
    const2 = lambda i: (0, 0)
    return pl.pallas_call(
        _sgu_kernel,
        grid=(t // tm,),
        in_specs=[
            pl.BlockSpec((tm, D_MODEL), lambda i: (i, ZB_U)),
            pl.BlockSpec((tm, D_MODEL), lambda i: (i, ZB_V)),
            pl.BlockSpec((tm, D_MODEL), lambda i: (i, ZG_A)),
            pl.BlockSpec((SGU_GROUPS, SGU_BLOCK, SGU_BLOCK), lambda i: (0, 0, 0)),
            pl.BlockSpec((SGU_BLOCK, D_MODEL), const2),
            pl.BlockSpec((1, D_MODEL), const2),
            pl.BlockSpec((1, D_MODEL), const2),
            pl.BlockSpec((D_MODEL, D_MODEL), const2),
        ],
        out_specs=pl.BlockSpec((tm, D_MODEL), lambda i: (i, 0)),
        out_shape=jax.ShapeDtypeStruct((t, D_MODEL), BF16),
        scratch_shapes=[pltpu.VMEM((tm, D_MODEL), BF16)],
        compiler_params=_cparams(("arbitrary",)),
        name="sgu",
    )(zm, zm, zg, ws, bsb, lng, lnb, pa)


def _mlstm_kernel(q_ref, k_ref, v_ref, so_ref, gates_ref, wc_ref, bc_ref, gn_ref, out_ref,
                  hist, ct_scr, n_scr, m_scr):
    L = q_ref.shape[0]
    c = pl.program_id(1)

    @pl.when(c == 0)
    def _():
        hist[0:SUBLANES, :] = jnp.zeros((SUBLANES, 2 * D_MODEL), F32)
        ct_scr[...] = jnp.zeros_like(ct_scr)
        n_scr[...] = jnp.zeros_like(n_scr)
        m_scr[...] = jnp.zeros_like(m_scr)

    hist[SUBLANES:SUBLANES + L, 0:D_MODEL] = q_ref[...].astype(F32)
    hist[SUBLANES:SUBLANES + L, D_MODEL:2 * D_MODEL] = k_ref[...].astype(F32)
    qk = bc_ref[...] + wc_ref[CONV_WIDTH - 1:CONV_WIDTH, :] * hist[SUBLANES:SUBLANES + L, :]
    for j in range(CONV_WIDTH - 1):
        off = SUBLANES - (CONV_WIDTH - 1) + j
        qk = qk + wc_ref[j:j + 1, :] * hist[off:off + L, :]
    qk = qk * _sigmoid(qk)
    hist[0:SUBLANES, :] = hist[L:L + SUBLANES, :]

    gts = gates_ref[...]
    lf = jnp.minimum(gts, 0.0) - jnp.log(1.0 + jnp.exp(-jnp.abs(gts)))
    row = lax.broadcasted_iota(I32, (L, L), 0)
    col = lax.broadcasted_iota(I32, (L, L), 1)
    tril = col <= row
    tri = jnp.where(tril, 1.0, 0.0).astype(BF16)
    lf_hi = lf.astype(BF16)
    lf_lo = (lf - lf_hi.astype(F32)).astype(BF16)
    bcum = (jnp.dot(tri, lf_hi, preferred_element_type=F32)
            + jnp.dot(tri, lf_lo, preferred_element_type=F32))
    bcum_t = bcum.T
    gts_t = gts.T

    scale = HEAD_DIM ** -0.5
    for h in range(HEADS):
        cols = slice(h * HEAD_DIM, (h + 1) * HEAD_DIM)
        q = qk[:, h * HEAD_DIM:(h + 1) * HEAD_DIM] * scale
        k = qk[:, D_MODEL + h * HEAD_DIM:D_MODEL + (h + 1) * HEAD_DIM]
        qb = q.astype(BF16)
        kb = k.astype(BF16)
        vb = v_ref[:, cols]
        bc = bcum[:, HEADS + h:HEADS + h + 1]
        bc_r = bcum_t[HEADS + h:HEADS + h + 1, :]
        ig_c = gts[:, h:h + 1]
        ig_r = gts_t[h:h + 1, :]
        m_st = m_scr[h]
        n_st = n_scr[h]

        dlog = jnp.where(tril, bc - bc_r + ig_r, -jnp.inf)
        m_inter = bc + m_st
        m_t = jnp.maximum(m_inter, jnp.max(dlog, axis=-1, keepdims=True))
        p = jnp.exp(dlog - m_t)
        s = lax.dot_general(qb, kb, (((1,), (1,)), ((), ())), preferred_element_type=F32)
        sc = s * p
        inter = jnp.exp(m_inter - m_t)
        ctb = ct_scr[h].astype(BF16)
        num = (jnp.dot(sc.astype(BF16), vb, preferred_element_type=F32)
               + inter * jnp.dot(qb, ctb, preferred_element_type=F32))
        den = (jnp.sum(sc, axis=-1, keepdims=True)
               + inter * jnp.sum(q * n_st, axis=-1, keepdims=True))
        hh = num / jnp.maximum(jnp.abs(den), jnp.exp(-m_t))
        hn = hh * lax.rsqrt(jnp.mean(hh * hh, axis=-1, keepdims=True) + EPS) * gn_ref[:, cols]
        out_ref[:, cols] = (hn * so_ref[:, cols].astype(F32)).astype(out_ref.dtype)

        g_tot = bc[L - 1:L, :]
        a = g_tot - bc + ig_c
        m_new = jnp.maximum(g_tot + m_st, jnp.max(a, axis=0, keepdims=True))
        wa = jnp.exp(a - m_new)
        decay = jnp.exp(g_tot + m_st - m_new)
        wv = (wa * vb.astype(F32)).astype(BF16)
        upd = jnp.dot(k.T.astype(BF16), wv, preferred_element_type=F32)
        ct_scr[h] = decay * ct_scr[h] + upd
        n_scr[h] = decay * n_st + jnp.sum(wa * k, axis=0, keepdims=True)
        m_scr[h] = m_new


def _mlstm(zm, gates, wc, bc, gn, bsz, seq):
    t = zm.shape[0]
    L = MLSTM_L
    nc = seq // L
    zspec = lambda blk: pl.BlockSpec((L, D_MODEL), lambda b, c: (b * nc + c, blk))
    const2 = lambda b, c: (0, 0)
    return pl.pallas_call(
        _mlstm_kernel,
        grid=(bsz, nc),
        in_specs=[
            zspec(ZB_Q), zspec(ZB_K), zspec(ZB_VM), zspec(ZB_O),
            pl.BlockSpec((L, LANES), lambda b, c: (b * nc + c, 0)),
            pl.BlockSpec((CONV_WIDTH, 2 * D_MODEL), const2),
            pl.BlockSpec((1, 2 * D_MODEL), const2),
            pl.BlockSpec((1, D_MODEL), const2),
        ],
        out_specs=pl.BlockSpec((L, D_MODEL), lambda b, c: (b * nc + c, 0)),
        out_shape=jax.ShapeDtypeStruct((t, D_MODEL), BF16),
        scratch_shapes=[
            pltpu.VMEM((L + 2 * SUBLANES, 2 * D_MODEL), F32),
            pltpu.VMEM((HEADS, HEAD_DIM, HEAD_DIM), F32),
            pltpu.VMEM((HEADS, 1, HEAD_DIM), F32),
            pltpu.VMEM((HEADS, 1, 1), F32),
        ],
        compiler_params=_cparams(("arbitrary", "arbitrary")),
        name="mlstm",
    )(zm, zm, zm, zm, gates, wc, bc, gn)


def _merge_kernel(ym_ref, a_ref, sgm_ref, x_ref, pm_ref, wo_ref, g2_ref, wrh_ref, wrl_ref, br_ref,
                  x1_ref, h2_ref, lg_ref):
    tm = x_ref.shape[0]
    rc = min(MERGE_RC, tm)
    for c in range(tm // rc):
        rows = slice(c * rc, (c + 1) * rc)
        proj = jnp.dot(ym_ref[rows, :], pm_ref[...], preferred_element_type=F32)
        merged = a_ref[rows, :].astype(F32) + sgm_ref[rows, :].astype(F32) * proj
        x1 = x_ref[rows, :] + jnp.dot(merged.astype(BF16), wo_ref[...],
                                      preferred_element_type=F32)
        x1_ref[rows, :] = x1
        h2 = x1 * lax.rsqrt(jnp.mean(x1 * x1, axis=-1, keepdims=True) + EPS) * g2_ref[...]
        h2_ref[rows, :] = h2
        h_hi = h2.astype(BF16)
        h_lo = (h2 - h_hi.astype(F32)).astype(BF16)
        lg_ref[rows, :] = (jnp.dot(h_hi, wrh_ref[...], preferred_element_type=F32)
                           + (jnp.dot(h_lo, wrh_ref[...], preferred_element_type=F32)
                              + jnp.dot(h_hi, wrl_ref[...], preferred_element_type=F32))
                           + br_ref[...])


def _merge(ym, a, zg, x, pm, wo, g2, wr_hi, wr_lo, br):
    t = x.shape[0]
    tm = MERGE_TM
    const2 = lambda i: (0, 0)
    row = lambda i: (i, 0)
    return pl.pallas_call(
        _merge_kernel,
        grid=(t // tm,),
        in_specs=[
            pl.BlockSpec((tm, D_MODEL), row),
            pl.BlockSpec((tm, D_MODEL), row),
            pl.BlockSpec((tm, D_MODEL), lambda i: (i, ZG_M)),
            pl.BlockSpec((tm, D_MODEL), row),
            pl.BlockSpec((D_MODEL, D_MODEL), const2),
            pl.BlockSpec((D_MODEL, D_MODEL), const2),
            pl.BlockSpec((1, D_MODEL), const2),
            pl.BlockSpec((D_MODEL, LANES), const2),
            pl.BlockSpec((D_MODEL, LANES), const2),
            pl.BlockSpec((1, LANES), const2),
        ],
        out_specs=[
            pl.BlockSpec((tm, D_MODEL), row),
            pl.BlockSpec((tm, D_MODEL), row),
            pl.BlockSpec((tm, LANES), row),
        ],
        out_shape=[
            jax.ShapeDtypeStruct((t, D_MODEL), F32),
            jax.ShapeDtypeStruct((t, D_MODEL), F32),
            jax.ShapeDtypeStruct((t, LANES), F32),
        ],
        compiler_params=_cparams(("arbitrary",)),
        name="merge",
    )(ym, a, zg, x, pm, wo, g2, wr_hi, wr_lo, br)


def _route_kernel(lg_ref, info_ref, gate_ref, cnt_ref, cnt_scr):
    tm = lg_ref.shape[0]
    i = pl.program_id(0)

    @pl.when(i == 0)
    def _():
        cnt_scr[...] = jnp.zeros_like(cnt_scr)

    lane = lax.broadcasted_iota(I32, (tm, LANES), 1)
    lane_f = lane.astype(F32)
    vals = jnp.where(lane < N_EXPERTS, lg_ref[...], -jnp.inf)
    tops, ids, hots = [], [], []
    for _ in range(TOP_K):
        m = jnp.max(vals, axis=-1, keepdims=True)
        idx_f = jnp.min(jnp.where(vals == m, lane_f, float(LANES)), axis=-1, keepdims=True)
        idx = idx_f.astype(I32)
        hot = lane == idx
        tops.append(m)
        ids.append(idx)
        hots.append(hot)
        vals = jnp.where(hot, -jnp.inf, vals)

    exps = [jnp.exp(tv - tops[0]) for tv in tops]
    denom = exps[0] + exps[1] + exps[2] + exps[3]

    sel = jnp.zeros((tm, LANES), F32)
    for hot in hots:
        sel = sel + jnp.where(hot, 1.0, 0.0)
    r = lax.broadcasted_iota(I32, (tm, tm), 0)
    c = lax.broadcasted_iota(I32, (tm, tm), 1)
    strict = jnp.where(c < r, 1.0, 0.0).astype(BF16)
    ahead = jnp.dot(strict, sel.astype(BF16), preferred_element_type=F32) + cnt_scr[...]

    info = jnp.zeros((tm, LANES), I32)
    gate = jnp.zeros((tm, LANES), F32)
    for k in range(TOP_K):
        pos = jnp.sum(jnp.where(hots[k], ahead, 0.0), axis=-1, keepdims=True).astype(I32)
        info = jnp.where(lane == k, ids[k], info)
        info = jnp.where(lane == TOP_K + k, pos, info)
        gate = jnp.where(lane == k, exps[k] / denom, gate)
    info_ref[...] = info
    gate_ref[...] = gate
    cnt_scr[...] = cnt_scr[...] + jnp.sum(sel, axis=0, keepdims=True)
    cnt_ref[...] = jnp.broadcast_to(cnt_scr[...], cnt_ref.shape)


def _route(logits):
    t = logits.shape[0]
    tm = ROUTE_TM
    row = lambda i: (i, 0)
    return pl.pallas_call(
        _route_kernel,
        grid=(t // tm,),
        in_specs=[pl.BlockSpec((tm, LANES), row)],
        out_specs=[
            pl.BlockSpec((tm, LANES), row),
            pl.BlockSpec((tm, LANES), row),
            pl.BlockSpec((SUBLANES, LANES), lambda i: (0, 0)),
        ],
        out_shape=[
            jax.ShapeDtypeStruct((t, LANES), I32),
            jax.ShapeDtypeStruct((t, LANES), F32),
            jax.ShapeDtypeStruct((SUBLANES, LANES), F32),
        ],
        scratch_shapes=[pltpu.VMEM((1, LANES), F32)],
        compiler_params=_cparams(("arbitrary",)),
        name="route",
    )(logits)


def _grouped_row(route_ref, start_ref, token, k):
    base = token * (2 * TOP_K)
    return start_ref[route_ref[base + k]] + route_ref[base + TOP_K + k]


def _dispatch_kernel(route_ref, start_ref, zero_row_ref, h2_ref, xs_hbm, zbuf, sem, zsem, *,
                     n_zero):
    tm = h2_ref.shape[0]
    i = pl.program_id(0)

    def zero_copy(j):
        row0 = pl.multiple_of(zero_row_ref[j], MOE_SUB)
        return pltpu.make_async_copy(zbuf, xs_hbm.at[pl.ds(row0, MOE_SUB)], zsem)

    @pl.when(i == 0)
    def _():
        zbuf[...] = jnp.zeros_like(zbuf)

        def start(j, carry):
            @pl.when(zero_row_ref[j] >= 0)
            def _():
                zero_copy(j).start()
            return carry

        def wait(j, carry):
            @pl.when(zero_row_ref[j] >= 0)
            def _():
                zero_copy(j).wait()
            return carry

        lax.fori_loop(0, n_zero, start, 0)
        lax.fori_loop(0, n_zero, wait, 0)

    for r in range(tm):
        for k in range(TOP_K):
            d = _grouped_row(route_ref, start_ref, i * tm + r, k)
            pltpu.make_async_copy(h2_ref.at[pl.ds(r, 1)], xs_hbm.at[pl.ds(d, 1)], sem).start()

    for _ in range(TOP_K):
        pltpu.make_async_copy(h2_ref, xs_hbm.at[pl.ds(0, tm)], sem).wait()


def _dispatch(route, starts, zero_rows, h2, n_rows):
    t = h2.shape[0]
    tm = DISPATCH_TM
    grid_spec = pltpu.PrefetchScalarGridSpec(
        num_scalar_prefetch=3,
        grid=(t // tm,),
        in_specs=[pl.BlockSpec((tm, D_MODEL), lambda i, r, s, z: (i, 0))],
        out_specs=pl.BlockSpec(memory_space=pl.ANY),
        scratch_shapes=[
            pltpu.VMEM((MOE_SUB, D_MODEL), F32),
            pltpu.SemaphoreType.DMA,
            pltpu.SemaphoreType.DMA,
        ],
    )
    return pl.pallas_call(
        functools.partial(_dispatch_kernel, n_zero=zero_rows.shape[0]),
        grid_spec=grid_spec,
        out_shape=jax.ShapeDtypeStruct((n_rows, D_MODEL), F32),
        compiler_params=_cparams(("arbitrary",)),
        name="dispatch",
    )(route, starts, zero_rows, h2)


def _experts_kernel(tile_e_ref, tile_occ_ref, tile_row_ref,
                    xs_hbm, wg_ref, wu_ref, bg_ref, bu_ref, wd_ref, bd_ref, out_ref,
                    xbuf, xb_scr, act_scr, sem, *, n_tiles):
    del tile_e_ref
    i = pl.program_id(0)
    j = pl.program_id(1)
    occ = tile_occ_ref[i]

    def sub_rows(sub):
        return pl.ds(pl.multiple_of(sub * MOE_SUB, MOE_SUB), MOE_SUB)

    def sub_copy(tile, sub):
        row0 = pl.multiple_of(tile_row_ref[tile] + sub * MOE_SUB, MOE_SUB)
        return pltpu.make_async_copy(xs_hbm.at[pl.ds(row0, MOE_SUB)], xbuf.at[sub_rows(sub)], sem)

    def start_load(tile):
        def body(sub, carry):
            sub_copy(tile, sub).start()
            return carry
        lax.fori_loop(0, tile_occ_ref[tile], body, 0)

    @pl.when(j == 0)
    def _():
        @pl.when(i == 0)
        def _():
            start_load(0)

        def wait_sub(sub, carry):
            sub_copy(i, sub).wait()
            return carry

        def round_sub(sub, carry):
            xb_scr[sub_rows(sub), :] = xbuf[sub_rows(sub), :].astype(BF16)
            return carry

        lax.fori_loop(0, occ, wait_sub, 0)
        lax.fori_loop(0, occ, round_sub, 0)

        @pl.when(i + 1 < n_tiles)
        def _():
            start_load(i + 1)

    def for_each_sub(weights, fn):
        def pair(p, carry):
            w = weights()
            fn(2 * p, *w)
            fn(2 * p + 1, *w)
            return carry

        lax.fori_loop(0, occ // 2, pair, 0)

        @pl.when(occ % 2 == 1)
        def _():
            fn(occ - 1, *weights())

    @pl.when(j < MOE_NF)
    def _():
        def weights():
            return wg_ref[...].astype(BF16), wu_ref[...].astype(BF16)

        def gate_up(sub, wg, wu):
            rows = sub_rows(sub)
            xb = xb_scr[rows, :]
            gate = jnp.dot(xb, wg, preferred_element_type=F32) + bg_ref[...]
            up = jnp.dot(xb, wu, preferred_element_type=F32) + bu_ref[...]
            gate = jnp.minimum(gate, SWIGLU_LIMIT)
            up = jnp.clip(up, -SWIGLU_LIMIT, SWIGLU_LIMIT)
            act = (up + 1.0) * (gate * _sigmoid(SWIGLU_ALPHA * gate))
            act_scr[j, rows, :] = act.astype(BF16)

        for_each_sub(weights, gate_up)

    @pl.when(j >= MOE_NF)
    def _():
        def weights():
            return (wd_ref[...].astype(BF16),)

        def down(sub, wd):
            rows = sub_rows(sub)
            y = bd_ref[...] + jnp.dot(act_scr[0, rows, :], wd[0:MOE_TF, :],
                                      preferred_element_type=F32)
            for c in range(1, MOE_NF):
                y = y + jnp.dot(act_scr[c, rows, :], wd[c * MOE_TF:(c + 1) * MOE_TF, :],
                                preferred_element_type=F32)
            out_ref[rows, :] = y

        for_each_sub(weights, down)

    @pl.when(j >= MOE_NF)
    def _():
        def zero_sub(sub, carry):
            out_ref[sub_rows(sub), :] = jnp.zeros((MOE_SUB, MOE_TN), F32)
            return carry

        lax.fori_loop(occ, MOE_SUBS, zero_sub, 0)


def _experts(tile_e, tile_occ, tile_row, xs, wgu, bgu, wd, bd, n_tiles):
    up_off = D_FF // MOE_TF

    def fa(i, j, to):
        return jnp.where(to[i] > 0, jnp.minimum(j, MOE_NF - 1), MOE_NF - 1)

    def fb(i, j, to):
        return jnp.where(to[i] > 0, jnp.maximum(j - MOE_NF, 0), MOE_NN - 1)

    def gate_map(i, j, te, to, tr):
        return (te[i], 0, fa(i, j, to))

    def up_map(i, j, te, to, tr):
        return (te[i], 0, up_off + fa(i, j, to))

    def down_map(i, j, te, to, tr):
        return (te[i], 0, fb(i, j, to))

    def out_map(i, j, te, to, tr):
        return (i, jnp.maximum(j - MOE_NF, 0))

    grid_spec = pltpu.PrefetchScalarGridSpec(
        num_scalar_prefetch=3,
        grid=(n_tiles, MOE_NF + MOE_NN),
        in_specs=[
            pl.BlockSpec(memory_space=pl.ANY),
            pl.BlockSpec((None, D_MODEL, MOE_TF), gate_map),
            pl.BlockSpec((None, D_MODEL, MOE_TF), up_map),
            pl.BlockSpec((None, 1, MOE_TF), gate_map),
            pl.BlockSpec((None, 1, MOE_TF), up_map),
            pl.BlockSpec((None, D_FF, MOE_TN), down_map),
            pl.BlockSpec((None, 1, MOE_TN), down_map),
        ],
        out_specs=pl.BlockSpec((MOE_CAP, MOE_TN), out_map),
        scratch_shapes=[
            pltpu.VMEM((MOE_CAP, D_MODEL), F32),
            pltpu.VMEM((MOE_CAP, D_MODEL), BF16),
            pltpu.VMEM((MOE_NF, MOE_CAP, MOE_TF), BF16),
            pltpu.SemaphoreType.DMA,
        ],
    )
    return pl.pallas_call(
        functools.partial(_experts_kernel, n_tiles=n_tiles),
        grid_spec=grid_spec,
        out_shape=jax.ShapeDtypeStruct((n_tiles * MOE_CAP, D_MODEL), F32),
        compiler_params=_cparams(("arbitrary", "arbitrary")),
        name="experts",
    )(tile_e, tile_occ, tile_row, xs, wgu, wgu, bgu, bgu, wd, bd)


def _combine_kernel(route_ref, start_ref, ys_hbm, x1_ref, gate_ref, gf_ref, out_ref, buf, sem, *,
                    n_tiles):
    tm = x1_ref.shape[0]
    i = pl.program_id(0)
    slot = i % 2

    def start_gather(tile, s):
        for r in range(tm):
            for k in range(TOP_K):
                d = _grouped_row(route_ref, start_ref, tile * tm + r, k)
                pltpu.make_async_copy(ys_hbm.at[pl.ds(d, 1)], buf.at[s, k, pl.ds(r, 1)],
                                      sem.at[s]).start()

    @pl.when(i == 0)
    def _():
        start_gather(0, 0)

    for k in range(TOP_K):
        pltpu.make_async_copy(ys_hbm.at[pl.ds(0, tm)], buf.at[slot, k], sem.at[slot]).wait()

    for s in range(2):
        @pl.when((i + 1 < n_tiles) & (slot == 1 - s))
        def _():
            start_gather(i + 1, s)

    x2 = x1_ref[...]
    for k in range(TOP_K):
        x2 = x2 + gate_ref[:, k:k + 1] * buf[slot, k]
    out_ref[...] = x2 * lax.rsqrt(jnp.mean(x2 * x2, axis=-1, keepdims=True) + EPS) * gf_ref[...]


def _combine(route, starts, ys, x1, gate, gf):
    t = x1.shape[0]
    tm = COMBINE_TM
    n_tiles = t // tm
    grid_spec = pltpu.PrefetchScalarGridSpec(
        num_scalar_prefetch=2,
        grid=(n_tiles,),
        in_specs=[
            pl.BlockSpec(memory_space=pl.ANY),
            pl.BlockSpec((tm, D_MODEL), lambda i, r, s: (i, 0)),
            pl.BlockSpec((tm, LANES), lambda i, r, s: (i, 0)),
            pl.BlockSpec((1, D_MODEL), lambda i, r, s: (0, 0)),
        ],
        out_specs=pl.BlockSpec((tm, D_MODEL), lambda i, r, s: (i, 0)),
        scratch_shapes=[
            pltpu.VMEM((2, TOP_K, tm, D_MODEL), F32),
            pltpu.SemaphoreType.DMA((2,)),
        ],
    )
    return pl.pallas_call(
        functools.partial(_combine_kernel, n_tiles=n_tiles),
        grid_spec=grid_spec,
        out_shape=jax.ShapeDtypeStruct((t, D_MODEL), F32),
        compiler_params=_cparams(("arbitrary",)),
        name="combine",
    )(route, starts, ys, x1, gate, gf)


def _routing_tables(counts, n_tokens):
    n_tiles = N_EXPERTS + (n_tokens * TOP_K) // MOE_CAP
    n_subs = N_EXPERTS + (n_tokens * TOP_K) // MOE_SUB
    subs_per = (counts + MOE_SUB - 1) // MOE_SUB
    sub_end = jnp.cumsum(subs_per)
    x_start = (sub_end - subs_per) * MOE_SUB
    tiles_per = (counts + MOE_CAP - 1) // MOE_CAP
    tile_end = jnp.cumsum(tiles_per)
    tile_start = tile_end - tiles_per
    y_start = tile_start * MOE_CAP

    tile_ids = jnp.arange(n_tiles, dtype=I32)
    n_active = tile_end[-1]
    last = jnp.maximum(n_active - 1, 0)
    active = tile_ids < n_active
    tile_eff = jnp.minimum(tile_ids, last)
    tile_e = jnp.sum((tile_eff[:, None] >= tile_end[None, :]).astype(I32), axis=1)
    tile_e = jnp.minimum(tile_e, N_EXPERTS - 1)
    in_expert = tile_eff - tile_start[tile_e]
    rows_in = counts[tile_e] - in_expert * MOE_CAP
    occ = jnp.clip((rows_in + MOE_SUB - 1) // MOE_SUB, 0, MOE_SUBS)
    tile_occ = jnp.where(active, occ, 0).astype(I32)
    tile_row = (x_start[tile_e] + in_expert * MOE_CAP).astype(I32)

    last_sub = jnp.where(counts > 0, x_start + (subs_per - 1) * MOE_SUB, -1)
    tail_ids = sub_end[-1] + jnp.arange(N_EXPERTS, dtype=I32)
    tail = jnp.where(tail_ids < n_subs, tail_ids * MOE_SUB, -1)
    zero_rows = jnp.concatenate([last_sub, tail]).astype(I32)
    return (n_tiles, n_subs * MOE_SUB, tile_e.astype(I32), tile_occ, tile_row,
            x_start, y_start, zero_rows)


def _layer(x2d, bsz, seq, g_norm_mix, w_in, b_in, w_conv, b_conv, w_sgu, b_sgu, g_sgu_ln, b_sgu_ln,
           g_mlstm_norm, w_proj_sgu, w_proj_mlstm, w_out, g_norm_moe, w_router, b_router,
           w_gate_up, b_gate_up, w_down, b_down, g_out):
    t = x2d.shape[0]

    pad = LANES - 2 * HEADS
    w_if = jnp.pad(w_in[:, N_MAIN:GATE_OFF], ((0, 0), (0, pad))).astype(BF16)
    b_if = jnp.pad(b_in[N_MAIN:GATE_OFF], (0, pad))[None, :]
    w_in_t = w_in.T
    w_mg_t = w_in_t[GATE_OFF:]
    bsb = jnp.broadcast_to(b_sgu.T[:, :, None], (SGU_BLOCK, SGU_GROUPS, SGU_GROUP_DIM))
    bsb = bsb.reshape(SGU_BLOCK, D_MODEL)
    w_r = jnp.pad(w_router, ((0, 0), (0, LANES - N_EXPERTS)))
    w_r_hi = w_r.astype(BF16)
    w_r_lo = (w_r - w_r_hi.astype(F32)).astype(BF16)
    b_r = jnp.pad(b_router, (0, LANES - N_EXPERTS))[None, :]

    h, gates = _norm_gates(x2d, g_norm_mix[None, :], w_if, b_if)
    zm = _in_proj(h, w_in_t, b_in[None, :], N_MAIN, 2 * D_MODEL, 3 * D_MODEL, "in_proj_main")
    zg = _in_proj(h, w_mg_t, b_in[None, GATE_OFF:], 2 * D_MODEL, 0, 0, "in_proj_gate")
    a = _sgu(zm, zg, w_sgu, bsb, g_sgu_ln[None, :], b_sgu_ln[None, :], w_proj_sgu.astype(BF16))
    ym = _mlstm(zm, gates, w_conv, b_conv[None, :], g_mlstm_norm[None, :], bsz, seq)
    x1, h2, logits = _merge(ym, a, zg, x2d, w_proj_mlstm.astype(BF16), w_out.astype(BF16),
                            g_norm_moe[None, :], w_r_hi, w_r_lo, b_r)
    info, gate, cnt = _route(logits)

    counts = cnt[0, :N_EXPERTS].astype(I32)
    (n_tiles, n_xrows, tile_e, tile_occ, tile_row, x_start, y_start,
     zero_rows) = _routing_tables(counts, t)
    route = info[:, :2 * TOP_K].reshape(-1)

    xs = _dispatch(route, x_start, zero_rows, h2, n_xrows)
    ys = _experts(tile_e, tile_occ, tile_row, xs, w_gate_up, b_gate_up[:, None, :],
                  w_down, b_down[:, None, :], n_tiles)
    return _combine(route, y_start, ys, x1, gate, g_out[None, :])


def kernel(x, g_norm_mix, w_in, b_in, w_conv, b_conv, w_sgu, b_sgu, g_sgu_ln, b_sgu_ln, g_mlstm_norm,
           w_proj_sgu, w_proj_mlstm, w_out, g_norm_moe, w_router, b_router, w_gate_up, b_gate_up,
           w_down, b_down, g_final):
    bsz, seq, d = x.shape
    assert d == D_MODEL and w_in.shape[0] == 1, "single-layer block with d_model 2048"
    out = _layer(x.reshape(bsz * seq, d), bsz, seq, g_norm_mix[0], w_in[0], b_in[0], w_conv[0],
                 b_conv[0], w_sgu[0], b_sgu[0], g_sgu_ln[0], b_sgu_ln[0], g_mlstm_norm[0],
                 w_proj_sgu[0], w_proj_mlstm[0], w_out[0], g_norm_moe[0], w_router[0], b_router[0],
                 w_gate_up[0], b_gate_up[0], w_down[0], b_down[0], g_final)
    return out.reshape(bsz, seq, d)
```

```python
import functools

import jax
import jax.numpy as jnp
from jax import lax
from jax.experimental import pallas as pl
from jax.experimental.pallas import tpu as pltpu

F32 = jnp.float32
BF16 = jnp.bfloat16
I32 = jnp.int32

D_MODEL = 2048
CHUNK = 64
SGU_BLOCK = 128
SGU_GROUPS = 8
SGU_GROUP_DIM = 256
HEADS = 4
HEAD_DIM = 512
CONV_WIDTH = 4
N_EXPERTS = 32
TOP_K = 4
D_FF = 2048
SWIGLU_LIMIT = 7.0
SWIGLU_ALPHA = 1.702
EPS = 1e-6

LANES = 128
SUBLANES = 8
VMEM_LIMIT = 56 * 1024 * 1024

ZB_U, ZB_V, ZB_Q, ZB_K, ZB_VM, ZB_O = range(6)
ZG_A, ZG_M = range(2)
N_MAIN = 6 * D_MODEL
GATE_OFF = N_MAIN + 2 * HEADS

NORM_TM = 512
INPROJ_TM = 1024
INPROJ_TN = 1024
INPROJ_RC = 512
SGU_TM = 256
MLSTM_L = 256
MERGE_TM = 256
MERGE_RC = 256
ROUTE_TM = 256
DISPATCH_TM = 256
MOE_SUB = 128
MOE_CAP = 1280
MOE_SUBS = MOE_CAP // MOE_SUB
MOE_TF = 512
MOE_NF = D_FF // MOE_TF
MOE_TN = 512
MOE_NN = D_MODEL // MOE_TN
COMBINE_TM = 128


def _cparams(sem, vmem_limit=VMEM_LIMIT):
    return pltpu.CompilerParams(dimension_semantics=sem, vmem_limit_bytes=vmem_limit)


def _gelu(x):
    return 0.5 * x * (1.0 + lax.erf(x * (2.0 ** -0.5)))


def _sigmoid(x):
    return 1.0 / (1.0 + jnp.exp(-x))


def _norm_gates_kernel(x_ref, g_ref, wif_ref, bif_ref, h_ref, gates_ref):
    x = x_ref[...]
    ms = jnp.mean(x * x, axis=-1, keepdims=True)
    h = (x * lax.rsqrt(ms + EPS) * g_ref[...]).astype(BF16)
    h_ref[...] = h
    gates_ref[...] = jnp.dot(h, wif_ref[...], preferred_element_type=F32) + bif_ref[...]


def _norm_gates(x, g, wif, bif):
    t = x.shape[0]
    tm = min(NORM_TM, t)
    row = lambda i: (i, 0)
    const = lambda i: (0, 0)
    return pl.pallas_call(
        _norm_gates_kernel,
        grid=(t // tm,),
        in_specs=[
            pl.BlockSpec((tm, D_MODEL), row),
            pl.BlockSpec((1, D_MODEL), const),
            pl.BlockSpec((D_MODEL, LANES), const),
            pl.BlockSpec((1, LANES), const),
        ],
        out_specs=[pl.BlockSpec((tm, D_MODEL), row), pl.BlockSpec((tm, LANES), row)],
        out_shape=[jax.ShapeDtypeStruct((t, D_MODEL), BF16), jax.ShapeDtypeStruct((t, LANES), F32)],
        compiler_params=_cparams(("arbitrary",)),
        name="norm_gates",
    )(x, g, wif, bif)


def _inproj_kernel(h_ref, w_ref, b_ref, z_ref, w_scr, *, n_gelu, n_plain):
    j = pl.program_id(0)

    @pl.when(pl.program_id(1) == 0)
    def _():
        w_scr[...] = w_ref[...].astype(BF16)

    def project(activation):
        tm = h_ref.shape[0]
        rc = min(INPROJ_RC, tm)
        for c in range(tm // rc):
            rows = slice(c * rc, (c + 1) * rc)
            acc = lax.dot_general(h_ref[rows, :], w_scr[...], (((1,), (1,)), ((), ())),
                                  preferred_element_type=F32) + b_ref[...]
            z_ref[rows, :] = activation(acc).astype(z_ref.dtype)

    @pl.when(j < n_gelu)
    def _():
        project(_gelu)

    @pl.when((j >= n_gelu) & (j < n_gelu + n_plain))
    def _():
        project(lambda acc: acc)

    @pl.when(j >= n_gelu + n_plain)
    def _():
        project(_sigmoid)


def _in_proj(h, w_t, b, n_cols, n_gelu_cols, n_plain_cols, name):
    t = h.shape[0]
    tm, tn = min(INPROJ_TM, t), INPROJ_TN
    return pl.pallas_call(
        functools.partial(_inproj_kernel, n_gelu=n_gelu_cols // tn, n_plain=n_plain_cols // tn),
        grid=(n_cols // tn, t // tm),
        in_specs=[
            pl.BlockSpec((tm, D_MODEL), lambda j, i: (i, 0)),
            pl.BlockSpec((tn, D_MODEL), lambda j, i: (j, 0)),
            pl.BlockSpec((1, tn), lambda j, i: (0, j)),
        ],
        out_specs=pl.BlockSpec((tm, tn), lambda j, i: (i, j)),
        out_shape=jax.ShapeDtypeStruct((t, n_cols), BF16),
        scratch_shapes=[pltpu.VMEM((tn, D_MODEL), BF16)],
        compiler_params=_cparams(("arbitrary", "arbitrary")),
        name=name,
    )(h, w_t, b)


def _sgu_kernel(gu_ref, gv_ref, sga_ref, ws_ref, bsb_ref, lng_ref, lnb_ref, pa_ref, out_ref,
                ya_scr):
    tm = gu_ref.shape[0]
    v = gv_ref[...].astype(F32)
    mu = jnp.mean(v, axis=-1, keepdims=True)
    vc = v - mu
    var = jnp.mean(vc * vc, axis=-1, keepdims=True)
    vln = (vc * lax.rsqrt(var + EPS) * lng_ref[...] + lnb_ref[...]).astype(BF16)

    t_id = lax.broadcasted_iota(I32, (SGU_BLOCK, SGU_BLOCK), 0) // CHUNK
    s_id = lax.broadcasted_iota(I32, (SGU_BLOCK, SGU_BLOCK), 1) // CHUNK
    causal = s_id <= t_id
    for g in range(SGU_GROUPS):
        w = jnp.where(causal, ws_ref[g], 0.0).astype(BF16)
        cols = slice(g * SGU_GROUP_DIM, (g + 1) * SGU_GROUP_DIM)
        bias = bsb_ref[:, cols]
        for blk in range(tm // SGU_BLOCK):
            rows = slice(blk * SGU_BLOCK, (blk + 1) * SGU_BLOCK)
            mixed = jnp.dot(w, vln[rows, cols], preferred_element_type=F32) + bias
            ya_scr[rows, cols] = (gu_ref[rows, cols].astype(F32) * mixed).astype(BF16)

    proj = jnp.dot(ya_scr[...], pa_ref[...], preferred_element_type=F32)
    out_ref[...] = (sga_ref[...].astype(F32) * proj).astype(out_ref.dtype)


def _sgu(zm, zg, ws, bsb, lng, lnb, pa):
    t = zm.shape[0]
    tm = SGU_TM
    const2 = lambda i: (0, 0)
    return pl.pallas_call(
        _sgu_kernel,
        grid=(t // tm,),
        in_specs=[
            pl.BlockSpec((tm, D_MODEL), lambda i: (i, ZB_U)),
            pl.BlockSpec((tm, D_MODEL), lambda i: (i, ZB_V)),
            pl.BlockSpec((tm, D_MODEL), lambda i: (i, ZG_A)),
            pl.BlockSpec((SGU_GROUPS, SGU_BLOCK, SGU_BLOCK), lambda i: (0, 0, 0)),
            pl.BlockSpec((SGU_BLOCK, D_MODEL), const2),
            pl.BlockSpec((1, D_MODEL), const2),
            pl.BlockSpec((1, D_MODEL), const2),
            pl.BlockSpec((D_MODEL, D_MODEL), const2),
        ],
        out_specs=pl.BlockSpec((tm, D_MODEL), lambda i: (i, 0)),
        out_shape=jax.ShapeDtypeStruct((t, D_MODEL), BF16),
        scratch_shapes=[pltpu.VMEM((tm, D_MODEL), BF16)],
        compiler_params=_cparams(("arbitrary",)),
        name="sgu",
    )(zm, zm, zg, ws, bsb, lng, lnb, pa)


def _mlstm_kernel(q_ref, k_ref, v_ref, so_ref, gates_ref, wc_ref, bc_ref, gn_ref, out_ref,
                  hist, ct_scr, n_scr, m_scr):
    L = q_ref.shape[0]
    c = pl.program_id(1)

    @pl.when(c == 0)
    def _():
        hist[0:SUBLANES, :] = jnp.zeros((SUBLANES, 2 * D_MODEL), F32)
        ct_scr[...] = jnp.zeros_like(ct_scr)
        n_scr[...] = jnp.zeros_like(n_scr)
        m_scr[...] = jnp.zeros_like(m_scr)

    hist[SUBLANES:SUBLANES + L, 0:D_MODEL] = q_ref[...].astype(F32)
    hist[SUBLANES:SUBLANES + L, D_MODEL:2 * D_MODEL] = k_ref[...].astype(F32)
    qk = bc_ref[...] + wc_ref[CONV_WIDTH - 1:CONV_WIDTH, :] * hist[SUBLANES:SUBLANES + L, :]
    for j in range(CONV_WIDTH - 1):
        off = SUBLANES - (CONV_WIDTH - 1) + j
        qk = qk + wc_ref[j:j + 1, :] * hist[off:off + L, :]
    qk = qk * _sigmoid(qk)
    hist[0:SUBLANES, :] = hist[L:L + SUBLANES, :]

    gts = gates_ref[...]
    lf = jnp.minimum(gts, 0.0) - jnp.log(1.0 + jnp.exp(-jnp.abs(gts)))
    row = lax.broadcasted_iota(I32, (L, L), 0)
    col = lax.broadcasted_iota(I32, (L, L), 1)
    tril = col <= row
    tri = jnp.where(tril, 1.0, 0.0).astype(BF16)
    lf_hi = lf.astype(BF16)
    lf_lo = (lf - lf_hi.astype(F32)).astype(BF16)
    bcum = (jnp.dot(tri, lf_hi, preferred_element_type=F32)
            + jnp.dot(tri, lf_lo, preferred_element_type=F32))
    bcum_t = bcum.T
    gts_t = gts.T

    scale = HEAD_DIM ** -0.5
    for h in range(HEADS):
        cols = slice(h * HEAD_DIM, (h + 1) * HEAD_DIM)
        q = qk[:, h * HEAD_DIM:(h + 1) * HEAD_DIM] * scale
        k = qk[:, D_MODEL + h * HEAD_DIM:D_MODEL + (h + 1) * HEAD_DIM]
        qb = q.astype(BF16)
        kb = k.astype(BF16)
        vb = v_ref[:, cols]
        bc = bcum[:, HEADS + h:HEADS + h + 1]
        bc_r = bcum_t[HEADS + h:HEADS + h + 1, :]
        ig_c = gts[:, h:h + 1]
        ig_r = gts_t[h:h + 1, :]
        m_st = m_scr[h]
        n_st = n_scr[h]

        dlog = jnp.where(tril, bc - bc_r + ig_r, -jnp.inf)
        m_inter = bc + m_st
        m_t = jnp.maximum(m_inter, jnp.max(dlog, axis=-1, keepdims=True))
        p = jnp.exp(dlog - m_t)
        s = lax.dot_general(qb, kb, (((1,), (1,)), ((), ())), preferred_element_type=F32)
        sc = s * p
        inter = jnp.exp(m_inter - m_t)
        ctb = ct_scr[h].astype(BF16)
        num = (jnp.dot(sc.astype(BF16), vb, preferred_element_type=F32)
               + inter * jnp.dot(qb, ctb, preferred_element_type=F32))
        den = (jnp.sum(sc, axis=-1, keepdims=True)
               + inter * jnp.sum(q * n_st, axis=-1, keepdims=True))
        hh = num / jnp.maximum(jnp.abs(den), jnp.exp(-m_t))
        hn = hh * lax.rsqrt(jnp.mean(hh * hh, axis=-1, keepdims=True) + EPS) * gn_ref[:, cols]
        out_ref[:, cols] = (hn * so_ref[:, cols].astype(F32)).astype(out_ref.dtype)

        g_tot = bc[L - 1:L, :]
        a = g_tot - bc + ig_c
        m_new = jnp.maximum(g_tot + m_st, jnp.max(a, axis=0, keepdims=True))
        wa = jnp.exp(a - m_new)
        decay = jnp.exp(g_tot + m_st - m_new)
        wv = (wa * vb.astype(F32)).astype(BF16)
        upd = jnp.dot(k.T.astype(BF16), wv, preferred_element_type=F32)
        ct_scr[h] = decay * ct_scr[h] + upd
        n_scr[h] = decay * n_st + jnp.sum(wa * k, axis=0, keepdims=True)
        m_scr[h] = m_new


def _mlstm(zm, gates, wc, bc, gn, bsz, seq):
    t = zm.shape[0]
    L = MLSTM_L
    nc = seq // L
    zspec = lambda blk: pl.BlockSpec((L, D_MODEL), lambda b, c: (b * nc + c, blk))
    const2 = lambda b, c: (0, 0)
    return pl.pallas_call(
        _mlstm_kernel,
        grid=(bsz, nc),
        in_specs=[
            zspec(ZB_Q), zspec(ZB_K), zspec(ZB_VM), zspec(ZB_O),
            pl.BlockSpec((L, LANES), lambda b, c: (b * nc + c, 0)),
            pl.BlockSpec((CONV_WIDTH, 2 * D_MODEL), const2),
            pl.BlockSpec((1, 2 * D_MODEL), const2),
            pl.BlockSpec((1, D_MODEL), const2),
        ],
        out_specs=pl.BlockSpec((L, D_MODEL), lambda b, c: (b * nc + c, 0)),
        out_shape=jax.ShapeDtypeStruct((t, D_MODEL), BF16),
        scratch_shapes=[
            pltpu.VMEM((L + 2 * SUBLANES, 2 * D_MODEL), F32),
            pltpu.VMEM((HEADS, HEAD_DIM, HEAD_DIM), F32),
            pltpu.VMEM((HEADS, 1, HEAD_DIM), F32),
            pltpu.VMEM((HEADS, 1, 1), F32),
        ],
        compiler_params=_cparams(("arbitrary", "arbitrary")),
        name="mlstm",
    )(zm, zm, zm, zm, gates, wc, bc, gn)


def _merge_kernel(ym_ref, a_ref, sgm_ref, x_ref, pm_ref, wo_ref, g2_ref, wrh_ref, wrl_ref, br_ref,
                  x1_ref, h2_ref, lg_ref):
    tm = x_ref.shape[0]
    rc = min(MERGE_RC, tm)
    for c in range(tm // rc):
        rows = slice(c * rc, (c + 1) * rc)
        proj = jnp.dot(ym_ref[rows, :], pm_ref[...], preferred_element_type=F32)
        merged = a_ref[rows, :].astype(F32) + sgm_ref[rows, :].astype(F32) * proj
        x1 = x_ref[rows, :] + jnp.dot(merged.astype(BF16), wo_ref[...],
                                      preferred_element_type=F32)
        x1_ref[rows, :] = x1
        h2 = x1 * lax.rsqrt(jnp.mean(x1 * x1, axis=-1, keepdims=True) + EPS) * g2_ref[...]
        h2_ref[rows, :] = h2
        h_hi = h2.astype(BF16)
        h_lo = (h2 - h_hi.astype(F32)).astype(BF16)
        lg_ref[rows, :] = (jnp.dot(h_hi, wrh_ref[...], preferred_element_type=F32)
                           + (jnp.dot(h_lo, wrh_ref[...], preferred_element_type=F32)
                              + jnp.dot(h_hi, wrl_ref[...], preferred_element_type=F32))
                           + br_ref[...])


def _merge(ym, a, zg, x, pm, wo, g2, wr_hi, wr_lo, br):
    t = x.shape[0]
    tm = MERGE_TM
    const2 = lambda i: (0, 0)
    row = lambda i: (i, 0)
    return pl.pallas_call(
        _merge_kernel,
        grid=(t // tm,),
        in_specs=[
            pl.BlockSpec((tm, D_MODEL), row),
            pl.BlockSpec((tm, D_MODEL), row),
            pl.BlockSpec((tm, D_MODEL), lambda i: (i, ZG_M)),
            pl.BlockSpec((tm, D_MODEL), row),
            pl.BlockSpec((D_MODEL, D_MODEL), const2),
            pl.BlockSpec((D_MODEL, D_MODEL), const2),
            pl.BlockSpec((1, D_MODEL), const2),
            pl.BlockSpec((D_MODEL, LANES), const2),
            pl.BlockSpec((D_MODEL, LANES), const2),
            pl.BlockSpec((1, LANES), const2),
        ],
        out_specs=[
            pl.BlockSpec((tm, D_MODEL), row),
            pl.BlockSpec((tm, D_MODEL), row),
            pl.BlockSpec((tm, LANES), row),
        ],
        out_shape=[
            jax.ShapeDtypeStruct((t, D_MODEL), F32),
            jax.ShapeDtypeStruct((t, D_MODEL), F32),
            jax.ShapeDtypeStruct((t, LANES), F32),
        ],
        compiler_params=_cparams(("arbitrary",)),
        name="merge",
    )(ym, a, zg, x, pm, wo, g2, wr_hi, wr_lo, br)


def _route_kernel(lg_ref, info_ref, gate_ref, cnt_ref, cnt_scr):
    tm = lg_ref.shape[0]
    i = pl.program_id(0)

    @pl.when(i == 0)
    def _():
        cnt_scr[...] = jnp.zeros_like(cnt_scr)

    lane = lax.broadcasted_iota(I32, (tm, LANES), 1)
    lane_f = lane.astype(F32)
    vals = jnp.where(lane < N_EXPERTS, lg_ref[...], -jnp.inf)
    tops, ids, hots = [], [], []
    for _ in range(TOP_K):
        m = jnp.max(vals, axis=-1, keepdims=True)
        idx_f = jnp.min(jnp.where(vals == m, lane_f, float(LANES)), axis=-1, keepdims=True)
        idx = idx_f.astype(I32)
        hot = lane == idx
        tops.append(m)
        ids.append(idx)
        hots.append(hot)
        vals = jnp.where(hot, -jnp.inf, vals)

    exps = [jnp.exp(tv - tops[0]) for tv in tops]
    denom = exps[0] + exps[1] + exps[2] + exps[3]

    sel = jnp.zeros((tm, LANES), F32)
    for hot in hots:
        sel = sel + jnp.where(hot, 1.0, 0.0)
    r = lax.broadcasted_iota(I32, (tm, tm), 0)
    c = lax.broadcasted_iota(I32, (tm, tm), 1)
    strict = jnp.where(c < r, 1.0, 0.0).astype(BF16)
    ahead = jnp.dot(strict, sel.astype(BF16), preferred_element_type=F32) + cnt_scr[...]

    info = jnp.zeros((tm, LANES), I32)
    gate = jnp.zeros((tm, LANES), F32)
    for k in range(TOP_K):
        pos = jnp.sum(jnp.where(hots[k], ahead, 0.0), axis=-1, keepdims=True).astype(I32)
        info = jnp.where(lane == k, ids[k], info)
        info = jnp.where(lane == TOP_K + k, pos, info)
        gate = jnp.where(lane == k, exps[k] / denom, gate)
    info_ref[...] = info
    gate_ref[...] = gate
    cnt_scr[...] = cnt_scr[...] + jnp.sum(sel, axis=0, keepdims=True)
    cnt_ref[...] = jnp.broadcast_to(cnt_scr[...], cnt_ref.shape)


def _route(logits):
    t = logits.shape[0]
    tm = ROUTE_TM
    row = lambda i: (i, 0)
    return pl.pallas_call(
        _route_kernel,
        grid=(t // tm,),
        in_specs=[pl.BlockSpec((tm, LANES), row)],
        out_specs=[
            pl.BlockSpec((tm, LANES), row),
            pl.BlockSpec((tm, LANES), row),
            pl.BlockSpec((SUBLANES, LANES), lambda i: (0, 0)),
        ],
        out_shape=[
            jax.ShapeDtypeStruct((t, LANES), I32),
            jax.ShapeDtypeStruct((t, LANES), F32),
            jax.ShapeDtypeStruct((SUBLANES, LANES), F32),
        ],
        scratch_shapes=[pltpu.VMEM((1, LANES), F32)],
        compiler_params=_cparams(("arbitrary",)),
        name="route",
    )(logits)


def _grouped_row(route_ref, start_ref, token, k):
    base = token * (2 * TOP_K)
    return start_ref[route_ref[base + k]] + route_ref[base + TOP_K + k]


def _dispatch_kernel(route_ref, start_ref, zero_row_ref, h2_ref, xs_hbm, zbuf, sem, zsem, *,
                     n_zero):
    tm = h2_ref.shape[0]
    i = pl.program_id(0)

    def zero_copy(j):
        row0 = pl.multiple_of(zero_row_ref[j], MOE_SUB)
        return pltpu.make_async_copy(zbuf, xs_hbm.at[pl.ds(row0, MOE_SUB)], zsem)

    @pl.when(i == 0)
    def _():
        zbuf[...] = jnp.zeros_like(zbuf)

        def start(j, carry):
            @pl.when(zero_row_ref[j] >= 0)
            def _():
                zero_copy(j).start()
            return carry

        def wait(j, carry):
            @pl.when(zero_row_ref[j] >= 0)
            def _():
                zero_copy(j).wait()
            return carry

        lax.fori_loop(0, n_zero, start, 0)
        lax.fori_loop(0, n_zero, wait, 0)

    for r in range(tm):
        for k in range(TOP_K):
            d = _grouped_row(route_ref, start_ref, i * tm + r, k)
            pltpu.make_async_copy(h2_ref.at[pl.ds(r, 1)], xs_hbm.at[pl.ds(d, 1)],
                                  sem).start(priority=k % 2)

    for _ in range(TOP_K):
        pltpu.make_async_copy(h2_ref, xs_hbm.at[pl.ds(0, tm)], sem).wait()


def _dispatch(route, starts, zero_rows, h2, n_rows):
    t = h2.shape[0]
    tm = DISPATCH_TM
    grid_spec = pltpu.PrefetchScalarGridSpec(
        num_scalar_prefetch=3,
        grid=(t // tm,),
        in_specs=[pl.BlockSpec((tm, D_MODEL), lambda i, r, s, z: (i, 0))],
        out_specs=pl.BlockSpec(memory_space=pl.ANY),
        scratch_shapes=[
            pltpu.VMEM((MOE_SUB, D_MODEL), F32),
            pltpu.SemaphoreType.DMA,
            pltpu.SemaphoreType.DMA,
        ],
    )
    return pl.pallas_call(
        functools.partial(_dispatch_kernel, n_zero=zero_rows.shape[0]),
        grid_spec=grid_spec,
        out_shape=jax.ShapeDtypeStruct((n_rows, D_MODEL), F32),
        compiler_params=_cparams(("arbitrary",)),
        name="dispatch",
    )(route, starts, zero_rows, h2)


def _experts_kernel(tile_e_ref, tile_occ_ref, tile_row_ref,
                    xs_hbm, wg_ref, wu_ref, bg_ref, bu_ref, wd_ref, bd_ref, out_ref,
                    xbuf, xb_scr, act_scr, sem, *, n_tiles):
    del tile_e_ref
    i = pl.program_id(0)
    j = pl.program_id(1)
    occ = tile_occ_ref[i]

    def sub_rows(sub):
        return pl.ds(pl.multiple_of(sub * MOE_SUB, MOE_SUB), MOE_SUB)

    def sub_copy(tile, sub):
        row0 = pl.multiple_of(tile_row_ref[tile] + sub * MOE_SUB, MOE_SUB)
        return pltpu.make_async_copy(xs_hbm.at[pl.ds(row0, MOE_SUB)], xbuf.at[sub_rows(sub)], sem)

    def start_load(tile):
        def body(sub, carry):
            sub_copy(tile, sub).start()
            return carry
        lax.fori_loop(0, tile_occ_ref[tile], body, 0)

    @pl.when(j == 0)
    def _():
        @pl.when(i == 0)
        def _():
            start_load(0)

        def wait_sub(sub, carry):
            sub_copy(i, sub).wait()
            return carry

        def round_sub(sub, carry):
            xb_scr[sub_rows(sub), :] = xbuf[sub_rows(sub), :].astype(BF16)
            return carry

        lax.fori_loop(0, occ, wait_sub, 0)
        lax.fori_loop(0, occ, round_sub, 0)

        @pl.when(i + 1 < n_tiles)
        def _():
            start_load(i + 1)

    def for_each_sub(weights, fn):
        def quad(p, carry):
            w = weights()
            for u in range(4):
                fn(4 * p + u, *w)
            return carry

        lax.fori_loop(0, occ // 4, quad, 0)
        done = (occ // 4) * 4

        @pl.when(occ % 4 >= 2)
        def _():
            w = weights()
            fn(done, *w)
            fn(done + 1, *w)

        @pl.when(occ % 2 == 1)
        def _():
            fn(occ - 1, *weights())

    @pl.when(j < MOE_NF)
    def _():
        def weights():
            return wg_ref[...].astype(BF16), wu_ref[...].astype(BF16)

        def gate_up(sub, wg, wu):
            rows = sub_rows(sub)
            xb = xb_scr[rows, :]
            gate = jnp.dot(xb, wg, preferred_element_type=F32) + bg_ref[...]
            up = jnp.dot(xb, wu, preferred_element_type=F32) + bu_ref[...]
            gate = jnp.minimum(gate, SWIGLU_LIMIT)
            up = jnp.clip(up, -SWIGLU_LIMIT, SWIGLU_LIMIT)
            act = (up + 1.0) * (gate * _sigmoid(SWIGLU_ALPHA * gate))
            act_scr[j, rows, :] = act.astype(BF16)

        for_each_sub(weights, gate_up)

    @pl.when(j >= MOE_NF)
    def _():
        def weights():
            return (wd_ref[...].astype(BF16),)

        def down(sub, wd):
            rows = sub_rows(sub)
            y = bd_ref[...] + jnp.dot(act_scr[0, rows, :], wd[0:MOE_TF, :],
                                      preferred_element_type=F32)
            for c in range(1, MOE_NF):
                y = y + jnp.dot(act_scr[c, rows, :], wd[c * MOE_TF:(c + 1) * MOE_TF, :],
                                preferred_element_type=F32)
            out_ref[rows, :] = y

        for_each_sub(weights, down)

    @pl.when(j >= MOE_NF)
    def _():
        def zero_sub(sub, carry):
            out_ref[sub_rows(sub), :] = jnp.zeros((MOE_SUB, MOE_TN), F32)
            return carry

        lax.fori_loop(occ, MOE_SUBS, zero_sub, 0)


def _experts(tile_e, tile_occ, tile_row, xs, wgu, bgu, wd, bd, n_tiles):
    up_off = D_FF // MOE_TF

    def fa(i, j, to):
        return jnp.where(to[i] > 0, jnp.minimum(j, MOE_NF - 1), MOE_NF - 1)

    def fb(i, j, to):
        return jnp.where(to[i] > 0, jnp.maximum(j - MOE_NF, 0), MOE_NN - 1)

    def gate_map(i, j, te, to, tr):
        return (te[i], 0, fa(i, j, to))

    def up_map(i, j, te, to, tr):
        return (te[i], 0, up_off + fa(i, j, to))

    def down_map(i, j, te, to, tr):
        return (te[i], 0, fb(i, j, to))

    def out_map(i, j, te, to, tr):
        return (i, jnp.maximum(j - MOE_NF, 0))

    grid_spec = pltpu.PrefetchScalarGridSpec(
        num_scalar_prefetch=3,
        grid=(n_tiles, MOE_NF + MOE_NN),
        in_specs=[
            pl.BlockSpec(memory_space=pl.ANY),
            pl.BlockSpec((None, D_MODEL, MOE_TF), gate_map),
            pl.BlockSpec((None, D_MODEL, MOE_TF), up_map),
            pl.BlockSpec((None, 1, MOE_TF), gate_map),
            pl.BlockSpec((None, 1, MOE_TF), up_map),
            pl.BlockSpec((None, D_FF, MOE_TN), down_map),
            pl.BlockSpec((None, 1, MOE_TN), down_map),
        ],
        out_specs=pl.BlockSpec((MOE_CAP, MOE_TN), out_map),
        scratch_shapes=[
            pltpu.VMEM((MOE_CAP, D_MODEL), F32),
            pltpu.VMEM((MOE_CAP, D_MODEL), BF16),
            pltpu.VMEM((MOE_NF, MOE_CAP, MOE_TF), BF16),
            pltpu.SemaphoreType.DMA,
        ],
    )
    return pl.pallas_call(
        functools.partial(_experts_kernel, n_tiles=n_tiles),
        grid_spec=grid_spec,
        out_shape=jax.ShapeDtypeStruct((n_tiles * MOE_CAP, D_MODEL), F32),
        compiler_params=_cparams(("arbitrary", "arbitrary")),
        name="experts",
    )(tile_e, tile_occ, tile_row, xs, wgu, wgu, bgu, bgu, wd, bd)


def _combine_kernel(route_ref, start_ref, ys_hbm, x1_ref, gate_ref, gf_ref, out_ref, buf, sem, *,
                    n_tiles):
    tm = x1_ref.shape[0]
    i = pl.program_id(0)
    slot = i % 2

    def start_gather(tile, s):
        for r in range(tm):
            for k in range(TOP_K):
                d = _grouped_row(route_ref, start_ref, tile * tm + r, k)
                pltpu.make_async_copy(ys_hbm.at[pl.ds(d, 1)], buf.at[s, k, pl.ds(r, 1)],
                                      sem.at[s]).start(priority=k % 2)

    @pl.when(i == 0)
    def _():
        start_gather(0, 0)

    for k in range(TOP_K):
        pltpu.make_async_copy(ys_hbm.at[pl.ds(0, tm)], buf.at[slot, k], sem.at[slot]).wait()

    for s in range(2):
        @pl.when((i + 1 < n_tiles) & (slot == 1 - s))
        def _():
            start_gather(i + 1, s)

    x2 = x1_ref[...]
    for k in range(TOP_K):
        x2 = x2 + gate_ref[:, k:k + 1] * buf[slot, k]
    out_ref[...] = x2 * lax.rsqrt(jnp.mean(x2 * x2, axis=-1, keepdims=True) + EPS) * gf_ref[...]


def _combine(route, starts, ys, x1, gate, gf):
    t = x1.shape[0]
    tm = COMBINE_TM
    n_tiles = t // tm
    grid_spec = pltpu.PrefetchScalarGridSpec(
        num_scalar_prefetch=2,
        grid=(n_tiles,),
        in_specs=[
            pl.BlockSpec(memory_space=pl.ANY),
            pl.BlockSpec((tm, D_MODEL), lambda i, r, s: (i, 0)),
            pl.BlockSpec((tm, LANES), lambda i, r, s: (i, 0)),
            pl.BlockSpec((1, D_MODEL), lambda i, r, s: (0, 0)),
        ],
        out_specs=pl.BlockSpec((tm, D_MODEL), lambda i, r, s: (i, 0)),
        scratch_shapes=[
            pltpu.VMEM((2, TOP_K, tm, D_MODEL), F32),
            pltpu.SemaphoreType.DMA((2,)),
        ],
    )
    return pl.pallas_call(
        functools.partial(_combine_kernel, n_tiles=n_tiles),
        grid_spec=grid_spec,
        out_shape=jax.ShapeDtypeStruct((t, D_MODEL), F32),
        compiler_params=_cparams(("arbitrary",)),
        name="combine",
    )(route, starts, ys, x1, gate, gf)


def _routing_tables(counts, n_tokens):
    n_tiles = N_EXPERTS + (n_tokens * TOP_K) // MOE_CAP
    n_subs = N_EXPERTS + (n_tokens * TOP_K) // MOE_SUB
    subs_per = (counts + MOE_SUB - 1) // MOE_SUB
    sub_end = jnp.cumsum(subs_per)
    x_start = (sub_end - subs_per) * MOE_SUB
    tiles_per = (counts + MOE_CAP - 1) // MOE_CAP
    tile_end = jnp.cumsum(tiles_per)
    tile_start = tile_end - tiles_per
    y_start = tile_start * MOE_CAP

    tile_ids = jnp.arange(n_tiles, dtype=I32)
    n_active = tile_end[-1]
    last = jnp.maximum(n_active - 1, 0)
    active = tile_ids < n_active
    tile_eff = jnp.minimum(tile_ids, last)
    tile_e = jnp.sum((tile_eff[:, None] >= tile_end[None, :]).astype(I32), axis=1)
    tile_e = jnp.minimum(tile_e, N_EXPERTS - 1)
    in_expert = tile_eff - tile_start[tile_e]
    rows_in = counts[tile_e] - in_expert * MOE_CAP
    occ = jnp.clip((rows_in + MOE_SUB - 1) // MOE_SUB, 0, MOE_SUBS)
    tile_occ = jnp.where(active, occ, 0).astype(I32)
    tile_row = (x_start[tile_e] + in_expert * MOE_CAP).astype(I32)

    last_sub = jnp.where(counts > 0, x_start + (subs_per - 1) * MOE_SUB, -1)
    tail_ids = sub_end[-1] + jnp.arange(N_EXPERTS, dtype=I32)
    tail = jnp.where(tail_ids < n_subs, tail_ids * MOE_SUB, -1)
    zero_rows = jnp.concatenate([last_sub, tail]).astype(I32)
    return (n_tiles, n_subs * MOE_SUB, tile_e.astype(I32), tile_occ, tile_row,
            x_start, y_start, zero_rows)


def _layer(x2d, bsz, seq, g_norm_mix, w_in, b_in, w_conv, b_conv, w_sgu, b_sgu, g_sgu_ln, b_sgu_ln,
           g_mlstm_norm, w_proj_sgu, w_proj_mlstm, w_out, g_norm_moe, w_router, b_router,
           w_gate_up, b_gate_up, w_down, b_down, g_out):
    t = x2d.shape[0]

    pad = LANES - 2 * HEADS
    w_if = jnp.pad(w_in[:, N_MAIN:GATE_OFF], ((0, 0), (0, pad))).astype(BF16)
    b_if = jnp.pad(b_in[N_MAIN:GATE_OFF], (0, pad))[None, :]
    w_in_t = w_in.T
    w_mg_t = w_in_t[GATE_OFF:]
    bsb = jnp.broadcast_to(b_sgu.T[:, :, None], (SGU_BLOCK, SGU_GROUPS, SGU_GROUP_DIM))
    bsb = bsb.reshape(SGU_BLOCK, D_MODEL)
    w_r = jnp.pad(w_router, ((0, 0), (0, LANES - N_EXPERTS)))
    w_r_hi = w_r.astype(BF16)
    w_r_lo = (w_r - w_r_hi.astype(F32)).astype(BF16)
    b_r = jnp.pad(b_router, (0, LANES - N_EXPERTS))[None, :]

    h, gates = _norm_gates(x2d, g_norm_mix[None, :], w_if, b_if)
    zm = _in_proj(h, w_in_t, b_in[None, :], N_MAIN, 2 * D_MODEL, 3 * D_MODEL, "in_proj_main")
    zg = _in_proj(h, w_mg_t, b_in[None, GATE_OFF:], 2 * D_MODEL, 0, 0, "in_proj_gate")
    a = _sgu(zm, zg, w_sgu, bsb, g_sgu_ln[None, :], b_sgu_ln[None, :], w_proj_sgu.astype(BF16))
    ym = _mlstm(zm, gates, w_conv, b_conv[None, :], g_mlstm_norm[None, :], bsz, seq)
    x1, h2, logits = _merge(ym, a, zg, x2d, w_proj_mlstm.astype(BF16), w_out.astype(BF16),
                            g_norm_moe[None, :], w_r_hi, w_r_lo, b_r)
    info, gate, cnt = _route(logits)

    counts = cnt[0, :N_EXPERTS].astype(I32)
    (n_tiles, n_xrows, tile_e, tile_occ, tile_row, x_start, y_start,
     zero_rows) = _routing_tables(counts, t)
    route = info[:, :2 * TOP_K].reshape(-1)

    xs = _dispatch(route, x_start, zero_rows, h2, n_xrows)
    ys = _experts(tile_e, tile_occ, tile_row, xs, w_gate_up, b_gate_up[:, None, :],
                  w_down, b_down[:, None, :], n_tiles)
    return _combine(route, y_start, ys, x1, gate, g_out[None, :])


def kernel(x, g_norm_mix, w_in, b_in, w_conv, b_conv, w_sgu, b_sgu, g_sgu_ln, b_sgu_ln, g_mlstm_norm,
           w_proj_sgu, w_proj_mlstm, w_out, g_norm_moe, w_router, b_router, w_gate_up, b_gate_up,
           w_down, b_down, g_final):
    bsz, seq, d = x.shape
    assert d == D_MODEL and w_in.shape[0] == 1, "single-layer block with d_model 2048"
    out = _layer(x.reshape(bsz * seq, d), bsz, seq, g_norm_mix[0], w_in[0], b_in[0], w_conv[0],
                 b_conv[0], w_sgu[0], b_sgu[0], g_sgu_ln[0], b_sgu_ln[0], g_mlstm_norm[0],
                 w_proj_sgu[0], w_proj_mlstm[0], w_out[0], g_norm_moe[0], w_router[0], b_router[0],
                 w_gate_up[0], b_gate_up[0], w_down[0], b_down[0], g_final)
    return out.reshape(bsz, seq, d)
```

```python
import functools

import jax
import jax.numpy as jnp
from jax import lax
from jax.experimental import pallas as pl
from jax.experimental.pallas import tpu as pltpu

F32 = jnp.float32
BF16 = jnp.bfloat16
I32 = jnp.int32

D_MODEL = 2048
CHUNK = 64
SGU_BLOCK = 128
SGU_GROUPS = 8
SGU_GROUP_DIM = 256
HEADS = 4
HEAD_DIM = 512
CONV_WIDTH = 4
N_EXPERTS = 32
TOP_K = 4
D_FF = 2048
SWIGLU_LIMIT = 7.0
SWIGLU_ALPHA = 1.702
EPS = 1e-6

LANES = 128
SUBLANES = 8
VMEM_LIMIT = 56 * 1024 * 1024

ZB_U, ZB_V, ZB_Q, ZB_K, ZB_VM, ZB_O = range(6)
ZG_A, ZG_M = range(2)
N_MAIN = 6 * D_MODEL
GATE_OFF = N_MAIN + 2 * HEADS

NORM_TM = 512
INPROJ_TM = 1024
INPROJ_TN = 1024
INPROJ_RC = 512
SGU_TM = 256
MLSTM_L = 256
MERGE_TM = 256
MERGE_RC = 256
ROUTE_TM = 256
DISPATCH_TM = 256
MOE_SUB = 128
MOE_CAP = 1280
MOE_SUBS = MOE_CAP // MOE_SUB
MOE_TF = 512
MOE_NF = D_FF // MOE_TF
MOE_TN = 512
MOE_NN = D_MODEL // MOE_TN
COMBINE_TM = 128


def _cparams(sem, vmem_limit=VMEM_LIMIT):
    return pltpu.CompilerParams(dimension_semantics=sem, vmem_limit_bytes=vmem_limit)


def _gelu(x):
    return 0.5 * x * (1.0 + lax.erf(x * (2.0 ** -0.5)))


def _sigmoid(x):
    return 1.0 / (1.0 + jnp.exp(-x))


def _norm_gates_kernel(x_ref, g_ref, wif_ref, bif_ref, h_ref, gates_ref):
    x = x_ref[...]
    ms = jnp.mean(x * x, axis=-1, keepdims=True)
    h = (x * lax.rsqrt(ms + EPS) * g_ref[...]).astype(BF16)
    h_ref[...] = h
    gates_ref[...] = jnp.dot(h, wif_ref[...], preferred_element_type=F32) + bif_ref[...]


def _norm_gates(x, g, wif, bif):
    t = x.shape[0]
    tm = min(NORM_TM, t)
    row = lambda i: (i, 0)
    const = lambda i: (0, 0)
    return pl.pallas_call(
        _norm_gates_kernel,
        grid=(t // tm,),
        in_specs=[
            pl.BlockSpec((tm, D_MODEL), row),
            pl.BlockSpec((1, D_MODEL), const),
            pl.BlockSpec((D_MODEL, LANES), const),
            pl.BlockSpec((1, LANES), const),
        ],
        out_specs=[pl.BlockSpec((tm, D_MODEL), row), pl.BlockSpec((tm, LANES), row)],
        out_shape=[jax.ShapeDtypeStruct((t, D_MODEL), BF16), jax.ShapeDtypeStruct((t, LANES), F32)],
        compiler_params=_cparams(("arbitrary",)),
        name="norm_gates",
    )(x, g, wif, bif)


def _inproj_kernel(h_ref, w_ref, b_ref, z_ref, w_scr, *, n_gelu, n_plain):
    j = pl.program_id(0)

    @pl.when(pl.program_id(1) == 0)
    def _():
        w_scr[...] = w_ref[...].astype(BF16)

    def project(activation):
        tm = h_ref.shape[0]
        rc = min(INPROJ_RC, tm)
        for c in range(tm // rc):
            rows = slice(c * rc, (c + 1) * rc)
            acc = lax.dot_general(h_ref[rows, :], w_scr[...], (((1,), (1,)), ((), ())),
                                  preferred_element_type=F32) + b_ref[...]
            z_ref[rows, :] = activation(acc).astype(z_ref.dtype)

    @pl.when(j < n_gelu)
    def _():
        project(_gelu)

    @pl.when((j >= n_gelu) & (j < n_gelu + n_plain))
    def _():
        project(lambda acc: acc)

    @pl.when(j >= n_gelu + n_plain)
    def _():
        project(_sigmoid)


def _in_proj(h, w_t, b, n_cols, n_gelu_cols, n_plain_cols, name):
    t = h.shape[0]
    tm, tn = min(INPROJ_TM, t), INPROJ_TN
    return pl.pallas_call(
        functools.partial(_inproj_kernel, n_gelu=n_gelu_cols // tn, n_plain=n_plain_cols // tn),
        grid=(n_cols // tn, t // tm),
        in_specs=[
            pl.BlockSpec((tm, D_MODEL), lambda j, i: (i, 0)),
            pl.BlockSpec((tn, D_MODEL), lambda j, i: (j, 0)),
            pl.BlockSpec((1, tn), lambda j, i: (0, j)),
        ],
        out_specs=pl.BlockSpec((tm, tn), lambda j, i: (i, j)),
        out_shape=jax.ShapeDtypeStruct((t, n_cols), BF16),
        scratch_shapes=[pltpu.VMEM((tn, D_MODEL), BF16)],
        compiler_params=_cparams(("arbitrary", "arbitrary")),
        name=name,
    )(h, w_t, b)


def _sgu_kernel(gu_ref, gv_ref, sga_ref, ws_ref, bsb_ref, lng_ref, lnb_ref, pa_ref, out_ref,
                ya_scr):
    tm = gu_ref.shape[0]
    v = gv_ref[...].astype(F32)
    mu = jnp.mean(v, axis=-1, keepdims=True)
    vc = v - mu
    var = jnp.mean(vc * vc, axis=-1, keepdims=True)
    vln = (vc * lax.rsqrt(var + EPS) * lng_ref[...] + lnb_ref[...]).astype(BF16)

    t_id = lax.broadcasted_iota(I32, (SGU_BLOCK, SGU_BLOCK), 0) // CHUNK
    s_id = lax.broadcasted_iota(I32, (SGU_BLOCK, SGU_BLOCK), 1) // CHUNK
    causal = s_id <= t_id
    for g in range(SGU_GROUPS):
        w = jnp.where(causal, ws_ref[g], 0.0).astype(BF16)
        cols = slice(g * SGU_GROUP_DIM, (g + 1) * SGU_GROUP_DIM)
        bias = bsb_ref[:, cols]
        for blk in range(tm // SGU_BLOCK):
            rows = slice(blk * SGU_BLOCK, (blk + 1) * SGU_BLOCK)
            mixed = jnp.dot(w, vln[rows, cols], preferred_element_type=F32) + bias
            ya_scr[rows, cols] = (gu_ref[rows, cols].astype(F32) * mixed).astype(BF16)

    proj = jnp.dot(ya_scr[...], pa_ref[...], preferred_element_type=F32)
    out_ref[...] = (sga_ref[...].astype(F32) * proj).astype(out_ref.dtype)


def _sgu(zm, zg, ws, bsb, lng, lnb, pa):
    t = zm.shape[0]
    tm = SGU_TM
    const2 = lambda i: (0, 0)
    return pl.pallas_call(
        _sgu_kernel,
        grid=(t // tm,),
        in_specs=[
            pl.BlockSpec((tm, D_MODEL), lambda i: (i, ZB_U)),
            pl.BlockSpec((tm, D_MODEL), lambda i: (i, ZB_V)),
            pl.BlockSpec((tm, D_MODEL), lambda i: (i, ZG_A)),
            pl.BlockSpec((SGU_GROUPS, SGU_BLOCK, SGU_BLOCK), lambda i: (0, 0, 0)),
            pl.BlockSpec((SGU_BLOCK, D_MODEL), const2),
            pl.BlockSpec((1, D_MODEL), const2),
            pl.BlockSpec((1, D_MODEL), const2),
            pl.BlockSpec((D_MODEL, D_MODEL), const2),
        ],
        out_specs=pl.BlockSpec((tm, D_MODEL), lambda i: (i, 0)),
        out_shape=jax.ShapeDtypeStruct((t, D_MODEL), BF16),
        scratch_shapes=[pltpu.VMEM((tm, D_MODEL), BF16)],
        compiler_params=_cparams(("arbitrary",)),
        name="sgu",
    )(zm, zm, zg, ws, bsb, lng, lnb, pa)


def _mlstm_kernel(q_ref, k_ref, v_ref, so_ref, gates_ref, wc_ref, bc_ref, gn_ref, out_ref,
                  hist, ct_scr, n_scr, m_scr):
    L = q_ref.shape[0]
    c = pl.program_id(1)

    @pl.when(c == 0)
    def _():
        hist[0:SUBLANES, :] = jnp.zeros((SUBLANES, 2 * D_MODEL), F32)
        ct_scr[...] = jnp.zeros_like(ct_scr)
        n_scr[...] = jnp.zeros_like(n_scr)
        m_scr[...] = jnp.zeros_like(m_scr)

    hist[SUBLANES:SUBLANES + L, 0:D_MODEL] = q_ref[...].astype(F32)
    hist[SUBLANES:SUBLANES + L, D_MODEL:2 * D_MODEL] = k_ref[...].astype(F32)
    qk = bc_ref[...] + wc_ref[CONV_WIDTH - 1:CONV_WIDTH, :] * hist[SUBLANES:SUBLANES + L, :]
    for j in range(CONV_WIDTH - 1):
        off = SUBLANES - (CONV_WIDTH - 1) + j
        qk = qk + wc_ref[j:j + 1, :] * hist[off:off + L, :]
    qk = qk * _sigmoid(qk)
    hist[0:SUBLANES, :] = hist[L:L + SUBLANES, :]

    gts = gates_ref[...]
    lf = jnp.minimum(gts, 0.0) - jnp.log(1.0 + jnp.exp(-jnp.abs(gts)))
    row = lax.broadcasted_iota(I32, (L, L), 0)
    col = lax.broadcasted_iota(I32, (L, L), 1)
    tril = col <= row
    tri = jnp.where(tril, 1.0, 0.0).astype(BF16)
    lf_hi = lf.astype(BF16)
    lf_lo = (lf - lf_hi.astype(F32)).astype(BF16)
    bcum = (jnp.dot(tri, lf_hi, preferred_element_type=F32)
            + jnp.dot(tri, lf_lo, preferred_element_type=F32))
    bcum_t = bcum.T
    gts_t = gts.T

    scale = HEAD_DIM ** -0.5
    for h in range(HEADS):
        cols = slice(h * HEAD_DIM, (h + 1) * HEAD_DIM)
        q = qk[:, h * HEAD_DIM:(h + 1) * HEAD_DIM] * scale
        k = qk[:, D_MODEL + h * HEAD_DIM:D_MODEL + (h + 1) * HEAD_DIM]
        qb = q.astype(BF16)
        kb = k.astype(BF16)
        vb = v_ref[:, cols]
        bc = bcum[:, HEADS + h:HEADS + h + 1]
        bc_r = bcum_t[HEADS + h:HEADS + h + 1, :]
        ig_c = gts[:, h:h + 1]
        ig_r = gts_t[h:h + 1, :]
        m_st = m_scr[h]
        n_st = n_scr[h]

        dlog = jnp.where(tril, bc - bc_r + ig_r, -jnp.inf)
        m_inter = bc + m_st
        m_t = jnp.maximum(m_inter, jnp.max(dlog, axis=-1, keepdims=True))
        p = jnp.exp(dlog - m_t)
        s = lax.dot_general(qb, kb, (((1,), (1,)), ((), ())), preferred_element_type=F32)
        sc = s * p
        inter = jnp.exp(m_inter - m_t)
        ctb = ct_scr[h].astype(BF16)
        num = (jnp.dot(sc.astype(BF16), vb, preferred_element_type=F32)
               + inter * jnp.dot(qb, ctb, preferred_element_type=F32))
        den = (jnp.sum(sc, axis=-1, keepdims=True)
               + inter * jnp.sum(q * n_st, axis=-1, keepdims=True))
        hh = num / jnp.maximum(jnp.abs(den), jnp.exp(-m_t))
        hn = hh * lax.rsqrt(jnp.mean(hh * hh, axis=-1, keepdims=True) + EPS) * gn_ref[:, cols]
        out_ref[:, cols] = (hn * so_ref[:, cols].astype(F32)).astype(out_ref.dtype)

        g_tot = bc[L - 1:L, :]
        a = g_tot - bc + ig_c
        m_new = jnp.maximum(g_tot + m_st, jnp.max(a, axis=0, keepdims=True))
        wa = jnp.exp(a - m_new)
        decay = jnp.exp(g_tot + m_st - m_new)
        wv = (wa * vb.astype(F32)).astype(BF16)
        upd = jnp.dot(k.T.astype(BF16), wv, preferred_element_type=F32)
        ct_scr[h] = decay * ct_scr[h] + upd
        n_scr[h] = decay * n_st + jnp.sum(wa * k, axis=0, keepdims=True)
        m_scr[h] = m_new


def _mlstm(zm, gates, wc, bc, gn, bsz, seq):
    t = zm.shape[0]
    L = MLSTM_L
    nc = seq // L
    zspec = lambda blk: pl.BlockSpec((L, D_MODEL), lambda b, c: (b * nc + c, blk))
    const2 = lambda b, c: (0, 0)
    return pl.pallas_call(
        _mlstm_kernel,
        grid=(bsz, nc),
        in_specs=[
            zspec(ZB_Q), zspec(ZB_K), zspec(ZB_VM), zspec(ZB_O),
            pl.BlockSpec((L, LANES), lambda b, c: (b * nc + c, 0)),
            pl.BlockSpec((CONV_WIDTH, 2 * D_MODEL), const2),
            pl.BlockSpec((1, 2 * D_MODEL), const2),
            pl.BlockSpec((1, D_MODEL), const2),
        ],
        out_specs=pl.BlockSpec((L, D_MODEL), lambda b, c: (b * nc + c, 0)),
        out_shape=jax.ShapeDtypeStruct((t, D_MODEL), BF16),
        scratch_shapes=[
            pltpu.VMEM((L + 2 * SUBLANES, 2 * D_MODEL), F32),
            pltpu.VMEM((HEADS, HEAD_DIM, HEAD_DIM), F32),
            pltpu.VMEM((HEADS, 1, HEAD_DIM), F32),
            pltpu.VMEM((HEADS, 1, 1), F32),
        ],
        compiler_params=_cparams(("arbitrary", "arbitrary")),
        name="mlstm",
    )(zm, zm, zm, zm, gates, wc, bc, gn)


def _merge_kernel(ym_ref, a_ref, sgm_ref, x_ref, pm_ref, wo_ref, g2_ref, wrh_ref, wrl_ref, br_ref,
                  x1_ref, h2_ref, lg_ref):
    tm = x_ref.shape[0]
    rc = min(MERGE_RC, tm)
    for c in range(tm // rc):
        rows = slice(c * rc, (c + 1) * rc)
        proj = jnp.dot(ym_ref[rows, :], pm_ref[...], preferred_element_type=F32)
        merged = a_ref[rows, :].astype(F32) + sgm_ref[rows, :].astype(F32) * proj
        x1 = x_ref[rows, :] + jnp.dot(merged.astype(BF16), wo_ref[...],
                                      preferred_element_type=F32)
        x1_ref[rows, :] = x1
        h2 = x1 * lax.rsqrt(jnp.mean(x1 * x1, axis=-1, keepdims=True) + EPS) * g2_ref[...]
        h2_ref[rows, :] = h2
        h_hi = h2.astype(BF16)
        h_lo = (h2 - h_hi.astype(F32)).astype(BF16)
        lg_ref[rows, :] = (jnp.dot(h_hi, wrh_ref[...], preferred_element_type=F32)
                           + (jnp.dot(h_lo, wrh_ref[...], preferred_element_type=F32)
                              + jnp.dot(h_hi, wrl_ref[...], preferred_element_type=F32))
                           + br_ref[...])


def _merge(ym, a, zg, x, pm, wo, g2, wr_hi, wr_lo, br):
    t = x.shape[0]
    tm = MERGE_TM
    const2 = lambda i: (0, 0)
    row = lambda i: (i, 0)
    return pl.pallas_call(
        _merge_kernel,
        grid=(t // tm,),
        in_specs=[
            pl.BlockSpec((tm, D_MODEL), row),
            pl.BlockSpec((tm, D_MODEL), row),
            pl.BlockSpec((tm, D_MODEL), lambda i: (i, ZG_M)),
            pl.BlockSpec((tm, D_MODEL), row),
            pl.BlockSpec((D_MODEL, D_MODEL), const2),
            pl.BlockSpec((D_MODEL, D_MODEL), const2),
            pl.BlockSpec((1, D_MODEL), const2),
            pl.BlockSpec((D_MODEL, LANES), const2),
            pl.BlockSpec((D_MODEL, LANES), const2),
            pl.BlockSpec((1, LANES), const2),
        ],
        out_specs=[
            pl.BlockSpec((tm, D_MODEL), row),
            pl.BlockSpec((tm, D_MODEL), row),
            pl.BlockSpec((tm, LANES), row),
        ],
        out_shape=[
            jax.ShapeDtypeStruct((t, D_MODEL), F32),
            jax.ShapeDtypeStruct((t, D_MODEL), F32),
            jax.ShapeDtypeStruct((t, LANES), F32),
        ],
        compiler_params=_cparams(("arbitrary",)),
        name="merge",
    )(ym, a, zg, x, pm, wo, g2, wr_hi, wr_lo, br)


def _route_kernel(lg_ref, info_ref, gate_ref, cnt_ref, cnt_scr):
    tm = lg_ref.shape[0]
    i = pl.program_id(0)

    @pl.when(i == 0)
    def _():
        cnt_scr[...] = jnp.zeros_like(cnt_scr)

    lane = lax.broadcasted_iota(I32, (tm, LANES), 1)
    lane_f = lane.astype(F32)
    vals = jnp.where(lane < N_EXPERTS, lg_ref[...], -jnp.inf)
    tops, ids, hots = [], [], []
    for _ in range(TOP_K):
        m = jnp.max(vals, axis=-1, keepdims=True)
        idx_f = jnp.min(jnp.where(vals == m, lane_f, float(LANES)), axis=-1, keepdims=True)
        idx = idx_f.astype(I32)
        hot = lane == idx
        tops.append(m)
        ids.append(idx)
        hots.append(hot)
        vals = jnp.where(hot, -jnp.inf, vals)

    exps = [jnp.exp(tv - tops[0]) for tv in tops]
    denom = exps[0] + exps[1] + exps[2] + exps[3]

    sel = jnp.zeros((tm, LANES), F32)
    for hot in hots:
        sel = sel + jnp.where(hot, 1.0, 0.0)
    r = lax.broadcasted_iota(I32, (tm, tm), 0)
    c = lax.broadcasted_iota(I32, (tm, tm), 1)
    strict = jnp.where(c < r, 1.0, 0.0).astype(BF16)
    ahead = jnp.dot(strict, sel.astype(BF16), preferred_element_type=F32) + cnt_scr[...]

    info = jnp.zeros((tm, LANES), I32)
    gate = jnp.zeros((tm, LANES), F32)
    for k in range(TOP_K):
        pos = jnp.sum(jnp.where(hots[k], ahead, 0.0), axis=-1, keepdims=True).astype(I32)
        info = jnp.where(lane == k, ids[k], info)
        info = jnp.where(lane == TOP_K + k, pos, info)
        gate = jnp.where(lane == k, exps[k] / denom, gate)
    info_ref[...] = info
    gate_ref[...] = gate
    cnt_scr[...] = cnt_scr[...] + jnp.sum(sel, axis=0, keepdims=True)
    cnt_ref[...] = jnp.broadcast_to(cnt_scr[...], cnt_ref.shape)


def _route(logits):
    t = logits.shape[0]
    tm = ROUTE_TM
    row = lambda i: (i, 0)
    return pl.pallas_call(
        _route_kernel,
        grid=(t // tm,),
        in_specs=[pl.BlockSpec((tm, LANES), row)],
        out_specs=[
            pl.BlockSpec((tm, LANES), row),
            pl.BlockSpec((tm, LANES), row),
            pl.BlockSpec((SUBLANES, LANES), lambda i: (0, 0)),
        ],
        out_shape=[
            jax.ShapeDtypeStruct((t, LANES), I32),
            jax.ShapeDtypeStruct((t, LANES), F32),
            jax.ShapeDtypeStruct((SUBLANES, LANES), F32),
        ],
        scratch_shapes=[pltpu.VMEM((1, LANES), F32)],
        compiler_params=_cparams(("arbitrary",)),
        name="route",
    )(logits)


def _grouped_row(route_ref, start_ref, token, k):
    base = token * (2 * TOP_K)
    return start_ref[route_ref[base + k]] + route_ref[base + TOP_K + k]


def _dispatch_kernel(route_ref, start_ref, zero_row_ref, h2_ref, xs_hbm, zbuf, sem, zsem, *,
                     n_zero):
    tm = h2_ref.shape[0]
    i = pl.program_id(0)

    def zero_copy(j):
        row0 = pl.multiple_of(zero_row_ref[j], MOE_SUB)
        return pltpu.make_async_copy(zbuf, xs_hbm.at[pl.ds(row0, MOE_SUB)], zsem)

    @pl.when(i == 0)
    def _():
        zbuf[...] = jnp.zeros_like(zbuf)

        def start(j, carry):
            @pl.when(zero_row_ref[j] >= 0)
            def _():
                zero_copy(j).start()
            return carry

        def wait(j, carry):
            @pl.when(zero_row_ref[j] >= 0)
            def _():
                zero_copy(j).wait()
            return carry

        lax.fori_loop(0, n_zero, start, 0)
        lax.fori_loop(0, n_zero, wait, 0)

    for r in range(tm):
        for k in range(TOP_K):
            d = _grouped_row(route_ref, start_ref, i * tm + r, k)
            pltpu.make_async_copy(h2_ref.at[pl.ds(r, 1)], xs_hbm.at[pl.ds(d, 1)],
                                  sem).start(priority=k % 2)

    for _ in range(TOP_K):
        pltpu.make_async_copy(h2_ref, xs_hbm.at[pl.ds(0, tm)], sem).wait()


def _dispatch(route, starts, zero_rows, h2, n_rows):
    t = h2.shape[0]
    tm = DISPATCH_TM
    grid_spec = pltpu.PrefetchScalarGridSpec(
        num_scalar_prefetch=3,
        grid=(t // tm,),
        in_specs=[pl.BlockSpec((tm, D_MODEL), lambda i, r, s, z: (i, 0))],
        out_specs=pl.BlockSpec(memory_space=pl.ANY),
        scratch_shapes=[
            pltpu.VMEM((MOE_SUB, D_MODEL), F32),
            pltpu.SemaphoreType.DMA,
            pltpu.SemaphoreType.DMA,
        ],
    )
    return pl.pallas_call(
        functools.partial(_dispatch_kernel, n_zero=zero_rows.shape[0]),
        grid_spec=grid_spec,
        out_shape=jax.ShapeDtypeStruct((n_rows, D_MODEL), F32),
        compiler_params=_cparams(("arbitrary",)),
        name="dispatch",
    )(route, starts, zero_rows, h2)


def _experts_kernel(tile_e_ref, tile_occ_ref, tile_row_ref,
                    xs_hbm, wg_ref, wu_ref, bg_ref, bu_ref, wd_ref, bd_ref, out_ref,
                    xbuf, xb_scr, act_scr, sem, *, n_tiles):
    del tile_e_ref
    i = pl.program_id(0)
    j = pl.program_id(1)
    occ = tile_occ_ref[i]

    def sub_rows(sub):
        return pl.ds(pl.multiple_of(sub * MOE_SUB, MOE_SUB), MOE_SUB)

    def sub_copy(tile, sub):
        row0 = pl.multiple_of(tile_row_ref[tile] + sub * MOE_SUB, MOE_SUB)
        return pltpu.make_async_copy(xs_hbm.at[pl.ds(row0, MOE_SUB)], xbuf.at[sub_rows(sub)], sem)

    def start_load(tile):
        def body(sub, carry):
            sub_copy(tile, sub).start()
            return carry
        lax.fori_loop(0, tile_occ_ref[tile], body, 0)

    @pl.when(j == 0)
    def _():
        @pl.when(i == 0)
        def _():
            start_load(0)

        def wait_sub(sub, carry):
            sub_copy(i, sub).wait()
            return carry

        def round_sub(sub, carry):
            xb_scr[sub_rows(sub), :] = xbuf[sub_rows(sub), :].astype(BF16)
            return carry

        lax.fori_loop(0, occ, wait_sub, 0)
        lax.fori_loop(0, occ, round_sub, 0)

        @pl.when(i + 1 < n_tiles)
        def _():
            start_load(i + 1)

    def for_each_sub(weights, fn):
        def quad(p, carry):
            w = weights()
            for u in range(4):
                fn(4 * p + u, *w)
            return carry

        lax.fori_loop(0, occ // 4, quad, 0)
        done = (occ // 4) * 4

        @pl.when(occ % 4 >= 2)
        def _():
            w = weights()
            fn(done, *w)
            fn(done + 1, *w)

        @pl.when(occ % 2 == 1)
        def _():
            fn(occ - 1, *weights())

    @pl.when(j < MOE_NF)
    def _():
        def weights():
            return wg_ref[...].astype(BF16), wu_ref[...].astype(BF16)

        def gate_up(sub, wg, wu):
            rows = sub_rows(sub)
            xb = xb_scr[rows, :]
            gate = jnp.dot(xb, wg, preferred_element_type=F32) + bg_ref[...]
            up = jnp.dot(xb, wu, preferred_element_type=F32) + bu_ref[...]
            gate = jnp.minimum(gate, SWIGLU_LIMIT)
            up = jnp.clip(up, -SWIGLU_LIMIT, SWIGLU_LIMIT)
            act = (up + 1.0) * (gate * _sigmoid(SWIGLU_ALPHA * gate))
            act_scr[j, rows, :] = act.astype(BF16)

        for_each_sub(weights, gate_up)

    @pl.when(j >= MOE_NF)
    def _():
        def weights():
            return (wd_ref[...].astype(BF16),)

        def down(sub, wd):
            rows = sub_rows(sub)
            y = bd_ref[...] + jnp.dot(act_scr[0, rows, :], wd[0:MOE_TF, :],
                                      preferred_element_type=F32)
            for c in range(1, MOE_NF):
                y = y + jnp.dot(act_scr[c, rows, :], wd[c * MOE_TF:(c + 1) * MOE_TF, :],
                                preferred_element_type=F32)
            out_ref[rows, :] = y

        for_each_sub(weights, down)

    @pl.when(j >= MOE_NF)
    def _():
        def zero_sub(sub, carry):
            out_ref[sub_rows(sub), :] = jnp.zeros((MOE_SUB, MOE_TN), F32)
            return carry

        lax.fori_loop(occ, MOE_SUBS, zero_sub, 0)


def _experts(tile_e, tile_occ, tile_row, xs, wgu, bgu, wd, bd, n_tiles):
    up_off = D_FF // MOE_TF

    def fa(i, j, to):
        return jnp.where(to[i] > 0, jnp.minimum(j, MOE_NF - 1), MOE_NF - 1)

    def fb(i, j, to):
        return jnp.where(to[i] > 0, jnp.maximum(j - MOE_NF, 0), MOE_NN - 1)

    def gate_map(i, j, te, to, tr):
        return (te[i], 0, fa(i, j, to))

    def up_map(i, j, te, to, tr):
        return (te[i], 0, up_off + fa(i, j, to))

    def down_map(i, j, te, to, tr):
        prev = jnp.maximum(i - 1, 0)
        hold = (j < MOE_NF) & (i > 0) & (to[i] > 0)
        return (jnp.where(hold, te[prev], te[i]), 0, jnp.where(hold, MOE_NN - 1, fb(i, j, to)))

    def out_map(i, j, te, to, tr):
        return (i, jnp.maximum(j - MOE_NF, 0))

    grid_spec = pltpu.PrefetchScalarGridSpec(
        num_scalar_prefetch=3,
        grid=(n_tiles, MOE_NF + MOE_NN),
        in_specs=[
            pl.BlockSpec(memory_space=pl.ANY),
            pl.BlockSpec((None, D_MODEL, MOE_TF), gate_map),
            pl.BlockSpec((None, D_MODEL, MOE_TF), up_map),
            pl.BlockSpec((None, 1, MOE_TF), gate_map),
            pl.BlockSpec((None, 1, MOE_TF), up_map),
            pl.BlockSpec((None, D_FF, MOE_TN), down_map),
            pl.BlockSpec((None, 1, MOE_TN), down_map),
        ],
        out_specs=pl.BlockSpec((MOE_CAP, MOE_TN), out_map),
        scratch_shapes=[
            pltpu.VMEM((MOE_CAP, D_MODEL), F32),
            pltpu.VMEM((MOE_CAP, D_MODEL), BF16),
            pltpu.VMEM((MOE_NF, MOE_CAP, MOE_TF), BF16),
            pltpu.SemaphoreType.DMA,
        ],
    )
    return pl.pallas_call(
        functools.partial(_experts_kernel, n_tiles=n_tiles),
        grid_spec=grid_spec,
        out_shape=jax.ShapeDtypeStruct((n_tiles * MOE_CAP, D_MODEL), F32),
        compiler_params=_cparams(("arbitrary", "arbitrary")),
        name="experts",
    )(tile_e, tile_occ, tile_row, xs, wgu, wgu, bgu, bgu, wd, bd)


def _combine_kernel(route_ref, start_ref, ys_hbm, x1_ref, gate_ref, gf_ref, out_ref, buf, sem, *,
                    n_tiles):
    tm = x1_ref.shape[0]
    i = pl.program_id(0)
    slot = i % 2

    def start_gather(tile, s):
        for r in range(tm):
            for k in range(TOP_K):
                d = _grouped_row(route_ref, start_ref, tile * tm + r, k)
                pltpu.make_async_copy(ys_hbm.at[pl.ds(d, 1)], buf.at[s, k, pl.ds(r, 1)],
                                      sem.at[s]).start(priority=k % 2)

    @pl.when(i == 0)
    def _():
        start_gather(0, 0)

    for k in range(TOP_K):
        pltpu.make_async_copy(ys_hbm.at[pl.ds(0, tm)], buf.at[slot, k], sem.at[slot]).wait()

    for s in range(2):
        @pl.when((i + 1 < n_tiles) & (slot == 1 - s))
        def _():
            start_gather(i + 1, s)

    x2 = x1_ref[...]
    for k in range(TOP_K):
        x2 = x2 + gate_ref[:, k:k + 1] * buf[slot, k]
    out_ref[...] = x2 * lax.rsqrt(jnp.mean(x2 * x2, axis=-1, keepdims=True) + EPS) * gf_ref[...]


def _combine(route, starts, ys, x1, gate, gf):
    t = x1.shape[0]
    tm = COMBINE_TM
    n_tiles = t // tm
    grid_spec = pltpu.PrefetchScalarGridSpec(
        num_scalar_prefetch=2,
        grid=(n_tiles,),
        in_specs=[
            pl.BlockSpec(memory_space=pl.ANY),
            pl.BlockSpec((tm, D_MODEL), lambda i, r, s: (i, 0)),
            pl.BlockSpec((tm, LANES), lambda i, r, s: (i, 0)),
            pl.BlockSpec((1, D_MODEL), lambda i, r, s: (0, 0)),
        ],
        out_specs=pl.BlockSpec((tm, D_MODEL), lambda i, r, s: (i, 0)),
        scratch_shapes=[
            pltpu.VMEM((2, TOP_K, tm, D_MODEL), F32),
            pltpu.SemaphoreType.DMA((2,)),
        ],
    )
    return pl.pallas_call(
        functools.partial(_combine_kernel, n_tiles=n_tiles),
        grid_spec=grid_spec,
        out_shape=jax.ShapeDtypeStruct((t, D_MODEL), F32),
        compiler_params=_cparams(("arbitrary",)),
        name="combine",
    )(route, starts, ys, x1, gate, gf)


def _routing_tables(counts, n_tokens):
    n_tiles = N_EXPERTS + (n_tokens * TOP_K) // MOE_CAP
    n_subs = N_EXPERTS + (n_tokens * TOP_K) // MOE_SUB
    subs_per = (counts + MOE_SUB - 1) // MOE_SUB
    sub_end = jnp.cumsum(subs_per)
    x_start = (sub_end - subs_per) * MOE_SUB
    tiles_per = (counts + MOE_CAP - 1) // MOE_CAP
    tile_end = jnp.cumsum(tiles_per)
    tile_start = tile_end - tiles_per
    y_start = tile_start * MOE_CAP

    tile_ids = jnp.arange(n_tiles, dtype=I32)
    n_active = tile_end[-1]
    last = jnp.maximum(n_active - 1, 0)
    active = tile_ids < n_active
    tile_eff = jnp.minimum(tile_ids, last)
    tile_e = jnp.sum((tile_eff[:, None] >= tile_end[None, :]).astype(I32), axis=1)
    tile_e = jnp.minimum(tile_e, N_EXPERTS - 1)
    in_expert = tile_eff - tile_start[tile_e]
    rows_in = counts[tile_e] - in_expert * MOE_CAP
    occ = jnp.clip((rows_in + MOE_SUB - 1) // MOE_SUB, 0, MOE_SUBS)
    tile_occ = jnp.where(active, occ, 0).astype(I32)
    tile_row = (x_start[tile_e] + in_expert * MOE_CAP).astype(I32)

    last_sub = jnp.where(counts > 0, x_start + (subs_per - 1) * MOE_SUB, -1)
    tail_ids = sub_end[-1] + jnp.arange(N_EXPERTS, dtype=I32)
    tail = jnp.where(tail_ids < n_subs, tail_ids * MOE_SUB, -1)
    zero_rows = jnp.concatenate([last_sub, tail]).astype(I32)
    return (n_tiles, n_subs * MOE_SUB, tile_e.astype(I32), tile_occ, tile_row,
            x_start, y_start, zero_rows)


def _layer(x2d, bsz, seq, g_norm_mix, w_in, b_in, w_conv, b_conv, w_sgu, b_sgu, g_sgu_ln, b_sgu_ln,
           g_mlstm_norm, w_proj_sgu, w_proj_mlstm, w_out, g_norm_moe, w_router, b_router,
           w_gate_up, b_gate_up, w_down, b_down, g_out):
    t = x2d.shape[0]

    pad = LANES - 2 * HEADS
    w_if = jnp.pad(w_in[:, N_MAIN:GATE_OFF], ((0, 0), (0, pad))).astype(BF16)
    b_if = jnp.pad(b_in[N_MAIN:GATE_OFF], (0, pad))[None, :]
    w_in_t = w_in.T
    w_mg_t = w_in_t[GATE_OFF:]
    bsb = jnp.broadcast_to(b_sgu.T[:, :, None], (SGU_BLOCK, SGU_GROUPS, SGU_GROUP_DIM))
    bsb = bsb.reshape(SGU_BLOCK, D_MODEL)
    w_r = jnp.pad(w_router, ((0, 0), (0, LANES - N_EXPERTS)))
    w_r_hi = w_r.astype(BF16)
    w_r_lo = (w_r - w_r_hi.astype(F32)).astype(BF16)
    b_r = jnp.pad(b_router, (0, LANES - N_EXPERTS))[None, :]

    h, gates = _norm_gates(x2d, g_norm_mix[None, :], w_if, b_if)
    zm = _in_proj(h, w_in_t, b_in[None, :], N_MAIN, 2 * D_MODEL, 3 * D_MODEL, "in_proj_main")
    zg = _in_proj(h, w_mg_t, b_in[None, GATE_OFF:], 2 * D_MODEL, 0, 0, "in_proj_gate")
    a = _sgu(zm, zg, w_sgu, bsb, g_sgu_ln[None, :], b_sgu_ln[None, :], w_proj_sgu.astype(BF16))
    ym = _mlstm(zm, gates, w_conv, b_conv[None, :], g_mlstm_norm[None, :], bsz, seq)
    x1, h2, logits = _merge(ym, a, zg, x2d, w_proj_mlstm.astype(BF16), w_out.astype(BF16),
                            g_norm_moe[None, :], w_r_hi, w_r_lo, b_r)
    info, gate, cnt = _route(logits)

    counts = cnt[0, :N_EXPERTS].astype(I32)
    (n_tiles, n_xrows, tile_e, tile_occ, tile_row, x_start, y_start,
     zero_rows) = _routing_tables(counts, t)
    route = info[:, :2 * TOP_K].reshape(-1)

    xs = _dispatch(route, x_start, zero_rows, h2, n_xrows)
    ys = _experts(tile_e, tile_occ, tile_row, xs, w_gate_up, b_gate_up[:, None, :],
                  w_down, b_down[:, None, :], n_tiles)
    return _combine(route, y_start, ys, x1, gate, g_out[None, :])


def kernel(x, g_norm_mix, w_in, b_in, w_conv, b_conv, w_sgu, b_sgu, g_sgu_ln, b_sgu_ln, g_mlstm_norm,
           w_proj_sgu, w_proj_mlstm, w_out, g_norm_moe, w_router, b_router, w_gate_up, b_gate_up,
           w_down, b_down, g_final):
    bsz, seq, d = x.shape
    assert d == D_MODEL and w_in.shape[0] == 1, "single-layer block with d_model 2048"
    out = _layer(x.reshape(bsz * seq, d), bsz, seq, g_norm_mix[0], w_in[0], b_in[0], w_conv[0],
                 b_conv[0], w_sgu[0], b_sgu[0], g_sgu_ln[0], b_sgu_ln[0], g_mlstm_norm[0],
                 w_proj_sgu[0], w_proj_mlstm[0], w_out[0], g_norm_moe[0], w_router[0], b_router[0],
                 w_gate_up[0], b_gate_up[0], w_down[0], b_down[0], g_final)
    return out.reshape(bsz, seq, d)
```

```python
import functools

import jax
import jax.numpy as jnp
from jax import lax
from jax.experimental import pallas as pl
from jax.experimental.pallas import tpu as pltpu

F32 = jnp.float32
BF16 = jnp.bfloat16
I32 = jnp.int32

D_MODEL = 2048
CHUNK = 64
SGU_BLOCK = 128
SGU_GROUPS = 8
SGU_GROUP_DIM = 256
HEADS = 4
HEAD_DIM = 512
CONV_WIDTH = 4
N_EXPERTS = 32
TOP_K = 4
D_FF = 2048
SWIGLU_LIMIT = 7.0
SWIGLU_ALPHA = 1.702
EPS = 1e-6

LANES = 128
SUBLANES = 8
VMEM_LIMIT = 56 * 1024 * 1024

ZB_U, ZB_V, ZB_Q, ZB_K, ZB_VM, ZB_O = range(6)
ZG_A, ZG_M = range(2)
N_MAIN = 6 * D_MODEL
GATE_OFF = N_MAIN + 2 * HEADS

NORM_TM = 512
INPROJ_TM = 1024
INPROJ_TN = 1024
INPROJ_RC = 512
SGU_TM = 256
MLSTM_L = 256
MERGE_TM = 256
MERGE_RC = 256
ROUTE_TM = 256
DISPATCH_TM = 256
MOE_SUB = 128
MOE_CAP = 1280
MOE_SUBS = MOE_CAP // MOE_SUB
MOE_TF = 512
MOE_NF = D_FF // MOE_TF
MOE_TN = 512
MOE_NN = D_MODEL // MOE_TN
COMBINE_TM = 128


def _cparams(sem, vmem_limit=VMEM_LIMIT):
    return pltpu.CompilerParams(dimension_semantics=sem, vmem_limit_bytes=vmem_limit)


def _gelu(x):
    return 0.5 * x * (1.0 + lax.erf(x * (2.0 ** -0.5)))


def _sigmoid(x):
    return 1.0 / (1.0 + jnp.exp(-x))


def _norm_gates_kernel(x_ref, g_ref, wif_ref, bif_ref, h_ref, gates_ref):
    x = x_ref[...]
    ms = jnp.mean(x * x, axis=-1, keepdims=True)
    h = (x * lax.rsqrt(ms + EPS) * g_ref[...]).astype(BF16)
    h_ref[...] = h
    gates_ref[...] = jnp.dot(h, wif_ref[...], preferred_element_type=F32) + bif_ref[...]


def _norm_gates(x, g, wif, bif):
    t = x.shape[0]
    tm = min(NORM_TM, t)
    row = lambda i: (i, 0)
    const = lambda i: (0, 0)
    return pl.pallas_call(
        _norm_gates_kernel,
        grid=(t // tm,),
        in_specs=[
            pl.BlockSpec((tm, D_MODEL), row),
            pl.BlockSpec((1, D_MODEL), const),
            pl.BlockSpec((D_MODEL, LANES), const),
            pl.BlockSpec((1, LANES), const),
        ],
        out_specs=[pl.BlockSpec((tm, D_MODEL), row), pl.BlockSpec((tm, LANES), row)],
        out_shape=[jax.ShapeDtypeStruct((t, D_MODEL), BF16), jax.ShapeDtypeStruct((t, LANES), F32)],
        compiler_params=_cparams(("arbitrary",)),
        name="norm_gates",
    )(x, g, wif, bif)


def _inproj_kernel(h_ref, w_ref, b_ref, z_ref, w_scr, *, n_gelu, n_plain):
    j = pl.program_id(0)

    @pl.when(pl.program_id(1) == 0)
    def _():
        w_scr[...] = w_ref[...].astype(BF16)

    def project(activation):
        tm = h_ref.shape[0]
        rc = min(INPROJ_RC, tm)
        for c in range(tm // rc):
            rows = slice(c * rc, (c + 1) * rc)
            acc = lax.dot_general(h_ref[rows, :], w_scr[...], (((1,), (1,)), ((), ())),
                                  preferred_element_type=F32) + b_ref[...]
            z_ref[rows, :] = activation(acc).astype(z_ref.dtype)

    @pl.when(j < n_gelu)
    def _():
        project(_gelu)

    @pl.when((j >= n_gelu) & (j < n_gelu + n_plain))
    def _():
        project(lambda acc: acc)

    @pl.when(j >= n_gelu + n_plain)
    def _():
        project(_sigmoid)


def _in_proj(h, w_t, row_off, b, n_cols, n_gelu_cols, n_plain_cols, name):
    t = h.shape[0]
    tm, tn = min(INPROJ_TM, t), INPROJ_TN
    assert row_off % SUBLANES == 0
    return pl.pallas_call(
        functools.partial(_inproj_kernel, n_gelu=n_gelu_cols // tn, n_plain=n_plain_cols // tn),
        grid=(n_cols // tn, t // tm),
        in_specs=[
            pl.BlockSpec((tm, D_MODEL), lambda j, i: (i, 0)),
            pl.BlockSpec((pl.Element(tn), pl.Element(D_MODEL)),
                         lambda j, i: (pl.multiple_of(row_off + j * tn, SUBLANES), 0)),
            pl.BlockSpec((1, tn), lambda j, i: (0, j)),
        ],
        out_specs=pl.BlockSpec((tm, tn), lambda j, i: (i, j)),
        out_shape=jax.ShapeDtypeStruct((t, n_cols), BF16),
        scratch_shapes=[pltpu.VMEM((tn, D_MODEL), BF16)],
        compiler_params=_cparams(("arbitrary", "arbitrary")),
        name=name,
    )(h, w_t, b)


def _sgu_kernel(gu_ref, gv_ref, sga_ref, ws_ref, bsb_ref, lng_ref, lnb_ref, pa_ref, out_ref,
                ya_scr):
    tm = gu_ref.shape[0]
    v = gv_ref[...].astype(F32)
    mu = jnp.mean(v, axis=-1, keepdims=True)
    vc = v - mu
    var = jnp.mean(vc * vc, axis=-1, keepdims=True)
    vln = (vc * lax.rsqrt(var + EPS) * lng_ref[...] + lnb_ref[...]).astype(BF16)

    t_id = lax.broadcasted_iota(I32, (SGU_BLOCK, SGU_BLOCK), 0) // CHUNK
    s_id = lax.broadcasted_iota(I32, (SGU_BLOCK, SGU_BLOCK), 1) // CHUNK
    causal = s_id <= t_id
    for g in range(SGU_GROUPS):
        w = jnp.where(causal, ws_ref[g], 0.0).astype(BF16)
        cols = slice(g * SGU_GROUP_DIM, (g + 1) * SGU_GROUP_DIM)
        bias = bsb_ref[:, cols]
        for blk in range(tm // SGU_BLOCK):
            rows = slice(blk * SGU_BLOCK, (blk + 1) * SGU_BLOCK)
            mixed = jnp.dot(w, vln[rows, cols], preferred_element_type=F32) + bias
            ya_scr[rows, cols] = (gu_ref[rows, cols].astype(F32) * mixed).astype(BF16)

    proj = jnp.dot(ya_scr[...], pa_ref[...], preferred_element_type=F32)
    out_ref[...] = (sga_ref[...].astype(F32) * proj).astype(out_ref.dtype)


def _sgu(zm, zg, ws, bsb, lng, lnb, pa):
    t = zm.shape[0]
    tm = SGU_TM
    const2 = lambda i: (0, 0)
    return pl.pallas_call(
        _sgu_kernel,
        grid=(t // tm,),
        in_specs=[
            pl.BlockSpec((tm, D_MODEL), lambda i: (i, ZB_U)),
            pl.BlockSpec((tm, D_MODEL), lambda i: (i, ZB_V)),
            pl.BlockSpec((tm, D_MODEL), lambda i: (i, ZG_A)),
            pl.BlockSpec((SGU_GROUPS, SGU_BLOCK, SGU_BLOCK), lambda i: (0, 0, 0)),
            pl.BlockSpec((SGU_BLOCK, D_MODEL), const2),
            pl.BlockSpec((1, D_MODEL), const2),
            pl.BlockSpec((1, D_MODEL), const2),
            pl.BlockSpec((D_MODEL, D_MODEL), const2),
        ],
        out_specs=pl.BlockSpec((tm, D_MODEL), lambda i: (i, 0)),
        out_shape=jax.ShapeDtypeStruct((t, D_MODEL), BF16),
        scratch_shapes=[pltpu.VMEM((tm, D_MODEL), BF16)],
        compiler_params=_cparams(("arbitrary",)),
        name="sgu",
    )(zm, zm, zg, ws, bsb, lng, lnb, pa)


def _mlstm_kernel(q_ref, k_ref, v_ref, so_ref, gates_ref, wc_ref, bc_ref, gn_ref, out_ref,
                  hist, ct_scr, n_scr, m_scr):
    L = q_ref.shape[0]
    c = pl.program_id(1)

    @pl.when(c == 0)
    def _():
        hist[0:SUBLANES, :] = jnp.zeros((SUBLANES, 2 * D_MODEL), F32)
        ct_scr[...] = jnp.zeros_like(ct_scr)
        n_scr[...] = jnp.zeros_like(n_scr)
        m_scr[...] = jnp.zeros_like(m_scr)

    hist[SUBLANES:SUBLANES + L, 0:D_MODEL] = q_ref[...].astype(F32)
    hist[SUBLANES:SUBLANES + L, D_MODEL:2 * D_MODEL] = k_ref[...].astype(F32)
    qk = bc_ref[...] + wc_ref[CONV_WIDTH - 1:CONV_WIDTH, :] * hist[SUBLANES:SUBLANES + L, :]
    for j in range(CONV_WIDTH - 1):
        off = SUBLANES - (CONV_WIDTH - 1) + j
        qk = qk + wc_ref[j:j + 1, :] * hist[off:off + L, :]
    qk = qk * _sigmoid(qk)
    hist[0:SUBLANES, :] = hist[L:L + SUBLANES, :]

    gts = gates_ref[...]
    lf = jnp.minimum(gts, 0.0) - jnp.log(1.0 + jnp.exp(-jnp.abs(gts)))
    row = lax.broadcasted_iota(I32, (L, L), 0)
    col = lax.broadcasted_iota(I32, (L, L), 1)
    tril = col <= row
    tri = jnp.where(tril, 1.0, 0.0).astype(BF16)
    lf_hi = lf.astype(BF16)
    lf_lo = (lf - lf_hi.astype(F32)).astype(BF16)
    bcum = (jnp.dot(tri, lf_hi, preferred_element_type=F32)
            + jnp.dot(tri, lf_lo, preferred_element_type=F32))
    bcum_t = bcum.T
    gts_t = gts.T

    scale = HEAD_DIM ** -0.5
    for h in range(HEADS):
        cols = slice(h * HEAD_DIM, (h + 1) * HEAD_DIM)
        q = qk[:, h * HEAD_DIM:(h + 1) * HEAD_DIM] * scale
        k = qk[:, D_MODEL + h * HEAD_DIM:D_MODEL + (h + 1) * HEAD_DIM]
        qb = q.astype(BF16)
        kb = k.astype(BF16)
        vb = v_ref[:, cols]
        bc = bcum[:, HEADS + h:HEADS + h + 1]
        bc_r = bcum_t[HEADS + h:HEADS + h + 1, :]
        ig_c = gts[:, h:h + 1]
        ig_r = gts_t[h:h + 1, :]
        m_st = m_scr[h]
        n_st = n_scr[h]

        dlog = jnp.where(tril, bc - bc_r + ig_r, -jnp.inf)
        m_inter = bc + m_st
        m_t = jnp.maximum(m_inter, jnp.max(dlog, axis=-1, keepdims=True))
        p = jnp.exp(dlog - m_t)
        s = lax.dot_general(qb, kb, (((1,), (1,)), ((), ())), preferred_element_type=F32)
        sc = s * p
        inter = jnp.exp(m_inter - m_t)
        ctb = ct_scr[h].astype(BF16)
        num = (jnp.dot(sc.astype(BF16), vb, preferred_element_type=F32)
               + inter * jnp.dot(qb, ctb, preferred_element_type=F32))
        den = (jnp.sum(sc, axis=-1, keepdims=True)
               + inter * jnp.sum(q * n_st, axis=-1, keepdims=True))
        hh = num / jnp.maximum(jnp.abs(den), jnp.exp(-m_t))
        hn = hh * lax.rsqrt(jnp.mean(hh * hh, axis=-1, keepdims=True) + EPS) * gn_ref[:, cols]
        out_ref[:, cols] = (hn * so_ref[:, cols].astype(F32)).astype(out_ref.dtype)

        g_tot = bc[L - 1:L, :]
        a = g_tot - bc + ig_c
        m_new = jnp.maximum(g_tot + m_st, jnp.max(a, axis=0, keepdims=True))
        wa = jnp.exp(a - m_new)
        decay = jnp.exp(g_tot + m_st - m_new)
        wv = (wa * vb.astype(F32)).astype(BF16)
        upd = jnp.dot(k.T.astype(BF16), wv, preferred_element_type=F32)
        ct_scr[h] = decay * ct_scr[h] + upd
        n_scr[h] = decay * n_st + jnp.sum(wa * k, axis=0, keepdims=True)
        m_scr[h] = m_new


def _mlstm(zm, gates, wc, bc, gn, bsz, seq):
    t = zm.shape[0]
    L = MLSTM_L
    nc = seq // L
    zspec = lambda blk: pl.BlockSpec((L, D_MODEL), lambda b, c: (b * nc + c, blk))
    const2 = lambda b, c: (0, 0)
    return pl.pallas_call(
        _mlstm_kernel,
        grid=(bsz, nc),
        in_specs=[
            zspec(ZB_Q), zspec(ZB_K), zspec(ZB_VM), zspec(ZB_O),
            pl.BlockSpec((L, LANES), lambda b, c: (b * nc + c, 0)),
            pl.BlockSpec((CONV_WIDTH, 2 * D_MODEL), const2),
            pl.BlockSpec((1, 2 * D_MODEL), const2),
            pl.BlockSpec((1, D_MODEL), const2),
        ],
        out_specs=pl.BlockSpec((L, D_MODEL), lambda b, c: (b * nc + c, 0)),
        out_shape=jax.ShapeDtypeStruct((t, D_MODEL), BF16),
        scratch_shapes=[
            pltpu.VMEM((L + 2 * SUBLANES, 2 * D_MODEL), F32),
            pltpu.VMEM((HEADS, HEAD_DIM, HEAD_DIM), F32),
            pltpu.VMEM((HEADS, 1, HEAD_DIM), F32),
            pltpu.VMEM((HEADS, 1, 1), F32),
        ],
        compiler_params=_cparams(("arbitrary", "arbitrary")),
        name="mlstm",
    )(zm, zm, zm, zm, gates, wc, bc, gn)


def _merge_kernel(ym_ref, a_ref, sgm_ref, x_ref, pm_ref, wo_ref, g2_ref, wrh_ref, wrl_ref, br_ref,
                  x1_ref, h2_ref, lg_ref):
    tm = x_ref.shape[0]
    rc = min(MERGE_RC, tm)
    for c in range(tm // rc):
        rows = slice(c * rc, (c + 1) * rc)
        proj = jnp.dot(ym_ref[rows, :], pm_ref[...], preferred_element_type=F32)
        merged = a_ref[rows, :].astype(F32) + sgm_ref[rows, :].astype(F32) * proj
        x1 = x_ref[rows, :] + jnp.dot(merged.astype(BF16), wo_ref[...],
                                      preferred_element_type=F32)
        x1_ref[rows, :] = x1
        h2 = x1 * lax.rsqrt(jnp.mean(x1 * x1, axis=-1, keepdims=True) + EPS) * g2_ref[...]
        h2_ref[rows, :] = h2
        h_hi = h2.astype(BF16)
        h_lo = (h2 - h_hi.astype(F32)).astype(BF16)
        lg_ref[rows, :] = (jnp.dot(h_hi, wrh_ref[...], preferred_element_type=F32)
                           + (jnp.dot(h_lo, wrh_ref[...], preferred_element_type=F32)
                              + jnp.dot(h_hi, wrl_ref[...], preferred_element_type=F32))
                           + br_ref[...])


def _merge(ym, a, zg, x, pm, wo, g2, wr_hi, wr_lo, br):
    t = x.shape[0]
    tm = MERGE_TM
    const2 = lambda i: (0, 0)
    row = lambda i: (i, 0)
    return pl.pallas_call(
        _merge_kernel,
        grid=(t // tm,),
        in_specs=[
            pl.BlockSpec((tm, D_MODEL), row),
            pl.BlockSpec((tm, D_MODEL), row),
            pl.BlockSpec((tm, D_MODEL), lambda i: (i, ZG_M)),
            pl.BlockSpec((tm, D_MODEL), row),
            pl.BlockSpec((D_MODEL, D_MODEL), const2),
            pl.BlockSpec((D_MODEL, D_MODEL), const2),
            pl.BlockSpec((1, D_MODEL), const2),
            pl.BlockSpec((D_MODEL, LANES), const2),
            pl.BlockSpec((D_MODEL, LANES), const2),
            pl.BlockSpec((1, LANES), const2),
        ],
        out_specs=[
            pl.BlockSpec((tm, D_MODEL), row),
            pl.BlockSpec((tm, D_MODEL), row),
            pl.BlockSpec((tm, LANES), row),
        ],
        out_shape=[
            jax.ShapeDtypeStruct((t, D_MODEL), F32),
            jax.ShapeDtypeStruct((t, D_MODEL), F32),
            jax.ShapeDtypeStruct((t, LANES), F32),
        ],
        compiler_params=_cparams(("arbitrary",)),
        name="merge",
    )(ym, a, zg, x, pm, wo, g2, wr_hi, wr_lo, br)


def _route_kernel(lg_ref, info_ref, gate_ref, cnt_ref, cnt_scr):
    tm = lg_ref.shape[0]
    i = pl.program_id(0)

    @pl.when(i == 0)
    def _():
        cnt_scr[...] = jnp.zeros_like(cnt_scr)

    lane = lax.broadcasted_iota(I32, (tm, LANES), 1)
    lane_f = lane.astype(F32)
    vals = jnp.where(lane < N_EXPERTS, lg_ref[...], -jnp.inf)
    tops, ids, hots = [], [], []
    for _ in range(TOP_K):
        m = jnp.max(vals, axis=-1, keepdims=True)
        idx_f = jnp.min(jnp.where(vals == m, lane_f, float(LANES)), axis=-1, keepdims=True)
        idx = idx_f.astype(I32)
        hot = lane == idx
        tops.append(m)
        ids.append(idx)
        hots.append(hot)
        vals = jnp.where(hot, -jnp.inf, vals)

    exps = [jnp.exp(tv - tops[0]) for tv in tops]
    denom = exps[0] + exps[1] + exps[2] + exps[3]

    sel = jnp.zeros((tm, LANES), F32)
    for hot in hots:
        sel = sel + jnp.where(hot, 1.0, 0.0)
    r = lax.broadcasted_iota(I32, (tm, tm), 0)
    c = lax.broadcasted_iota(I32, (tm, tm), 1)
    strict = jnp.where(c < r, 1.0, 0.0).astype(BF16)
    ahead = jnp.dot(strict, sel.astype(BF16), preferred_element_type=F32) + cnt_scr[...]

    info = jnp.zeros((tm, LANES), I32)
    gate = jnp.zeros((tm, LANES), F32)
    for k in range(TOP_K):
        pos = jnp.sum(jnp.where(hots[k], ahead, 0.0), axis=-1, keepdims=True).astype(I32)
        info = jnp.where(lane == k, ids[k], info)
        info = jnp.where(lane == TOP_K + k, pos, info)
        gate = jnp.where(lane == k, exps[k] / denom, gate)
    info_ref[...] = info
    gate_ref[...] = gate
    cnt_scr[...] = cnt_scr[...] + jnp.sum(sel, axis=0, keepdims=True)
    cnt_ref[...] = jnp.broadcast_to(cnt_scr[...], cnt_ref.shape)


def _route(logits):
    t = logits.shape[0]
    tm = ROUTE_TM
    row = lambda i: (i, 0)
    return pl.pallas_call(
        _route_kernel,
        grid=(t // tm,),
        in_specs=[pl.BlockSpec((tm, LANES), row)],
        out_specs=[
            pl.BlockSpec((tm, LANES), row),
            pl.BlockSpec((tm, LANES), row),
            pl.BlockSpec((SUBLANES, LANES), lambda i: (0, 0)),
        ],
        out_shape=[
            jax.ShapeDtypeStruct((t, LANES), I32),
            jax.ShapeDtypeStruct((t, LANES), F32),
            jax.ShapeDtypeStruct((SUBLANES, LANES), F32),
        ],
        scratch_shapes=[pltpu.VMEM((1, LANES), F32)],
        compiler_params=_cparams(("arbitrary",)),
        name="route",
    )(logits)


def _grouped_row(route_ref, start_ref, token, k):
    base = token * (2 * TOP_K)
    return start_ref[route_ref[base + k]] + route_ref[base + TOP_K + k]


def _dispatch_kernel(route_ref, start_ref, zero_row_ref, h2_ref, xs_hbm, zbuf, sem, zsem, *,
                     n_zero):
    tm = h2_ref.shape[0]
    i = pl.program_id(0)

    def zero_copy(j):
        row0 = pl.multiple_of(zero_row_ref[j], MOE_SUB)
        return pltpu.make_async_copy(zbuf, xs_hbm.at[pl.ds(row0, MOE_SUB)], zsem)

    @pl.when(i == 0)
    def _():
        zbuf[...] = jnp.zeros_like(zbuf)

        def start(j, carry):
            @pl.when(zero_row_ref[j] >= 0)
            def _():
                zero_copy(j).start()
            return carry

        def wait(j, carry):
            @pl.when(zero_row_ref[j] >= 0)
            def _():
                zero_copy(j).wait()
            return carry

        lax.fori_loop(0, n_zero, start, 0)
        lax.fori_loop(0, n_zero, wait, 0)

    for r in range(tm):
        for k in range(TOP_K):
            d = _grouped_row(route_ref, start_ref, i * tm + r, k)
            pltpu.make_async_copy(h2_ref.at[pl.ds(r, 1)], xs_hbm.at[pl.ds(d, 1)],
                                  sem).start(priority=k % 2)

    for _ in range(TOP_K):
        pltpu.make_async_copy(h2_ref, xs_hbm.at[pl.ds(0, tm)], sem).wait()


def _dispatch(route, starts, zero_rows, h2, n_rows):
    t = h2.shape[0]
    tm = DISPATCH_TM
    grid_spec = pltpu.PrefetchScalarGridSpec(
        num_scalar_prefetch=3,
        grid=(t // tm,),
        in_specs=[pl.BlockSpec((tm, D_MODEL), lambda i, r, s, z: (i, 0))],
        out_specs=pl.BlockSpec(memory_space=pl.ANY),
        scratch_shapes=[
            pltpu.VMEM((MOE_SUB, D_MODEL), F32),
            pltpu.SemaphoreType.DMA,
            pltpu.SemaphoreType.DMA,
        ],
    )
    return pl.pallas_call(
        functools.partial(_dispatch_kernel, n_zero=zero_rows.shape[0]),
        grid_spec=grid_spec,
        out_shape=jax.ShapeDtypeStruct((n_rows, D_MODEL), F32),
        compiler_params=_cparams(("arbitrary",)),
        name="dispatch",
    )(route, starts, zero_rows, h2)


def _experts_kernel(tile_e_ref, tile_occ_ref, tile_row_ref,
                    xs_hbm, wg_ref, wu_ref, bg_ref, bu_ref, wd_ref, bd_ref, out_ref,
                    xbuf, xb_scr, act_scr, sem, *, n_tiles):
    del tile_e_ref
    i = pl.program_id(0)
    j = pl.program_id(1)
    occ = tile_occ_ref[i]

    def sub_rows(sub):
        return pl.ds(pl.multiple_of(sub * MOE_SUB, MOE_SUB), MOE_SUB)

    def sub_copy(tile, sub):
        row0 = pl.multiple_of(tile_row_ref[tile] + sub * MOE_SUB, MOE_SUB)
        return pltpu.make_async_copy(xs_hbm.at[pl.ds(row0, MOE_SUB)], xbuf.at[sub_rows(sub)], sem)

    def start_load(tile):
        def body(sub, carry):
            sub_copy(tile, sub).start()
            return carry
        lax.fori_loop(0, tile_occ_ref[tile], body, 0)

    @pl.when(j == 0)
    def _():
        @pl.when(i == 0)
        def _():
            start_load(0)

        def wait_sub(sub, carry):
            sub_copy(i, sub).wait()
            return carry

        def round_sub(sub, carry):
            xb_scr[sub_rows(sub), :] = xbuf[sub_rows(sub), :].astype(BF16)
            return carry

        lax.fori_loop(0, occ, wait_sub, 0)
        lax.fori_loop(0, occ, round_sub, 0)

        @pl.when(i + 1 < n_tiles)
        def _():
            start_load(i + 1)

    def for_each_sub(weights, fn):
        def quad(p, carry):
            w = weights()
            for u in range(4):
                fn(4 * p + u, *w)
            return carry

        lax.fori_loop(0, occ // 4, quad, 0)
        done = (occ // 4) * 4

        @pl.when(occ % 4 >= 2)
        def _():
            w = weights()
            fn(done, *w)
            fn(done + 1, *w)

        @pl.when(occ % 2 == 1)
        def _():
            fn(occ - 1, *weights())

    @pl.when(j < MOE_NF)
    def _():
        def weights():
            return wg_ref[...].astype(BF16), wu_ref[...].astype(BF16)

        def gate_up(sub, wg, wu):
            rows = sub_rows(sub)
            xb = xb_scr[rows, :]
            gate = jnp.dot(xb, wg, preferred_element_type=F32) + bg_ref[...]
            up = jnp.dot(xb, wu, preferred_element_type=F32) + bu_ref[...]
            gate = jnp.minimum(gate, SWIGLU_LIMIT)
            up = jnp.clip(up, -SWIGLU_LIMIT, SWIGLU_LIMIT)
            act = (up + 1.0) * (gate * _sigmoid(SWIGLU_ALPHA * gate))
            act_scr[j, rows, :] = act.astype(BF16)

        for_each_sub(weights, gate_up)

    @pl.when(j >= MOE_NF)
    def _():
        def weights():
            return (wd_ref[...].astype(BF16),)

        def down(sub, wd):
            rows = sub_rows(sub)
            y = bd_ref[...] + jnp.dot(act_scr[0, rows, :], wd[0:MOE_TF, :],
                                      preferred_element_type=F32)
            for c in range(1, MOE_NF):
                y = y + jnp.dot(act_scr[c, rows, :], wd[c * MOE_TF:(c + 1) * MOE_TF, :],
                                preferred_element_type=F32)
            out_ref[rows, :] = y

        for_each_sub(weights, down)

    @pl.when(j >= MOE_NF)
    def _():
        def zero_sub(sub, carry):
            out_ref[sub_rows(sub), :] = jnp.zeros((MOE_SUB, MOE_TN), F32)
            return carry

        lax.fori_loop(occ, MOE_SUBS, zero_sub, 0)


def _experts(tile_e, tile_occ, tile_row, xs, wgu, bgu, wd, bd, n_tiles):
    up_off = D_FF // MOE_TF

    def fa(i, j, to):
        return jnp.where(to[i] > 0, jnp.minimum(j, MOE_NF - 1), MOE_NF - 1)

    def fb(i, j, to):
        return jnp.where(to[i] > 0, jnp.maximum(j - MOE_NF, 0), MOE_NN - 1)

    def gate_up_block(i, j, te, to):
        nxt = jnp.minimum(i + 1, n_tiles - 1)
        ahead = j >= MOE_NF
        tile = jnp.where(ahead, nxt, i)
        return te[tile], fa(tile, jnp.where(ahead, 0, j), to)

    def gate_map(i, j, te, to, tr):
        e, chunk = gate_up_block(i, j, te, to)
        return (e, 0, chunk)

    def up_map(i, j, te, to, tr):
        e, chunk = gate_up_block(i, j, te, to)
        return (e, 0, up_off + chunk)

    def down_map(i, j, te, to, tr):
        prev = jnp.maximum(i - 1, 0)
        hold = (j < MOE_NF) & (i > 0) & (to[i] > 0)
        return (jnp.where(hold, te[prev], te[i]), 0, jnp.where(hold, MOE_NN - 1, fb(i, j, to)))

    def out_map(i, j, te, to, tr):
        return (i, jnp.maximum(j - MOE_NF, 0))

    grid_spec = pltpu.PrefetchScalarGridSpec(
        num_scalar_prefetch=3,
        grid=(n_tiles, MOE_NF + MOE_NN),
        in_specs=[
            pl.BlockSpec(memory_space=pl.ANY),
            pl.BlockSpec((None, D_MODEL, MOE_TF), gate_map),
            pl.BlockSpec((None, D_MODEL, MOE_TF), up_map),
            pl.BlockSpec((None, 1, MOE_TF), gate_map),
            pl.BlockSpec((None, 1, MOE_TF), up_map),
            pl.BlockSpec((None, D_FF, MOE_TN), down_map),
            pl.BlockSpec((None, 1, MOE_TN), down_map),
        ],
        out_specs=pl.BlockSpec((MOE_CAP, MOE_TN), out_map),
        scratch_shapes=[
            pltpu.VMEM((MOE_CAP, D_MODEL), F32),
            pltpu.VMEM((MOE_CAP, D_MODEL), BF16),
            pltpu.VMEM((MOE_NF, MOE_CAP, MOE_TF), BF16),
            pltpu.SemaphoreType.DMA,
        ],
    )
    return pl.pallas_call(
        functools.partial(_experts_kernel, n_tiles=n_tiles),
        grid_spec=grid_spec,
        out_shape=jax.ShapeDtypeStruct((n_tiles * MOE_CAP, D_MODEL), F32),
        compiler_params=_cparams(("arbitrary", "arbitrary")),
        name="experts",
    )(tile_e, tile_occ, tile_row, xs, wgu, wgu, bgu, bgu, wd, bd)


def _combine_kernel(route_ref, start_ref, ys_hbm, x1_ref, gate_ref, gf_ref, out_ref, buf, sem, *,
                    n_tiles):
    tm = x1_ref.shape[0]
    i = pl.program_id(0)
    slot = i % 2

    def start_gather(tile, s):
        for r in range(tm):
            for k in range(TOP_K):
                d = _grouped_row(route_ref, start_ref, tile * tm + r, k)
                pltpu.make_async_copy(ys_hbm.at[pl.ds(d, 1)], buf.at[s, k, pl.ds(r, 1)],
                                      sem.at[s]).start(priority=k % 2)

    @pl.when(i == 0)
    def _():
        start_gather(0, 0)

    for k in range(TOP_K):
        pltpu.make_async_copy(ys_hbm.at[pl.ds(0, tm)], buf.at[slot, k], sem.at[slot]).wait()

    for s in range(2):
        @pl.when((i + 1 < n_tiles) & (slot == 1 - s))
        def _():
            start_gather(i + 1, s)

    x2 = x1_ref[...]
    for k in range(TOP_K):
        x2 = x2 + gate_ref[:, k:k + 1] * buf[slot, k]
    out_ref[...] = x2 * lax.rsqrt(jnp.mean(x2 * x2, axis=-1, keepdims=True) + EPS) * gf_ref[...]


def _combine(route, starts, ys, x1, gate, gf):
    t = x1.shape[0]
    tm = COMBINE_TM
    n_tiles = t // tm
    grid_spec = pltpu.PrefetchScalarGridSpec(
        num_scalar_prefetch=2,
        grid=(n_tiles,),
        in_specs=[
            pl.BlockSpec(memory_space=pl.ANY),
            pl.BlockSpec((tm, D_MODEL), lambda i, r, s: (i, 0)),
            pl.BlockSpec((tm, LANES), lambda i, r, s: (i, 0)),
            pl.BlockSpec((1, D_MODEL), lambda i, r, s: (0, 0)),
        ],
        out_specs=pl.BlockSpec((tm, D_MODEL), lambda i, r, s: (i, 0)),
        scratch_shapes=[
            pltpu.VMEM((2, TOP_K, tm, D_MODEL), F32),
            pltpu.SemaphoreType.DMA((2,)),
        ],
    )
    return pl.pallas_call(
        functools.partial(_combine_kernel, n_tiles=n_tiles),
        grid_spec=grid_spec,
        out_shape=jax.ShapeDtypeStruct((t, D_MODEL), F32),
        compiler_params=_cparams(("arbitrary",)),
        name="combine",
    )(route, starts, ys, x1, gate, gf)


def _routing_tables(counts, n_tokens):
    n_tiles = N_EXPERTS + (n_tokens * TOP_K) // MOE_CAP
    n_subs = N_EXPERTS + (n_tokens * TOP_K) // MOE_SUB
    subs_per = (counts + MOE_SUB - 1) // MOE_SUB
    sub_end = jnp.cumsum(subs_per)
    x_start = (sub_end - subs_per) * MOE_SUB
    tiles_per = (counts + MOE_CAP - 1) // MOE_CAP
    tile_end = jnp.cumsum(tiles_per)
    tile_start = tile_end - tiles_per
    y_start = tile_start * MOE_CAP

    tile_ids = jnp.arange(n_tiles, dtype=I32)
    n_active = tile_end[-1]
    last = jnp.maximum(n_active - 1, 0)
    active = tile_ids < n_active
    tile_eff = jnp.minimum(tile_ids, last)
    tile_e = jnp.sum((tile_eff[:, None] >= tile_end[None, :]).astype(I32), axis=1)
    tile_e = jnp.minimum(tile_e, N_EXPERTS - 1)
    in_expert = tile_eff - tile_start[tile_e]
    rows_in = counts[tile_e] - in_expert * MOE_CAP
    occ = jnp.clip((rows_in + MOE_SUB - 1) // MOE_SUB, 0, MOE_SUBS)
    tile_occ = jnp.where(active, occ, 0).astype(I32)
    tile_row = (x_start[tile_e] + in_expert * MOE_CAP).astype(I32)

    last_sub = jnp.where(counts > 0, x_start + (subs_per - 1) * MOE_SUB, -1)
    tail_ids = sub_end[-1] + jnp.arange(N_EXPERTS, dtype=I32)
    tail = jnp.where(tail_ids < n_subs, tail_ids * MOE_SUB, -1)
    zero_rows = jnp.concatenate([last_sub, tail]).astype(I32)
    return (n_tiles, n_subs * MOE_SUB, tile_e.astype(I32), tile_occ, tile_row,
            x_start, y_start, zero_rows)


def _layer(x2d, bsz, seq, g_norm_mix, w_in, b_in, w_conv, b_conv, w_sgu, b_sgu, g_sgu_ln, b_sgu_ln,
           g_mlstm_norm, w_proj_sgu, w_proj_mlstm, w_out, g_norm_moe, w_router, b_router,
           w_gate_up, b_gate_up, w_down, b_down, g_out):
    t = x2d.shape[0]

    pad = LANES - 2 * HEADS
    w_if = jnp.pad(w_in[:, N_MAIN:GATE_OFF], ((0, 0), (0, pad))).astype(BF16)
    b_if = jnp.pad(b_in[N_MAIN:GATE_OFF], (0, pad))[None, :]
    w_in_t = w_in.T
    bsb = jnp.broadcast_to(b_sgu.T[:, :, None], (SGU_BLOCK, SGU_GROUPS, SGU_GROUP_DIM))
    bsb = bsb.reshape(SGU_BLOCK, D_MODEL)
    w_r = jnp.pad(w_router, ((0, 0), (0, LANES - N_EXPERTS)))
    w_r_hi = w_r.astype(BF16)
    w_r_lo = (w_r - w_r_hi.astype(F32)).astype(BF16)
    b_r = jnp.pad(b_router, (0, LANES - N_EXPERTS))[None, :]

    h, gates = _norm_gates(x2d, g_norm_mix[None, :], w_if, b_if)
    zm = _in_proj(h, w_in_t, 0, b_in[None, :], N_MAIN, 2 * D_MODEL, 3 * D_MODEL, "in_proj_main")
    zg = _in_proj(h, w_in_t, GATE_OFF, b_in[None, GATE_OFF:], 2 * D_MODEL, 0, 0, "in_proj_gate")
    a = _sgu(zm, zg, w_sgu, bsb, g_sgu_ln[None, :], b_sgu_ln[None, :], w_proj_sgu.astype(BF16))
    ym = _mlstm(zm, gates, w_conv, b_conv[None, :], g_mlstm_norm[None, :], bsz, seq)
    x1, h2, logits = _merge(ym, a, zg, x2d, w_proj_mlstm.astype(BF16), w_out.astype(BF16),
                            g_norm_moe[None, :], w_r_hi, w_r_lo, b_r)
    info, gate, cnt = _route(logits)

    counts = cnt[0, :N_EXPERTS].astype(I32)
    (n_tiles, n_xrows, tile_e, tile_occ, tile_row, x_start, y_start,
     zero_rows) = _routing_tables(counts, t)
    route = info[:, :2 * TOP_K].reshape(-1)

    xs = _dispatch(route, x_start, zero_rows, h2, n_xrows)
    ys = _experts(tile_e, tile_occ, tile_row, xs, w_gate_up, b_gate_up[:, None, :],
                  w_down, b_down[:, None, :], n_tiles)
    return _combine(route, y_start, ys, x1, gate, g_out[None, :])


def kernel(x, g_norm_mix, w_in, b_in, w_conv, b_conv, w_sgu, b_sgu, g_sgu_ln, b_sgu_ln, g_mlstm_norm,
           w_proj_sgu, w_proj_mlstm, w_out, g_norm_moe, w_router, b_router, w_gate_up, b_gate_up,
           w_down, b_down, g_final):
    bsz, seq, d = x.shape
    assert d == D_MODEL and w_in.shape[0] == 1, "single-layer block with d_model 2048"
    out = _layer(x.reshape(bsz * seq, d), bsz, seq, g_norm_mix[0], w_in[0], b_in[0], w_conv[0],
                 b_conv[0], w_sgu[0], b_sgu[0], g_sgu_ln[0], b_sgu_ln[0], g_mlstm_norm[0],
                 w_proj_sgu[0], w_proj_mlstm[0], w_out[0], g_norm_moe[0], w_router[0], b_router[0],
                 w_gate_up[0], b_gate_up[0], w_down[0], b_down[0], g_final)
    return out.reshape(bsz, seq, d)
```

```python
import functools

import jax
import jax.numpy as jnp
from jax import lax
from jax.experimental import pallas as pl
from jax.experimental.pallas import tpu as pltpu

F32 = jnp.float32
BF16 = jnp.bfloat16
I32 = jnp.int32

D_MODEL = 2048
CHUNK = 64
SGU_BLOCK = 128
SGU_GROUPS = 8
SGU_GROUP_DIM = 256
HEADS = 4
HEAD_DIM = 512
CONV_WIDTH = 4
N_EXPERTS = 32
TOP_K = 4
D_FF = 2048
SWIGLU_LIMIT = 7.0
SWIGLU_ALPHA = 1.702
EPS = 1e-6

LANES = 128
SUBLANES = 8
VMEM_LIMIT = 56 * 1024 * 1024

ZB_U, ZB_V, ZB_Q, ZB_K, ZB_VM, ZB_O = range(6)
ZG_A, ZG_M = range(2)
N_MAIN = 6 * D_MODEL
GATE_OFF = N_MAIN + 2 * HEADS

NORM_TM = 512
INPROJ_TM = 1024
INPROJ_TN = 1024
INPROJ_RC = 256
SGU_TM = 256
MLSTM_L = 256
MERGE_TM = 256
MERGE_RC = 256
ROUTE_TM = 256
DISPATCH_TM = 256
MOE_SUB = 128
MOE_CAP = 1280
MOE_SUBS = MOE_CAP // MOE_SUB
MOE_TF = 512
MOE_NF = D_FF // MOE_TF
MOE_TN = 512
MOE_NN = D_MODEL // MOE_TN
COMBINE_TM = 128


def _cparams(sem, vmem_limit=VMEM_LIMIT):
    return pltpu.CompilerParams(dimension_semantics=sem, vmem_limit_bytes=vmem_limit)


def _gelu(x):
    return 0.5 * x * (1.0 + lax.erf(x * (2.0 ** -0.5)))


def _sigmoid(x):
    return 1.0 / (1.0 + jnp.exp(-x))


def _norm_gates_kernel(x_ref, g_ref, wif_ref, bif_ref, h_ref, gates_ref):
    x = x_ref[...]
    ms = jnp.mean(x * x, axis=-1, keepdims=True)
    h = (x * lax.rsqrt(ms + EPS) * g_ref[...]).astype(BF16)
    h_ref[...] = h
    gates_ref[...] = jnp.dot(h, wif_ref[...], preferred_element_type=F32) + bif_ref[...]


def _norm_gates(x, g, wif, bif):
    t = x.shape[0]
    tm = min(NORM_TM, t)
    row = lambda i: (i, 0)
    const = lambda i: (0, 0)
    return pl.pallas_call(
        _norm_gates_kernel,
        grid=(t // tm,),
        in_specs=[
            pl.BlockSpec((tm, D_MODEL), row),
            pl.BlockSpec((1, D_MODEL), const),
            pl.BlockSpec((D_MODEL, LANES), const),
            pl.BlockSpec((1, LANES), const),
        ],
        out_specs=[pl.BlockSpec((tm, D_MODEL), row), pl.BlockSpec((tm, LANES), row)],
        out_shape=[jax.ShapeDtypeStruct((t, D_MODEL), BF16), jax.ShapeDtypeStruct((t, LANES), F32)],
        compiler_params=_cparams(("arbitrary",)),
        name="norm_gates",
    )(x, g, wif, bif)


def _inproj_kernel(h_ref, w_ref, b_ref, z_ref, w_scr, *, n_gelu, n_plain):
    j = pl.program_id(0)

    @pl.when(pl.program_id(1) == 0)
    def _():
        w_scr[...] = w_ref[...].astype(BF16)

    def project(activation):
        tm = h_ref.shape[0]
        rc = min(INPROJ_RC, tm)
        for c in range(tm // rc):
            rows = slice(c * rc, (c + 1) * rc)
            acc = lax.dot_general(h_ref[rows, :], w_scr[...], (((1,), (1,)), ((), ())),
                                  preferred_element_type=F32) + b_ref[...]
            z_ref[rows, :] = activation(acc).astype(z_ref.dtype)

    @pl.when(j < n_gelu)
    def _():
        project(_gelu)

    @pl.when((j >= n_gelu) & (j < n_gelu + n_plain))
    def _():
        project(lambda acc: acc)

    @pl.when(j >= n_gelu + n_plain)
    def _():
        project(_sigmoid)


def _in_proj(h, w_t, row_off, b, n_cols, n_gelu_cols, n_plain_cols, name):
    t = h.shape[0]
    tm, tn = min(INPROJ_TM, t), INPROJ_TN
    assert row_off % SUBLANES == 0
    return pl.pallas_call(
        functools.partial(_inproj_kernel, n_gelu=n_gelu_cols // tn, n_plain=n_plain_cols // tn),
        grid=(n_cols // tn, t // tm),
        in_specs=[
            pl.BlockSpec((tm, D_MODEL), lambda j, i: (i, 0)),
            pl.BlockSpec((pl.Element(tn), pl.Element(D_MODEL)),
                         lambda j, i: (pl.multiple_of(row_off + j * tn, SUBLANES), 0)),
            pl.BlockSpec((1, tn), lambda j, i: (0, j)),
        ],
        out_specs=pl.BlockSpec((tm, tn), lambda j, i: (i, j)),
        out_shape=jax.ShapeDtypeStruct((t, n_cols), BF16),
        scratch_shapes=[pltpu.VMEM((tn, D_MODEL), BF16)],
        compiler_params=_cparams(("arbitrary", "arbitrary")),
        name=name,
    )(h, w_t, b)


def _sgu_kernel(gu_ref, gv_ref, sga_ref, ws_ref, bsb_ref, lng_ref, lnb_ref, pa_ref, out_ref,
                ya_scr):
    tm = gu_ref.shape[0]
    v = gv_ref[...].astype(F32)
    mu = jnp.mean(v, axis=-1, keepdims=True)
    vc = v - mu
    var = jnp.mean(vc * vc, axis=-1, keepdims=True)
    vln = (vc * lax.rsqrt(var + EPS) * lng_ref[...] + lnb_ref[...]).astype(BF16)

    t_id = lax.broadcasted_iota(I32, (SGU_BLOCK, SGU_BLOCK), 0) // CHUNK
    s_id = lax.broadcasted_iota(I32, (SGU_BLOCK, SGU_BLOCK), 1) // CHUNK
    causal = s_id <= t_id
    for g in range(SGU_GROUPS):
        w = jnp.where(causal, ws_ref[g], 0.0).astype(BF16)
        cols = slice(g * SGU_GROUP_DIM, (g + 1) * SGU_GROUP_DIM)
        bias = bsb_ref[:, cols]
        for blk in range(tm // SGU_BLOCK):
            rows = slice(blk * SGU_BLOCK, (blk + 1) * SGU_BLOCK)
            mixed = jnp.dot(w, vln[rows, cols], preferred_element_type=F32) + bias
            ya_scr[rows, cols] = (gu_ref[rows, cols].astype(F32) * mixed).astype(BF16)

    proj = jnp.dot(ya_scr[...], pa_ref[...], preferred_element_type=F32)
    out_ref[...] = (sga_ref[...].astype(F32) * proj).astype(out_ref.dtype)


def _sgu(zm, zg, ws, bsb, lng, lnb, pa):
    t = zm.shape[0]
    tm = SGU_TM
    const2 = lambda i: (0, 0)
    return pl.pallas_call(
        _sgu_kernel,
        grid=(t // tm,),
        in_specs=[
            pl.BlockSpec((tm, D_MODEL), lambda i: (i, ZB_U)),
            pl.BlockSpec((tm, D_MODEL), lambda i: (i, ZB_V)),
            pl.BlockSpec((tm, D_MODEL), lambda i: (i, ZG_A)),
            pl.BlockSpec((SGU_GROUPS, SGU_BLOCK, SGU_BLOCK), lambda i: (0, 0, 0)),
            pl.BlockSpec((SGU_BLOCK, D_MODEL), const2),
            pl.BlockSpec((1, D_MODEL), const2),
            pl.BlockSpec((1, D_MODEL), const2),
            pl.BlockSpec((D_MODEL, D_MODEL), const2),
        ],
        out_specs=pl.BlockSpec((tm, D_MODEL), lambda i: (i, 0)),
        out_shape=jax.ShapeDtypeStruct((t, D_MODEL), BF16),
        scratch_shapes=[pltpu.VMEM((tm, D_MODEL), BF16)],
        compiler_params=_cparams(("arbitrary",)),
        name="sgu",
    )(zm, zm, zg, ws, bsb, lng, lnb, pa)


def _mlstm_kernel(q_ref, k_ref, v_ref, so_ref, gates_ref, wc_ref, bc_ref, gn_ref, out_ref,
                  hist, ct_scr, n_scr, m_scr):
    L = q_ref.shape[0]
    c = pl.program_id(1)

    @pl.when(c == 0)
    def _():
        hist[0:SUBLANES, :] = jnp.zeros((SUBLANES, 2 * D_MODEL), F32)
        ct_scr[...] = jnp.zeros_like(ct_scr)
        n_scr[...] = jnp.zeros_like(n_scr)
        m_scr[...] = jnp.zeros_like(m_scr)

    hist[SUBLANES:SUBLANES + L, 0:D_MODEL] = q_ref[...].astype(F32)
    hist[SUBLANES:SUBLANES + L, D_MODEL:2 * D_MODEL] = k_ref[...].astype(F32)
    qk = bc_ref[...] + wc_ref[CONV_WIDTH - 1:CONV_WIDTH, :] * hist[SUBLANES:SUBLANES + L, :]
    for j in range(CONV_WIDTH - 1):
        off = SUBLANES - (CONV_WIDTH - 1) + j
        qk = qk + wc_ref[j:j + 1, :] * hist[off:off + L, :]
    qk = qk * _sigmoid(qk)
    hist[0:SUBLANES, :] = hist[L:L + SUBLANES, :]

    gts = gates_ref[...]
    lf = jnp.minimum(gts, 0.0) - jnp.log(1.0 + jnp.exp(-jnp.abs(gts)))
    row = lax.broadcasted_iota(I32, (L, L), 0)
    col = lax.broadcasted_iota(I32, (L, L), 1)
    tril = col <= row
    tri = jnp.where(tril, 1.0, 0.0).astype(BF16)
    lf_hi = lf.astype(BF16)
    lf_lo = (lf - lf_hi.astype(F32)).astype(BF16)
    bcum = (jnp.dot(tri, lf_hi, preferred_element_type=F32)
            + jnp.dot(tri, lf_lo, preferred_element_type=F32))
    bcum_t = bcum.T
    gts_t = gts.T

    scale = HEAD_DIM ** -0.5
    for h in range(HEADS):
        cols = slice(h * HEAD_DIM, (h + 1) * HEAD_DIM)
        q = qk[:, h * HEAD_DIM:(h + 1) * HEAD_DIM] * scale
        k = qk[:, D_MODEL + h * HEAD_DIM:D_MODEL + (h + 1) * HEAD_DIM]
        qb = q.astype(BF16)
        kb = k.astype(BF16)
        vb = v_ref[:, cols]
        bc = bcum[:, HEADS + h:HEADS + h + 1]
        bc_r = bcum_t[HEADS + h:HEADS + h + 1, :]
        ig_c = gts[:, h:h + 1]
        ig_r = gts_t[h:h + 1, :]
        m_st = m_scr[h]
        n_st = n_scr[h]

        dlog = jnp.where(tril, bc - bc_r + ig_r, -jnp.inf)
        m_inter = bc + m_st
        m_t = jnp.maximum(m_inter, jnp.max(dlog, axis=-1, keepdims=True))
        p = jnp.exp(dlog - m_t)
        s = lax.dot_general(qb, kb, (((1,), (1,)), ((), ())), preferred_element_type=F32)
        sc = s * p
        inter = jnp.exp(m_inter - m_t)
        ctb = ct_scr[h].astype(BF16)
        num = (jnp.dot(sc.astype(BF16), vb, preferred_element_type=F32)
               + inter * jnp.dot(qb, ctb, preferred_element_type=F32))
        den = (jnp.sum(sc, axis=-1, keepdims=True)
               + inter * jnp.sum(q * n_st, axis=-1, keepdims=True))
        hh = num / jnp.maximum(jnp.abs(den), jnp.exp(-m_t))
        hn = hh * lax.rsqrt(jnp.mean(hh * hh, axis=-1, keepdims=True) + EPS) * gn_ref[:, cols]
        out_ref[:, cols] = (hn * so_ref[:, cols].astype(F32)).astype(out_ref.dtype)

        g_tot = bc[L - 1:L, :]
        a = g_tot - bc + ig_c
        m_new = jnp.maximum(g_tot + m_st, jnp.max(a, axis=0, keepdims=True))
        wa = jnp.exp(a - m_new)
        decay = jnp.exp(g_tot + m_st - m_new)
        wv = (wa * vb.astype(F32)).astype(BF16)
        upd = jnp.dot(k.T.astype(BF16), wv, preferred_element_type=F32)
        ct_scr[h] = decay * ct_scr[h] + upd
        n_scr[h] = decay * n_st + jnp.sum(wa * k, axis=0, keepdims=True)
        m_scr[h] = m_new


def _mlstm(zm, gates, wc, bc, gn, bsz, seq):
    t = zm.shape[0]
    L = MLSTM_L
    nc = seq // L
    zspec = lambda blk: pl.BlockSpec((L, D_MODEL), lambda b, c: (b * nc + c, blk))
    const2 = lambda b, c: (0, 0)
    return pl.pallas_call(
        _mlstm_kernel,
        grid=(bsz, nc),
        in_specs=[
            zspec(ZB_Q), zspec(ZB_K), zspec(ZB_VM), zspec(ZB_O),
            pl.BlockSpec((L, LANES), lambda b, c: (b * nc + c, 0)),
            pl.BlockSpec((CONV_WIDTH, 2 * D_MODEL), const2),
            pl.BlockSpec((1, 2 * D_MODEL), const2),
            pl.BlockSpec((1, D_MODEL), const2),
        ],
        out_specs=pl.BlockSpec((L, D_MODEL), lambda b, c: (b * nc + c, 0)),
        out_shape=jax.ShapeDtypeStruct((t, D_MODEL), BF16),
        scratch_shapes=[
            pltpu.VMEM((L + 2 * SUBLANES, 2 * D_MODEL), F32),
            pltpu.VMEM((HEADS, HEAD_DIM, HEAD_DIM), F32),
            pltpu.VMEM((HEADS, 1, HEAD_DIM), F32),
            pltpu.VMEM((HEADS, 1, 1), F32),
        ],
        compiler_params=_cparams(("arbitrary", "arbitrary")),
        name="mlstm",
    )(zm, zm, zm, zm, gates, wc, bc, gn)


def _merge_kernel(ym_ref, a_ref, sgm_ref, x_ref, pm_ref, wo_ref, g2_ref, wrh_ref, wrl_ref, br_ref,
                  x1_ref, h2_ref, lg_ref):
    tm = x_ref.shape[0]
    rc = min(MERGE_RC, tm)
    for c in range(tm // rc):
        rows = slice(c * rc, (c + 1) * rc)
        proj = jnp.dot(ym_ref[rows, :], pm_ref[...], preferred_element_type=F32)
        merged = a_ref[rows, :].astype(F32) + sgm_ref[rows, :].astype(F32) * proj
        x1 = x_ref[rows, :] + jnp.dot(merged.astype(BF16), wo_ref[...],
                                      preferred_element_type=F32)
        x1_ref[rows, :] = x1
        h2 = x1 * lax.rsqrt(jnp.mean(x1 * x1, axis=-1, keepdims=True) + EPS) * g2_ref[...]
        h2_ref[rows, :] = h2
        h_hi = h2.astype(BF16)
        h_lo = (h2 - h_hi.astype(F32)).astype(BF16)
        lg_ref[rows, :] = (jnp.dot(h_hi, wrh_ref[...], preferred_element_type=F32)
                           + (jnp.dot(h_lo, wrh_ref[...], preferred_element_type=F32)
                              + jnp.dot(h_hi, wrl_ref[...], preferred_element_type=F32))
                           + br_ref[...])


def _merge(ym, a, zg, x, pm, wo, g2, wr_hi, wr_lo, br):
    t = x.shape[0]
    tm = MERGE_TM
    const2 = lambda i: (0, 0)
    row = lambda i: (i, 0)
    return pl.pallas_call(
        _merge_kernel,
        grid=(t // tm,),
        in_specs=[
            pl.BlockSpec((tm, D_MODEL), row),
            pl.BlockSpec((tm, D_MODEL), row),
            pl.BlockSpec((tm, D_MODEL), lambda i: (i, ZG_M)),
            pl.BlockSpec((tm, D_MODEL), row),
            pl.BlockSpec((D_MODEL, D_MODEL), const2),
            pl.BlockSpec((D_MODEL, D_MODEL), const2),
            pl.BlockSpec((1, D_MODEL), const2),
            pl.BlockSpec((D_MODEL, LANES), const2),
            pl.BlockSpec((D_MODEL, LANES), const2),
            pl.BlockSpec((1, LANES), const2),
        ],
        out_specs=[
            pl.BlockSpec((tm, D_MODEL), row),
            pl.BlockSpec((tm, D_MODEL), row),
            pl.BlockSpec((tm, LANES), row),
        ],
        out_shape=[
            jax.ShapeDtypeStruct((t, D_MODEL), F32),
            jax.ShapeDtypeStruct((t, D_MODEL), F32),
            jax.ShapeDtypeStruct((t, LANES), F32),
        ],
        compiler_params=_cparams(("arbitrary",)),
        name="merge",
    )(ym, a, zg, x, pm, wo, g2, wr_hi, wr_lo, br)


def _route_kernel(lg_ref, info_ref, gate_ref, cnt_ref, cnt_scr):
    tm = lg_ref.shape[0]
    i = pl.program_id(0)

    @pl.when(i == 0)
    def _():
        cnt_scr[...] = jnp.zeros_like(cnt_scr)

    lane = lax.broadcasted_iota(I32, (tm, LANES), 1)
    lane_f = lane.astype(F32)
    vals = jnp.where(lane < N_EXPERTS, lg_ref[...], -jnp.inf)
    tops, ids, hots = [], [], []
    for _ in range(TOP_K):
        m = jnp.max(vals, axis=-1, keepdims=True)
        idx_f = jnp.min(jnp.where(vals == m, lane_f, float(LANES)), axis=-1, keepdims=True)
        idx = idx_f.astype(I32)
        hot = lane == idx
        tops.append(m)
        ids.append(idx)
        hots.append(hot)
        vals = jnp.where(hot, -jnp.inf, vals)

    exps = [jnp.exp(tv - tops[0]) for tv in tops]
    denom = exps[0] + exps[1] + exps[2] + exps[3]

    sel = jnp.zeros((tm, LANES), F32)
    for hot in hots:
        sel = sel + jnp.where(hot, 1.0, 0.0)
    r = lax.broadcasted_iota(I32, (tm, tm), 0)
    c = lax.broadcasted_iota(I32, (tm, tm), 1)
    strict = jnp.where(c < r, 1.0, 0.0).astype(BF16)
    ahead = jnp.dot(strict, sel.astype(BF16), preferred_element_type=F32) + cnt_scr[...]

    info = jnp.zeros((tm, LANES), I32)
    gate = jnp.zeros((tm, LANES), F32)
    for k in range(TOP_K):
        pos = jnp.sum(jnp.where(hots[k], ahead, 0.0), axis=-1, keepdims=True).astype(I32)
        info = jnp.where(lane == k, ids[k], info)
        info = jnp.where(lane == TOP_K + k, pos, info)
        gate = jnp.where(lane == k, exps[k] / denom, gate)
    info_ref[...] = info
    gate_ref[...] = gate
    cnt_scr[...] = cnt_scr[...] + jnp.sum(sel, axis=0, keepdims=True)
    cnt_ref[...] = jnp.broadcast_to(cnt_scr[...], cnt_ref.shape)


def _route(logits):
    t = logits.shape[0]
    tm = ROUTE_TM
    row = lambda i: (i, 0)
    return pl.pallas_call(
        _route_kernel,
        grid=(t // tm,),
        in_specs=[pl.BlockSpec((tm, LANES), row)],
        out_specs=[
            pl.BlockSpec((tm, LANES), row),
            pl.BlockSpec((tm, LANES), row),
            pl.BlockSpec((SUBLANES, LANES), lambda i: (0, 0)),
        ],
        out_shape=[
            jax.ShapeDtypeStruct((t, LANES), I32),
            jax.ShapeDtypeStruct((t, LANES), F32),
            jax.ShapeDtypeStruct((SUBLANES, LANES), F32),
        ],
        scratch_shapes=[pltpu.VMEM((1, LANES), F32)],
        compiler_params=_cparams(("arbitrary",)),
        name="route",
    )(logits)


def _rows_kernel(info_ref, xstart_ref, ystart_ref, rows_ref):
    tm = info_ref.shape[0]
    lane = lax.broadcasted_iota(I32, (tm, LANES), 1)
    info = info_ref[...]
    rows = jnp.zeros((tm, LANES), I32)
    for k in range(TOP_K):
        hot = lane == info[:, k:k + 1]
        pos = info[:, TOP_K + k:TOP_K + k + 1]
        for space, start_ref in enumerate((xstart_ref, ystart_ref)):
            start = jnp.sum(jnp.where(hot, start_ref[...], 0.0), axis=-1, keepdims=True)
            rows = jnp.where(lane == space * TOP_K + k, start.astype(I32) + pos, rows)
    rows_ref[...] = rows


def _rows(info, x_start, y_start):
    t = info.shape[0]
    tm = ROUTE_TM
    row = lambda i: (i, 0)
    const = lambda i: (0, 0)
    pad = lambda s: jnp.pad(s.astype(F32), (0, LANES - N_EXPERTS))[None, :]
    return pl.pallas_call(
        _rows_kernel,
        grid=(t // tm,),
        in_specs=[pl.BlockSpec((tm, LANES), row), pl.BlockSpec((1, LANES), const),
                  pl.BlockSpec((1, LANES), const)],
        out_specs=pl.BlockSpec((tm, LANES), row),
        out_shape=jax.ShapeDtypeStruct((t, LANES), I32),
        compiler_params=_cparams(("arbitrary",)),
        name="rows",
    )(info, pad(x_start), pad(y_start))


def _dispatch_kernel(dest_ref, zero_row_ref, h2_ref, xs_hbm, zbuf, sem, zsem, *, n_zero):
    tm = h2_ref.shape[0]
    i = pl.program_id(0)

    def zero_copy(j):
        row0 = pl.multiple_of(zero_row_ref[j], MOE_SUB)
        return pltpu.make_async_copy(zbuf, xs_hbm.at[pl.ds(row0, MOE_SUB)], zsem)

    @pl.when(i == 0)
    def _():
        zbuf[...] = jnp.zeros_like(zbuf)

        def start(j, carry):
            @pl.when(zero_row_ref[j] >= 0)
            def _():
                zero_copy(j).start()
            return carry

        def wait(j, carry):
            @pl.when(zero_row_ref[j] >= 0)
            def _():
                zero_copy(j).wait()
            return carry

        lax.fori_loop(0, n_zero, start, 0)
        lax.fori_loop(0, n_zero, wait, 0)

    for r in range(tm):
        for k in range(TOP_K):
            d = dest_ref[(i * tm + r) * TOP_K + k]
            pltpu.make_async_copy(h2_ref.at[pl.ds(r, 1)], xs_hbm.at[pl.ds(d, 1)],
                                  sem).start(priority=k % 2)

    for _ in range(TOP_K):
        pltpu.make_async_copy(h2_ref, xs_hbm.at[pl.ds(0, tm)], sem).wait()


def _dispatch(dest, zero_rows, h2, n_rows):
    t = h2.shape[0]
    tm = DISPATCH_TM
    grid_spec = pltpu.PrefetchScalarGridSpec(
        num_scalar_prefetch=2,
        grid=(t // tm,),
        in_specs=[pl.BlockSpec((tm, D_MODEL), lambda i, d, z: (i, 0))],
        out_specs=pl.BlockSpec(memory_space=pl.ANY),
        scratch_shapes=[
            pltpu.VMEM((MOE_SUB, D_MODEL), F32),
            pltpu.SemaphoreType.DMA,
            pltpu.SemaphoreType.DMA,
        ],
    )
    return pl.pallas_call(
        functools.partial(_dispatch_kernel, n_zero=zero_rows.shape[0]),
        grid_spec=grid_spec,
        out_shape=jax.ShapeDtypeStruct((n_rows, D_MODEL), F32),
        compiler_params=_cparams(("arbitrary",)),
        name="dispatch",
    )(dest, zero_rows, h2)


def _experts_kernel(tile_e_ref, tile_occ_ref, tile_row_ref,
                    xs_hbm, wg_ref, wu_ref, bg_ref, bu_ref, wd_ref, bd_ref, out_ref,
                    xbuf, xb_scr, act_scr, sem, *, n_tiles):
    del tile_e_ref
    i = pl.program_id(0)
    j = pl.program_id(1)
    occ = tile_occ_ref[i]

    def sub_rows(sub):
        return pl.ds(pl.multiple_of(sub * MOE_SUB, MOE_SUB), MOE_SUB)

    def sub_copy(tile, sub):
        row0 = pl.multiple_of(tile_row_ref[tile] + sub * MOE_SUB, MOE_SUB)
        return pltpu.make_async_copy(xs_hbm.at[pl.ds(row0, MOE_SUB)], xbuf.at[sub_rows(sub)], sem)

    def start_load(tile):
        def body(sub, carry):
            sub_copy(tile, sub).start()
            return carry
        lax.fori_loop(0, tile_occ_ref[tile], body, 0)

    @pl.when(j == 0)
    def _():
        @pl.when(i == 0)
        def _():
            start_load(0)

        def wait_sub(sub, carry):
            sub_copy(i, sub).wait()
            return carry

        def round_sub(sub, carry):
            xb_scr[sub_rows(sub), :] = xbuf[sub_rows(sub), :].astype(BF16)
            return carry

        lax.fori_loop(0, occ, wait_sub, 0)
        lax.fori_loop(0, occ, round_sub, 0)

        @pl.when(i + 1 < n_tiles)
        def _():
            start_load(i + 1)

    def for_each_sub(weights, fn):
        def quad(p, carry):
            w = weights()
            for u in range(4):
                fn(4 * p + u, *w)
            return carry

        lax.fori_loop(0, occ // 4, quad, 0)
        done = (occ // 4) * 4

        @pl.when(occ % 4 >= 2)
        def _():
            w = weights()
            fn(done, *w)
            fn(done + 1, *w)

        @pl.when(occ % 2 == 1)
        def _():
            fn(occ - 1, *weights())

    @pl.when(j < MOE_NF)
    def _():
        def weights():
            return wg_ref[...].astype(BF16), wu_ref[...].astype(BF16)

        def gate_up(sub, wg, wu):
            rows = sub_rows(sub)
            xb = xb_scr[rows, :]
            gate = jnp.dot(xb, wg, preferred_element_type=F32) + bg_ref[...]
            up = jnp.dot(xb, wu, preferred_element_type=F32) + bu_ref[...]
            gate = jnp.minimum(gate, SWIGLU_LIMIT)
            up = jnp.clip(up, -SWIGLU_LIMIT, SWIGLU_LIMIT)
            act = (up + 1.0) * (gate * _sigmoid(SWIGLU_ALPHA * gate))
            act_scr[j, rows, :] = act.astype(BF16)

        for_each_sub(weights, gate_up)

    @pl.when(j >= MOE_NF)
    def _():
        def weights():
            return (wd_ref[...].astype(BF16),)

        def down(sub, wd):
            rows = sub_rows(sub)
            y = bd_ref[...] + jnp.dot(act_scr[0, rows, :], wd[0:MOE_TF, :],
                                      preferred_element_type=F32)
            for c in range(1, MOE_NF):
                y = y + jnp.dot(act_scr[c, rows, :], wd[c * MOE_TF:(c + 1) * MOE_TF, :],
                                preferred_element_type=F32)
            out_ref[rows, :] = y

        for_each_sub(weights, down)

    @pl.when(j >= MOE_NF)
    def _():
        def zero_sub(sub, carry):
            out_ref[sub_rows(sub), :] = jnp.zeros((MOE_SUB, MOE_TN), F32)
            return carry

        lax.fori_loop(occ, MOE_SUBS, zero_sub, 0)


def _experts(tile_e, tile_occ, tile_row, xs, wgu, bgu, wd, bd, n_tiles):
    up_off = D_FF // MOE_TF

    def fa(i, j, to):
        return jnp.where(to[i] > 0, jnp.minimum(j, MOE_NF - 1), MOE_NF - 1)

    def fb(i, j, to):
        return jnp.where(to[i] > 0, jnp.maximum(j - MOE_NF, 0), MOE_NN - 1)

    def gate_up_block(i, j, te, to):
        nxt = jnp.minimum(i + 1, n_tiles - 1)
        ahead = j >= MOE_NF
        tile = jnp.where(ahead, nxt, i)
        return te[tile], fa(tile, jnp.where(ahead, 0, j), to)

    def gate_map(i, j, te, to, tr):
        e, chunk = gate_up_block(i, j, te, to)
        return (e, 0, chunk)

    def up_map(i, j, te, to, tr):
        e, chunk = gate_up_block(i, j, te, to)
        return (e, 0, up_off + chunk)

    def down_map(i, j, te, to, tr):
        prev = jnp.maximum(i - 1, 0)
        hold = (j < MOE_NF) & (i > 0) & (to[i] > 0)
        return (jnp.where(hold, te[prev], te[i]), 0, jnp.where(hold, MOE_NN - 1, fb(i, j, to)))

    def out_map(i, j, te, to, tr):
        return (i, jnp.maximum(j - MOE_NF, 0))

    grid_spec = pltpu.PrefetchScalarGridSpec(
        num_scalar_prefetch=3,
        grid=(n_tiles, MOE_NF + MOE_NN),
        in_specs=[
            pl.BlockSpec(memory_space=pl.ANY),
            pl.BlockSpec((None, D_MODEL, MOE_TF), gate_map),
            pl.BlockSpec((None, D_MODEL, MOE_TF), up_map),
            pl.BlockSpec((None, 1, MOE_TF), gate_map),
            pl.BlockSpec((None, 1, MOE_TF), up_map),
            pl.BlockSpec((None, D_FF, MOE_TN), down_map),
            pl.BlockSpec((None, 1, MOE_TN), down_map),
        ],
        out_specs=pl.BlockSpec((MOE_CAP, MOE_TN), out_map),
        scratch_shapes=[
            pltpu.VMEM((MOE_CAP, D_MODEL), F32),
            pltpu.VMEM((MOE_CAP, D_MODEL), BF16),
            pltpu.VMEM((MOE_NF, MOE_CAP, MOE_TF), BF16),
            pltpu.SemaphoreType.DMA,
        ],
    )
    return pl.pallas_call(
        functools.partial(_experts_kernel, n_tiles=n_tiles),
        grid_spec=grid_spec,
        out_shape=jax.ShapeDtypeStruct((n_tiles * MOE_CAP, D_MODEL), F32),
        compiler_params=_cparams(("arbitrary", "arbitrary")),
        name="experts",
    )(tile_e, tile_occ, tile_row, xs, wgu, wgu, bgu, bgu, wd, bd)


def _combine_kernel(dest_ref, ys_hbm, x1_ref, gate_ref, gf_ref, out_ref, buf, sem, *, n_tiles):
    tm = x1_ref.shape[0]
    i = pl.program_id(0)
    slot = i % 2

    def start_gather(tile, s):
        for r in range(tm):
            for k in range(TOP_K):
                d = dest_ref[(tile * tm + r) * TOP_K + k]
                pltpu.make_async_copy(ys_hbm.at[pl.ds(d, 1)], buf.at[s, k, pl.ds(r, 1)],
                                      sem.at[s]).start(priority=k % 2)

    @pl.when(i == 0)
    def _():
        start_gather(0, 0)

    for k in range(TOP_K):
        pltpu.make_async_copy(ys_hbm.at[pl.ds(0, tm)], buf.at[slot, k], sem.at[slot]).wait()

    for s in range(2):
        @pl.when((i + 1 < n_tiles) & (slot == 1 - s))
        def _():
            start_gather(i + 1, s)

    x2 = x1_ref[...]
    for k in range(TOP_K):
        x2 = x2 + gate_ref[:, k:k + 1] * buf[slot, k]
    out_ref[...] = x2 * lax.rsqrt(jnp.mean(x2 * x2, axis=-1, keepdims=True) + EPS) * gf_ref[...]


def _combine(dest, ys, x1, gate, gf):
    t = x1.shape[0]
    tm = COMBINE_TM
    n_tiles = t // tm
    grid_spec = pltpu.PrefetchScalarGridSpec(
        num_scalar_prefetch=1,
        grid=(n_tiles,),
        in_specs=[
            pl.BlockSpec(memory_space=pl.ANY),
            pl.BlockSpec((tm, D_MODEL), lambda i, d: (i, 0)),
            pl.BlockSpec((tm, LANES), lambda i, d: (i, 0)),
            pl.BlockSpec((1, D_MODEL), lambda i, d: (0, 0)),
        ],
        out_specs=pl.BlockSpec((tm, D_MODEL), lambda i, d: (i, 0)),
        scratch_shapes=[
            pltpu.VMEM((2, TOP_K, tm, D_MODEL), F32),
            pltpu.SemaphoreType.DMA((2,)),
        ],
    )
    return pl.pallas_call(
        functools.partial(_combine_kernel, n_tiles=n_tiles),
        grid_spec=grid_spec,
        out_shape=jax.ShapeDtypeStruct((t, D_MODEL), F32),
        compiler_params=_cparams(("arbitrary",)),
        name="combine",
    )(dest, ys, x1, gate, gf)


def _routing_tables(counts, n_tokens):
    n_tiles = N_EXPERTS + (n_tokens * TOP_K) // MOE_CAP
    n_subs = N_EXPERTS + (n_tokens * TOP_K) // MOE_SUB
    subs_per = (counts + MOE_SUB - 1) // MOE_SUB
    sub_end = jnp.cumsum(subs_per)
    x_start = (sub_end - subs_per) * MOE_SUB
    tiles_per = (counts + MOE_CAP - 1) // MOE_CAP
    tile_end = jnp.cumsum(tiles_per)
    tile_start = tile_end - tiles_per
    y_start = tile_start * MOE_CAP

    tile_ids = jnp.arange(n_tiles, dtype=I32)
    n_active = tile_end[-1]
    last = jnp.maximum(n_active - 1, 0)
    active = tile_ids < n_active
    tile_eff = jnp.minimum(tile_ids, last)
    tile_e = jnp.sum((tile_eff[:, None] >= tile_end[None, :]).astype(I32), axis=1)
    tile_e = jnp.minimum(tile_e, N_EXPERTS - 1)
    in_expert = tile_eff - tile_start[tile_e]
    rows_in = counts[tile_e] - in_expert * MOE_CAP
    occ = jnp.clip((rows_in + MOE_SUB - 1) // MOE_SUB, 0, MOE_SUBS)
    tile_occ = jnp.where(active, occ, 0).astype(I32)
    tile_row = (x_start[tile_e] + in_expert * MOE_CAP).astype(I32)

    last_sub = jnp.where(counts > 0, x_start + (subs_per - 1) * MOE_SUB, -1)
    tail_ids = sub_end[-1] + jnp.arange(N_EXPERTS, dtype=I32)
    tail = jnp.where(tail_ids < n_subs, tail_ids * MOE_SUB, -1)
    zero_rows = jnp.concatenate([last_sub, tail]).astype(I32)
    return (n_tiles, n_subs * MOE_SUB, tile_e.astype(I32), tile_occ, tile_row,
            x_start, y_start, zero_rows)


def _layer(x2d, bsz, seq, g_norm_mix, w_in, b_in, w_conv, b_conv, w_sgu, b_sgu, g_sgu_ln, b_sgu_ln,
           g_mlstm_norm, w_proj_sgu, w_proj_mlstm, w_out, g_norm_moe, w_router, b_router,
           w_gate_up, b_gate_up, w_down, b_down, g_out):
    t = x2d.shape[0]

    pad = LANES - 2 * HEADS
    w_if = jnp.pad(w_in[:, N_MAIN:GATE_OFF], ((0, 0), (0, pad))).astype(BF16)
    b_if = jnp.pad(b_in[N_MAIN:GATE_OFF], (0, pad))[None, :]
    w_in_t = w_in.T
    bsb = jnp.broadcast_to(b_sgu.T[:, :, None], (SGU_BLOCK, SGU_GROUPS, SGU_GROUP_DIM))
    bsb = bsb.reshape(SGU_BLOCK, D_MODEL)
    w_r = jnp.pad(w_router, ((0, 0), (0, LANES - N_EXPERTS)))
    w_r_hi = w_r.astype(BF16)
    w_r_lo = (w_r - w_r_hi.astype(F32)).astype(BF16)
    b_r = jnp.pad(b_router, (0, LANES - N_EXPERTS))[None, :]

    h, gates = _norm_gates(x2d, g_norm_mix[None, :], w_if, b_if)
    zm = _in_proj(h, w_in_t, 0, b_in[None, :], N_MAIN, 2 * D_MODEL, 3 * D_MODEL, "in_proj_main")
    zg = _in_proj(h, w_in_t, GATE_OFF, b_in[None, GATE_OFF:], 2 * D_MODEL, 0, 0, "in_proj_gate")
    a = _sgu(zm, zg, w_sgu, bsb, g_sgu_ln[None, :], b_sgu_ln[None, :], w_proj_sgu.astype(BF16))
    ym = _mlstm(zm, gates, w_conv, b_conv[None, :], g_mlstm_norm[None, :], bsz, seq)
    x1, h2, logits = _merge(ym, a, zg, x2d, w_proj_mlstm.astype(BF16), w_out.astype(BF16),
                            g_norm_moe[None, :], w_r_hi, w_r_lo, b_r)
    info, gate, cnt = _route(logits)

    counts = cnt[0, :N_EXPERTS].astype(I32)
    (n_tiles, n_xrows, tile_e, tile_occ, tile_row, x_start, y_start,
     zero_rows) = _routing_tables(counts, t)
    rows = _rows(info, x_start, y_start)
    dest_x = rows[:, :TOP_K].reshape(-1)
    dest_y = rows[:, TOP_K:2 * TOP_K].reshape(-1)

    xs = _dispatch(dest_x, zero_rows, h2, n_xrows)
    ys = _experts(tile_e, tile_occ, tile_row, xs, w_gate_up, b_gate_up[:, None, :],
                  w_down, b_down[:, None, :], n_tiles)
    return _combine(dest_y, ys, x1, gate, g_out[None, :])


def kernel(x, g_norm_mix, w_in, b_in, w_conv, b_conv, w_sgu, b_sgu, g_sgu_ln, b_sgu_ln, g_mlstm_norm,
           w_proj_sgu, w_proj_mlstm, w_out, g_norm_moe, w_router, b_router, w_gate_up, b_gate_up,
           w_down, b_down, g_final):
    bsz, seq, d = x.shape
    assert d == D_MODEL and w_in.shape[0] == 1, "single-layer block with d_model 2048"
    out = _layer(x.reshape(bsz * seq, d), bsz, seq, g_norm_mix[0], w_in[0], b_in[0], w_conv[0],
                 b_conv[0], w_sgu[0], b_sgu[0], g_sgu_ln[0], b_sgu_ln[0], g_mlstm_norm[0],
                 w_proj_sgu[0], w_proj_mlstm[0], w_out[0], g_norm_moe[0], w_router[0], b_router[0],
                 w_gate_up[0], b_gate_up[0], w_down[0], b_down[0], g_final)
    return out.reshape(bsz, seq, d)
```

```python
import functools

import jax
import jax.numpy as jnp
from jax import lax
from jax.experimental import pallas as pl
from jax.experimental.pallas import tpu as pltpu

F32 = jnp.float32
BF16 = jnp.bfloat16
I32 = jnp.int32

D_MODEL = 2048
CHUNK = 64
SGU_BLOCK = 128
SGU_GROUPS = 8
SGU_GROUP_DIM = 256
HEADS = 4
HEAD_DIM = 512
CONV_WIDTH = 4
N_EXPERTS = 32
TOP_K = 4
D_FF = 2048
SWIGLU_LIMIT = 7.0
SWIGLU_ALPHA = 1.702
EPS = 1e-6

LANES = 128
SUBLANES = 8
VMEM_LIMIT = 56 * 1024 * 1024

ZB_U, ZB_V, ZB_Q, ZB_K, ZB_VM, ZB_O = range(6)
ZG_A, ZG_M = range(2)
N_MAIN = 6 * D_MODEL
GATE_OFF = N_MAIN + 2 * HEADS

NORM_TM = 512
INPROJ_TM = 1024
INPROJ_TN = 1024
INPROJ_RC = 512
SGU_TM = 256
MLSTM_L = 256
MERGE_TM = 256
MERGE_RC = 256
ROUTE_TM = 256
DISPATCH_TM = 256
MOE_SUB = 128
MOE_CAP = 1280
MOE_SUBS = MOE_CAP // MOE_SUB
MOE_TF = 512
MOE_NF = D_FF // MOE_TF
MOE_TN = 512
MOE_NN = D_MODEL // MOE_TN
COMBINE_TM = 128
ROWS_PACK = 512


def _cparams(sem, vmem_limit=VMEM_LIMIT):
    return pltpu.CompilerParams(dimension_semantics=sem, vmem_limit_bytes=vmem_limit)


def _gelu(x):
    return 0.5 * x * (1.0 + lax.erf(x * (2.0 ** -0.5)))


def _sigmoid(x):
    return 1.0 / (1.0 + jnp.exp(-x))


def _norm_gates_kernel(x_ref, g_ref, wif_ref, bif_ref, h_ref, gates_ref):
    x = x_ref[...]
    ms = jnp.mean(x * x, axis=-1, keepdims=True)
    h = (x * lax.rsqrt(ms + EPS) * g_ref[...]).astype(BF16)
    h_ref[...] = h
    gates_ref[...] = jnp.dot(h, wif_ref[...], preferred_element_type=F32) + bif_ref[...]


def _norm_gates(x, g, wif, bif):
    t = x.shape[0]
    tm = min(NORM_TM, t)
    row = lambda i: (i, 0)
    const = lambda i: (0, 0)
    return pl.pallas_call(
        _norm_gates_kernel,
        grid=(t // tm,),
        in_specs=[
            pl.BlockSpec((tm, D_MODEL), row),
            pl.BlockSpec((1, D_MODEL), const),
            pl.BlockSpec((D_MODEL, LANES), const),
            pl.BlockSpec((1, LANES), const),
        ],
        out_specs=[pl.BlockSpec((tm, D_MODEL), row), pl.BlockSpec((tm, LANES), row)],
        out_shape=[jax.ShapeDtypeStruct((t, D_MODEL), BF16), jax.ShapeDtypeStruct((t, LANES), F32)],
        compiler_params=_cparams(("arbitrary",)),
        name="norm_gates",
    )(x, g, wif, bif)


def _inproj_kernel(h_ref, w_ref, b_ref, z_ref, w_scr, *, n_gelu, n_plain):
    j = pl.program_id(0)

    @pl.when(pl.program_id(1) == 0)
    def _():
        w_scr[...] = w_ref[...].astype(BF16)

    def project(activation):
        tm = h_ref.shape[0]
        rc = min(INPROJ_RC, tm)
        for c in range(tm // rc):
            rows = slice(c * rc, (c + 1) * rc)
            acc = lax.dot_general(h_ref[rows, :], w_scr[...], (((1,), (1,)), ((), ())),
                                  preferred_element_type=F32) + b_ref[...]
            z_ref[rows, :] = activation(acc).astype(z_ref.dtype)

    @pl.when(j < n_gelu)
    def _():
        project(_gelu)

    @pl.when((j >= n_gelu) & (j < n_gelu + n_plain))
    def _():
        project(lambda acc: acc)

    @pl.when(j >= n_gelu + n_plain)
    def _():
        project(_sigmoid)


def _in_proj(h, w_t, row_off, b, n_cols, n_gelu_cols, n_plain_cols, name):
    t = h.shape[0]
    tm, tn = min(INPROJ_TM, t), INPROJ_TN
    assert row_off % SUBLANES == 0
    return pl.pallas_call(
        functools.partial(_inproj_kernel, n_gelu=n_gelu_cols // tn, n_plain=n_plain_cols // tn),
        grid=(n_cols // tn, t // tm),
        in_specs=[
            pl.BlockSpec((tm, D_MODEL), lambda j, i: (i, 0)),
            pl.BlockSpec((pl.Element(tn), pl.Element(D_MODEL)),
                         lambda j, i: (pl.multiple_of(row_off + j * tn, SUBLANES), 0)),
            pl.BlockSpec((1, tn), lambda j, i: (0, j)),
        ],
        out_specs=pl.BlockSpec((tm, tn), lambda j, i: (i, j)),
        out_shape=jax.ShapeDtypeStruct((t, n_cols), BF16),
        scratch_shapes=[pltpu.VMEM((tn, D_MODEL), BF16)],
        compiler_params=_cparams(("arbitrary", "arbitrary")),
        name=name,
    )(h, w_t, b)


def _sgu_kernel(gu_ref, gv_ref, sga_ref, ws_ref, bsb_ref, lng_ref, lnb_ref, pa_ref, out_ref,
                ya_scr):
    tm = gu_ref.shape[0]
    v = gv_ref[...].astype(F32)
    mu = jnp.mean(v, axis=-1, keepdims=True)
    vc = v - mu
    var = jnp.mean(vc * vc, axis=-1, keepdims=True)
    vln = (vc * lax.rsqrt(var + EPS) * lng_ref[...] + lnb_ref[...]).astype(BF16)

    t_id = lax.broadcasted_iota(I32, (SGU_BLOCK, SGU_BLOCK), 0) // CHUNK
    s_id = lax.broadcasted_iota(I32, (SGU_BLOCK, SGU_BLOCK), 1) // CHUNK
    causal = s_id <= t_id
    for g in range(SGU_GROUPS):
        w = jnp.where(causal, ws_ref[g], 0.0).astype(BF16)
        cols = slice(g * SGU_GROUP_DIM, (g + 1) * SGU_GROUP_DIM)
        bias = bsb_ref[:, cols]
        for blk in range(tm // SGU_BLOCK):
            rows = slice(blk * SGU_BLOCK, (blk + 1) * SGU_BLOCK)
            mixed = jnp.dot(w, vln[rows, cols], preferred_element_type=F32) + bias
            ya_scr[rows, cols] = (gu_ref[rows, cols].astype(F32) * mixed).astype(BF16)

    proj = jnp.dot(ya_scr[...], pa_ref[...], preferred_element_type=F32)
    out_ref[...] = (sga_ref[...].astype(F32) * proj).astype(out_ref.dtype)


def _sgu(zm, zg, ws, bsb, lng, lnb, pa):
    t = zm.shape[0]
    tm = SGU_TM
    const2 = lambda i: (0, 0)
    return pl.pallas_call(
        _sgu_kernel,
        grid=(t // tm,),
        in_specs=[
            pl.BlockSpec((tm, D_MODEL), lambda i: (i, ZB_U)),
            pl.BlockSpec((tm, D_MODEL), lambda i: (i, ZB_V)),
            pl.BlockSpec((tm, D_MODEL), lambda i: (i, ZG_A)),
            pl.BlockSpec((SGU_GROUPS, SGU_BLOCK, SGU_BLOCK), lambda i: (0, 0, 0)),
            pl.BlockSpec((SGU_BLOCK, D_MODEL), const2),
            pl.BlockSpec((1, D_MODEL), const2),
            pl.BlockSpec((1, D_MODEL), const2),
            pl.BlockSpec((D_MODEL, D_MODEL), const2),
        ],
        out_specs=pl.BlockSpec((tm, D_MODEL), lambda i: (i, 0)),
        out_shape=jax.ShapeDtypeStruct((t, D_MODEL), BF16),
        scratch_shapes=[pltpu.VMEM((tm, D_MODEL), BF16)],
        compiler_params=_cparams(("arbitrary",)),
        name="sgu",
    )(zm, zm, zg, ws, bsb, lng, lnb, pa)


def _mlstm_kernel(q_ref, k_ref, v_ref, so_ref, gates_ref, wc_ref, bc_ref, gn_ref, out_ref,
                  hist, ct_scr, n_scr, m_scr):
    L = q_ref.shape[0]
    c = pl.program_id(1)

    @pl.when(c == 0)
    def _():
        hist[0:SUBLANES, :] = jnp.zeros((SUBLANES, 2 * D_MODEL), F32)
        ct_scr[...] = jnp.zeros_like(ct_scr)
        n_scr[...] = jnp.zeros_like(n_scr)
        m_scr[...] = jnp.zeros_like(m_scr)

    hist[SUBLANES:SUBLANES + L, 0:D_MODEL] = q_ref[...].astype(F32)
    hist[SUBLANES:SUBLANES + L, D_MODEL:2 * D_MODEL] = k_ref[...].astype(F32)
    qk = bc_ref[...] + wc_ref[CONV_WIDTH - 1:CONV_WIDTH, :] * hist[SUBLANES:SUBLANES + L, :]
    for j in range(CONV_WIDTH - 1):
        off = SUBLANES - (CONV_WIDTH - 1) + j
        qk = qk + wc_ref[j:j + 1, :] * hist[off:off + L, :]
    qk = qk * _sigmoid(qk)
    hist[0:SUBLANES, :] = hist[L:L + SUBLANES, :]

    gts = gates_ref[...]
    lf = jnp.minimum(gts, 0.0) - jnp.log(1.0 + jnp.exp(-jnp.abs(gts)))
    row = lax.broadcasted_iota(I32, (L, L), 0)
    col = lax.broadcasted_iota(I32, (L, L), 1)
    tril = col <= row
    tri = jnp.where(tril, 1.0, 0.0).astype(BF16)
    lf_hi = lf.astype(BF16)
    lf_lo = (lf - lf_hi.astype(F32)).astype(BF16)
    bcum = (jnp.dot(tri, lf_hi, preferred_element_type=F32)
            + jnp.dot(tri, lf_lo, preferred_element_type=F32))
    bcum_t = bcum.T
    gts_t = gts.T

    scale = HEAD_DIM ** -0.5
    for h in range(HEADS):
        cols = slice(h * HEAD_DIM, (h + 1) * HEAD_DIM)
        q = qk[:, h * HEAD_DIM:(h + 1) * HEAD_DIM] * scale
        k = qk[:, D_MODEL + h * HEAD_DIM:D_MODEL + (h + 1) * HEAD_DIM]
        qb = q.astype(BF16)
        kb = k.astype(BF16)
        vb = v_ref[:, cols]
        bc = bcum[:, HEADS + h:HEADS + h + 1]
        bc_r = bcum_t[HEADS + h:HEADS + h + 1, :]
        ig_c = gts[:, h:h + 1]
        ig_r = gts_t[h:h + 1, :]
        m_st = m_scr[h]
        n_st = n_scr[h]

        dlog = jnp.where(tril, bc - bc_r + ig_r, -jnp.inf)
        m_inter = bc + m_st
        m_t = jnp.maximum(m_inter, jnp.max(dlog, axis=-1, keepdims=True))
        p = jnp.exp(dlog - m_t)
        s = lax.dot_general(qb, kb, (((1,), (1,)), ((), ())), preferred_element_type=F32)
        sc = s * p
        inter = jnp.exp(m_inter - m_t)
        ctb = ct_scr[h].astype(BF16)
        num = (jnp.dot(sc.astype(BF16), vb, preferred_element_type=F32)
               + inter * jnp.dot(qb, ctb, preferred_element_type=F32))
        den = (jnp.sum(sc, axis=-1, keepdims=True)
               + inter * jnp.sum(q * n_st, axis=-1, keepdims=True))
        hh = num / jnp.maximum(jnp.abs(den), jnp.exp(-m_t))
        hn = hh * lax.rsqrt(jnp.mean(hh * hh, axis=-1, keepdims=True) + EPS) * gn_ref[:, cols]
        out_ref[:, cols] = (hn * so_ref[:, cols].astype(F32)).astype(out_ref.dtype)

        g_tot = bc[L - 1:L, :]
        a = g_tot - bc + ig_c
        m_new = jnp.maximum(g_tot + m_st, jnp.max(a, axis=0, keepdims=True))
        wa = jnp.exp(a - m_new)
        decay = jnp.exp(g_tot + m_st - m_new)
        wv = (wa * vb.astype(F32)).astype(BF16)
        upd = jnp.dot(k.T.astype(BF16), wv, preferred_element_type=F32)
        ct_scr[h] = decay * ct_scr[h] + upd
        n_scr[h] = decay * n_st + jnp.sum(wa * k, axis=0, keepdims=True)
        m_scr[h] = m_new


def _mlstm(zm, gates, wc, bc, gn, bsz, seq):
    t = zm.shape[0]
    L = MLSTM_L
    nc = seq // L
    zspec = lambda blk: pl.BlockSpec((L, D_MODEL), lambda b, c: (b * nc + c, blk))
    const2 = lambda b, c: (0, 0)
    return pl.pallas_call(
        _mlstm_kernel,
        grid=(bsz, nc),
        in_specs=[
            zspec(ZB_Q), zspec(ZB_K), zspec(ZB_VM), zspec(ZB_O),
            pl.BlockSpec((L, LANES), lambda b, c: (b * nc + c, 0)),
            pl.BlockSpec((CONV_WIDTH, 2 * D_MODEL), const2),
            pl.BlockSpec((1, 2 * D_MODEL), const2),
            pl.BlockSpec((1, D_MODEL), const2),
        ],
        out_specs=pl.BlockSpec((L, D_MODEL), lambda b, c: (b * nc + c, 0)),
        out_shape=jax.ShapeDtypeStruct((t, D_MODEL), BF16),
        scratch_shapes=[
            pltpu.VMEM((L + 2 * SUBLANES, 2 * D_MODEL), F32),
            pltpu.VMEM((HEADS, HEAD_DIM, HEAD_DIM), F32),
            pltpu.VMEM((HEADS, 1, HEAD_DIM), F32),
            pltpu.VMEM((HEADS, 1, 1), F32),
        ],
        compiler_params=_cparams(("arbitrary", "arbitrary")),
        name="mlstm",
    )(zm, zm, zm, zm, gates, wc, bc, gn)


def _merge_kernel(ym_ref, a_ref, sgm_ref, x_ref, pm_ref, wo_ref, g2_ref, wrh_ref, wrl_ref, br_ref,
                  x1_ref, h2_ref, lg_ref):
    tm = x_ref.shape[0]
    rc = min(MERGE_RC, tm)
    for c in range(tm // rc):
        rows = slice(c * rc, (c + 1) * rc)
        proj = jnp.dot(ym_ref[rows, :], pm_ref[...], preferred_element_type=F32)
        merged = a_ref[rows, :].astype(F32) + sgm_ref[rows, :].astype(F32) * proj
        x1 = x_ref[rows, :] + jnp.dot(merged.astype(BF16), wo_ref[...],
                                      preferred_element_type=F32)
        x1_ref[rows, :] = x1
        h2 = x1 * lax.rsqrt(jnp.mean(x1 * x1, axis=-1, keepdims=True) + EPS) * g2_ref[...]
        h2_ref[rows, :] = h2
        h_hi = h2.astype(BF16)
        h_lo = (h2 - h_hi.astype(F32)).astype(BF16)
        lg_ref[rows, :] = (jnp.dot(h_hi, wrh_ref[...], preferred_element_type=F32)
                           + (jnp.dot(h_lo, wrh_ref[...], preferred_element_type=F32)
                              + jnp.dot(h_hi, wrl_ref[...], preferred_element_type=F32))
                           + br_ref[...])


def _merge(ym, a, zg, x, pm, wo, g2, wr_hi, wr_lo, br):
    t = x.shape[0]
    tm = MERGE_TM
    const2 = lambda i: (0, 0)
    row = lambda i: (i, 0)
    return pl.pallas_call(
        _merge_kernel,
        grid=(t // tm,),
        in_specs=[
            pl.BlockSpec((tm, D_MODEL), row),
            pl.BlockSpec((tm, D_MODEL), row),
            pl.BlockSpec((tm, D_MODEL), lambda i: (i, ZG_M)),
            pl.BlockSpec((tm, D_MODEL), row),
            pl.BlockSpec((D_MODEL, D_MODEL), const2),
            pl.BlockSpec((D_MODEL, D_MODEL), const2),
            pl.BlockSpec((1, D_MODEL), const2),
            pl.BlockSpec((D_MODEL, LANES), const2),
            pl.BlockSpec((D_MODEL, LANES), const2),
            pl.BlockSpec((1, LANES), const2),
        ],
        out_specs=[
            pl.BlockSpec((tm, D_MODEL), row),
            pl.BlockSpec((tm, D_MODEL), row),
            pl.BlockSpec((tm, LANES), row),
        ],
        out_shape=[
            jax.ShapeDtypeStruct((t, D_MODEL), F32),
            jax.ShapeDtypeStruct((t, D_MODEL), F32),
            jax.ShapeDtypeStruct((t, LANES), F32),
        ],
        compiler_params=_cparams(("arbitrary",)),
        name="merge",
    )(ym, a, zg, x, pm, wo, g2, wr_hi, wr_lo, br)


def _route_kernel(lg_ref, info_ref, gate_ref, cnt_ref, cnt_scr):
    tm = lg_ref.shape[0]
    i = pl.program_id(0)

    @pl.when(i == 0)
    def _():
        cnt_scr[...] = jnp.zeros_like(cnt_scr)

    lane = lax.broadcasted_iota(I32, (tm, LANES), 1)
    lane_f = lane.astype(F32)
    vals = jnp.where(lane < N_EXPERTS, lg_ref[...], -jnp.inf)
    tops, ids, hots = [], [], []
    for _ in range(TOP_K):
        m = jnp.max(vals, axis=-1, keepdims=True)
        idx_f = jnp.min(jnp.where(vals == m, lane_f, float(LANES)), axis=-1, keepdims=True)
        idx = idx_f.astype(I32)
        hot = lane == idx
        tops.append(m)
        ids.append(idx)
        hots.append(hot)
        vals = jnp.where(hot, -jnp.inf, vals)

    exps = [jnp.exp(tv - tops[0]) for tv in tops]
    denom = exps[0] + exps[1] + exps[2] + exps[3]

    sel = jnp.zeros((tm, LANES), F32)
    for hot in hots:
        sel = sel + jnp.where(hot, 1.0, 0.0)
    r = lax.broadcasted_iota(I32, (tm, tm), 0)
    c = lax.broadcasted_iota(I32, (tm, tm), 1)
    strict = jnp.where(c < r, 1.0, 0.0).astype(BF16)
    ahead = jnp.dot(strict, sel.astype(BF16), preferred_element_type=F32) + cnt_scr[...]

    info = jnp.zeros((tm, LANES), I32)
    gate = jnp.zeros((tm, LANES), F32)
    for k in range(TOP_K):
        pos = jnp.sum(jnp.where(hots[k], ahead, 0.0), axis=-1, keepdims=True).astype(I32)
        info = jnp.where(lane == k, ids[k], info)
        info = jnp.where(lane == TOP_K + k, pos, info)
        gate = jnp.where(lane == k, exps[k] / denom, gate)
    info_ref[...] = info
    gate_ref[...] = gate
    cnt_scr[...] = cnt_scr[...] + jnp.sum(sel, axis=0, keepdims=True)
    cnt_ref[...] = jnp.broadcast_to(cnt_scr[...], cnt_ref.shape)


def _route(logits):
    t = logits.shape[0]
    tm = ROUTE_TM
    row = lambda i: (i, 0)
    return pl.pallas_call(
        _route_kernel,
        grid=(t // tm,),
        in_specs=[pl.BlockSpec((tm, LANES), row)],
        out_specs=[
            pl.BlockSpec((tm, LANES), row),
            pl.BlockSpec((tm, LANES), row),
            pl.BlockSpec((SUBLANES, LANES), lambda i: (0, 0)),
        ],
        out_shape=[
            jax.ShapeDtypeStruct((t, LANES), I32),
            jax.ShapeDtypeStruct((t, LANES), F32),
            jax.ShapeDtypeStruct((SUBLANES, LANES), F32),
        ],
        scratch_shapes=[pltpu.VMEM((1, LANES), F32)],
        compiler_params=_cparams(("arbitrary",)),
        name="route",
    )(logits)


def _rows_kernel(info_ref, start_ref, rows_ref):
    tm = info_ref.shape[0]
    lane = lax.broadcasted_iota(I32, (tm, LANES), 1)
    info = info_ref[...]
    rows = jnp.zeros((tm, LANES), I32)
    for k in range(TOP_K):
        hot = lane == info[:, k:k + 1]
        pos = info[:, TOP_K + k:TOP_K + k + 1]
        packed = jnp.sum(jnp.where(hot, start_ref[...], 0.0), axis=-1, keepdims=True).astype(I32)
        x_row = jnp.bitwise_and(packed, ROWS_PACK - 1) * MOE_SUB + pos
        y_row = jnp.right_shift(packed, ROWS_PACK.bit_length() - 1) * MOE_CAP + pos
        rows = jnp.where(lane == k, x_row, rows)
        rows = jnp.where(lane == TOP_K + k, y_row, rows)
    rows_ref[...] = rows


def _rows(info, x_start, y_start, n_subs):
    t = info.shape[0]
    tm = ROUTE_TM
    assert n_subs < ROWS_PACK, "sub-tile indices must fit below the packing factor"
    packed = (x_start // MOE_SUB + ROWS_PACK * (y_start // MOE_CAP)).astype(F32)
    packed = jnp.pad(packed, (0, LANES - N_EXPERTS))[None, :]
    row = lambda i: (i, 0)
    return pl.pallas_call(
        _rows_kernel,
        grid=(t // tm,),
        in_specs=[pl.BlockSpec((tm, LANES), row), pl.BlockSpec((1, LANES), lambda i: (0, 0))],
        out_specs=pl.BlockSpec((tm, LANES), row),
        out_shape=jax.ShapeDtypeStruct((t, LANES), I32),
        compiler_params=_cparams(("arbitrary",)),
        name="rows",
    )(info, packed)


def _dispatch_kernel(dest_ref, zero_row_ref, h2_ref, xs_hbm, zbuf, sem, zsem, *, n_zero):
    tm = h2_ref.shape[0]
    i = pl.program_id(0)

    def zero_copy(j):
        row0 = pl.multiple_of(zero_row_ref[j], MOE_SUB)
        return pltpu.make_async_copy(zbuf, xs_hbm.at[pl.ds(row0, MOE_SUB)], zsem)

    @pl.when(i == 0)
    def _():
        zbuf[...] = jnp.zeros_like(zbuf)

        def start(j, carry):
            @pl.when(zero_row_ref[j] >= 0)
            def _():
                zero_copy(j).start()
            return carry

        def wait(j, carry):
            @pl.when(zero_row_ref[j] >= 0)
            def _():
                zero_copy(j).wait()
            return carry

        lax.fori_loop(0, n_zero, start, 0)
        lax.fori_loop(0, n_zero, wait, 0)

    for r in range(tm):
        for k in range(TOP_K):
            d = dest_ref[(i * tm + r) * TOP_K + k]
            pltpu.make_async_copy(h2_ref.at[pl.ds(r, 1)], xs_hbm.at[pl.ds(d, 1)],
                                  sem).start(priority=k % 2)

    for _ in range(TOP_K):
        pltpu.make_async_copy(h2_ref, xs_hbm.at[pl.ds(0, tm)], sem).wait()


def _dispatch(dest, zero_rows, h2, n_rows):
    t = h2.shape[0]
    tm = DISPATCH_TM
    grid_spec = pltpu.PrefetchScalarGridSpec(
        num_scalar_prefetch=2,
        grid=(t // tm,),
        in_specs=[pl.BlockSpec((tm, D_MODEL), lambda i, d, z: (i, 0))],
        out_specs=pl.BlockSpec(memory_space=pl.ANY),
        scratch_shapes=[
            pltpu.VMEM((MOE_SUB, D_MODEL), F32),
            pltpu.SemaphoreType.DMA,
            pltpu.SemaphoreType.DMA,
        ],
    )
    return pl.pallas_call(
        functools.partial(_dispatch_kernel, n_zero=zero_rows.shape[0]),
        grid_spec=grid_spec,
        out_shape=jax.ShapeDtypeStruct((n_rows, D_MODEL), F32),
        compiler_params=_cparams(("arbitrary",)),
        name="dispatch",
    )(dest, zero_rows, h2)


def _experts_kernel(tile_e_ref, tile_occ_ref, tile_row_ref,
                    xs_hbm, wg_ref, wu_ref, bg_ref, bu_ref, wd_ref, bd_ref, out_ref,
                    xbuf, xb_scr, act_scr, sem, *, n_tiles):
    del tile_e_ref
    i = pl.program_id(0)
    j = pl.program_id(1)
    occ = tile_occ_ref[i]

    def sub_rows(sub):
        return pl.ds(pl.multiple_of(sub * MOE_SUB, MOE_SUB), MOE_SUB)

    def sub_copy(tile, sub):
        row0 = pl.multiple_of(tile_row_ref[tile] + sub * MOE_SUB, MOE_SUB)
        return pltpu.make_async_copy(xs_hbm.at[pl.ds(row0, MOE_SUB)], xbuf.at[sub_rows(sub)], sem)

    def start_load(tile):
        def body(sub, carry):
            sub_copy(tile, sub).start()
            return carry
        lax.fori_loop(0, tile_occ_ref[tile], body, 0)

    @pl.when(j == 0)
    def _():
        @pl.when(i == 0)
        def _():
            start_load(0)

        def wait_sub(sub, carry):
            sub_copy(i, sub).wait()
            return carry

        def round_sub(sub, carry):
            xb_scr[sub_rows(sub), :] = xbuf[sub_rows(sub), :].astype(BF16)
            return carry

        lax.fori_loop(0, occ, wait_sub, 0)
        lax.fori_loop(0, occ, round_sub, 0)

        @pl.when(i + 1 < n_tiles)
        def _():
            start_load(i + 1)

    def for_each_sub(weights, fn):
        def quad(p, carry):
            w = weights()
            for u in range(4):
                fn(4 * p + u, *w)
            return carry

        lax.fori_loop(0, occ // 4, quad, 0)
        done = (occ // 4) * 4

        @pl.when(occ % 4 >= 2)
        def _():
            w = weights()
            fn(done, *w)
            fn(done + 1, *w)

        @pl.when(occ % 2 == 1)
        def _():
            fn(occ - 1, *weights())

    @pl.when(j < MOE_NF)
    def _():
        def weights():
            return wg_ref[...].astype(BF16), wu_ref[...].astype(BF16)

        def gate_up(sub, wg, wu):
            rows = sub_rows(sub)
            xb = xb_scr[rows, :]
            gate = jnp.dot(xb, wg, preferred_element_type=F32) + bg_ref[...]
            up = jnp.dot(xb, wu, preferred_element_type=F32) + bu_ref[...]
            gate = jnp.minimum(gate, SWIGLU_LIMIT)
            up = jnp.clip(up, -SWIGLU_LIMIT, SWIGLU_LIMIT)
            act = (up + 1.0) * (gate * _sigmoid(SWIGLU_ALPHA * gate))
            act_scr[j, rows, :] = act.astype(BF16)

        for_each_sub(weights, gate_up)

    @pl.when(j >= MOE_NF)
    def _():
        def weights():
            return (wd_ref[...].astype(BF16),)

        def down(sub, wd):
            rows = sub_rows(sub)
            y = bd_ref[...] + jnp.dot(act_scr[0, rows, :], wd[0:MOE_TF, :],
                                      preferred_element_type=F32)
            for c in range(1, MOE_NF):
                y = y + jnp.dot(act_scr[c, rows, :], wd[c * MOE_TF:(c + 1) * MOE_TF, :],
                                preferred_element_type=F32)
            out_ref[rows, :] = y

        for_each_sub(weights, down)

    @pl.when(j >= MOE_NF)
    def _():
        def zero_sub(sub, carry):
            out_ref[sub_rows(sub), :] = jnp.zeros((MOE_SUB, MOE_TN), F32)
            return carry

        lax.fori_loop(occ, MOE_SUBS, zero_sub, 0)


def _experts(tile_e, tile_occ, tile_row, xs, wgu, bgu, wd, bd, n_tiles):
    up_off = D_FF // MOE_TF

    def fa(i, j, to):
        return jnp.where(to[i] > 0, jnp.minimum(j, MOE_NF - 1), MOE_NF - 1)

    def fb(i, j, to):
        return jnp.where(to[i] > 0, jnp.maximum(j - MOE_NF, 0), MOE_NN - 1)

    def gate_up_block(i, j, te, to):
        nxt = jnp.minimum(i + 1, n_tiles - 1)
        ahead = j >= MOE_NF
        tile = jnp.where(ahead, nxt, i)
        return te[tile], fa(tile, jnp.where(ahead, 0, j), to)

    def gate_map(i, j, te, to, tr):
        e, chunk = gate_up_block(i, j, te, to)
        return (e, 0, chunk)

    def up_map(i, j, te, to, tr):
        e, chunk = gate_up_block(i, j, te, to)
        return (e, 0, up_off + chunk)

    def down_map(i, j, te, to, tr):
        prev = jnp.maximum(i - 1, 0)
        hold = (j < MOE_NF) & (i > 0) & (to[i] > 0)
        return (jnp.where(hold, te[prev], te[i]), 0, jnp.where(hold, MOE_NN - 1, fb(i, j, to)))

    def out_map(i, j, te, to, tr):
        return (i, jnp.maximum(j - MOE_NF, 0))

    grid_spec = pltpu.PrefetchScalarGridSpec(
        num_scalar_prefetch=3,
        grid=(n_tiles, MOE_NF + MOE_NN),
        in_specs=[
            pl.BlockSpec(memory_space=pl.ANY),
            pl.BlockSpec((None, D_MODEL, MOE_TF), gate_map),
            pl.BlockSpec((None, D_MODEL, MOE_TF), up_map),
            pl.BlockSpec((None, 1, MOE_TF), gate_map),
            pl.BlockSpec((None, 1, MOE_TF), up_map),
            pl.BlockSpec((None, D_FF, MOE_TN), down_map),
            pl.BlockSpec((None, 1, MOE_TN), down_map),
        ],
        out_specs=pl.BlockSpec((MOE_CAP, MOE_TN), out_map),
        scratch_shapes=[
            pltpu.VMEM((MOE_CAP, D_MODEL), F32),
            pltpu.VMEM((MOE_CAP, D_MODEL), BF16),
            pltpu.VMEM((MOE_NF, MOE_CAP, MOE_TF), BF16),
            pltpu.SemaphoreType.DMA,
        ],
    )
    return pl.pallas_call(
        functools.partial(_experts_kernel, n_tiles=n_tiles),
        grid_spec=grid_spec,
        out_shape=jax.ShapeDtypeStruct((n_tiles * MOE_CAP, D_MODEL), F32),
        compiler_params=_cparams(("arbitrary", "arbitrary")),
        name="experts",
    )(tile_e, tile_occ, tile_row, xs, wgu, wgu, bgu, bgu, wd, bd)


def _combine_kernel(dest_ref, ys_hbm, x1_ref, gate_ref, gf_ref, out_ref, buf, sem, *, n_tiles):
    tm = x1_ref.shape[0]
    i = pl.program_id(0)
    slot = i % 2

    def start_gather(tile, s):
        for r in range(tm):
            for k in range(TOP_K):
                d = dest_ref[(tile * tm + r) * TOP_K + k]
                pltpu.make_async_copy(ys_hbm.at[pl.ds(d, 1)], buf.at[s, k, pl.ds(r, 1)],
                                      sem.at[s]).start(priority=k % 2)

    @pl.when(i == 0)
    def _():
        start_gather(0, 0)

    for k in range(TOP_K):
        pltpu.make_async_copy(ys_hbm.at[pl.ds(0, tm)], buf.at[slot, k], sem.at[slot]).wait()

    for s in range(2):
        @pl.when((i + 1 < n_tiles) & (slot == 1 - s))
        def _():
            start_gather(i + 1, s)

    x2 = x1_ref[...]
    for k in range(TOP_K):
        x2 = x2 + gate_ref[:, k:k + 1] * buf[slot, k]
    out_ref[...] = x2 * lax.rsqrt(jnp.mean(x2 * x2, axis=-1, keepdims=True) + EPS) * gf_ref[...]


def _combine(dest, ys, x1, gate, gf):
    t = x1.shape[0]
    tm = COMBINE_TM
    n_tiles = t // tm
    grid_spec = pltpu.PrefetchScalarGridSpec(
        num_scalar_prefetch=1,
        grid=(n_tiles,),
        in_specs=[
            pl.BlockSpec(memory_space=pl.ANY),
            pl.BlockSpec((tm, D_MODEL), lambda i, d: (i, 0)),
            pl.BlockSpec((tm, LANES), lambda i, d: (i, 0)),
            pl.BlockSpec((1, D_MODEL), lambda i, d: (0, 0)),
        ],
        out_specs=pl.BlockSpec((tm, D_MODEL), lambda i, d: (i, 0)),
        scratch_shapes=[
            pltpu.VMEM((2, TOP_K, tm, D_MODEL), F32),
            pltpu.SemaphoreType.DMA((2,)),
        ],
    )
    return pl.pallas_call(
        functools.partial(_combine_kernel, n_tiles=n_tiles),
        grid_spec=grid_spec,
        out_shape=jax.ShapeDtypeStruct((t, D_MODEL), F32),
        compiler_params=_cparams(("arbitrary",)),
        name="combine",
    )(dest, ys, x1, gate, gf)


def _routing_tables(counts, n_tokens):
    n_tiles = N_EXPERTS + (n_tokens * TOP_K) // MOE_CAP
    n_subs = N_EXPERTS + (n_tokens * TOP_K) // MOE_SUB
    subs_per = (counts + MOE_SUB - 1) // MOE_SUB
    sub_end = jnp.cumsum(subs_per)
    x_start = (sub_end - subs_per) * MOE_SUB
    tiles_per = (counts + MOE_CAP - 1) // MOE_CAP
    tile_end = jnp.cumsum(tiles_per)
    tile_start = tile_end - tiles_per
    y_start = tile_start * MOE_CAP

    tile_ids = jnp.arange(n_tiles, dtype=I32)
    n_active = tile_end[-1]
    last = jnp.maximum(n_active - 1, 0)
    active = tile_ids < n_active
    tile_eff = jnp.minimum(tile_ids, last)
    tile_e = jnp.sum((tile_eff[:, None] >= tile_end[None, :]).astype(I32), axis=1)
    tile_e = jnp.minimum(tile_e, N_EXPERTS - 1)
    in_expert = tile_eff - tile_start[tile_e]
    rows_in = counts[tile_e] - in_expert * MOE_CAP
    occ = jnp.clip((rows_in + MOE_SUB - 1) // MOE_SUB, 0, MOE_SUBS)
    tile_occ = jnp.where(active, occ, 0).astype(I32)
    tile_row = (x_start[tile_e] + in_expert * MOE_CAP).astype(I32)

    last_sub = jnp.where(counts > 0, x_start + (subs_per - 1) * MOE_SUB, -1)
    tail_ids = sub_end[-1] + jnp.arange(N_EXPERTS, dtype=I32)
    tail = jnp.where(tail_ids < n_subs, tail_ids * MOE_SUB, -1)
    zero_rows = jnp.concatenate([last_sub, tail]).astype(I32)
    return (n_tiles, n_subs * MOE_SUB, tile_e.astype(I32), tile_occ, tile_row,
            x_start, y_start, zero_rows)


def _layer(x2d, bsz, seq, g_norm_mix, w_in, b_in, w_conv, b_conv, w_sgu, b_sgu, g_sgu_ln, b_sgu_ln,
           g_mlstm_norm, w_proj_sgu, w_proj_mlstm, w_out, g_norm_moe, w_router, b_router,
           w_gate_up, b_gate_up, w_down, b_down, g_out):
    t = x2d.shape[0]

    pad = LANES - 2 * HEADS
    w_if = jnp.pad(w_in[:, N_MAIN:GATE_OFF], ((0, 0), (0, pad))).astype(BF16)
    b_if = jnp.pad(b_in[N_MAIN:GATE_OFF], (0, pad))[None, :]
    w_in_t = w_in.T
    bsb = jnp.broadcast_to(b_sgu.T[:, :, None], (SGU_BLOCK, SGU_GROUPS, SGU_GROUP_DIM))
    bsb = bsb.reshape(SGU_BLOCK, D_MODEL)
    w_r = jnp.pad(w_router, ((0, 0), (0, LANES - N_EXPERTS)))
    w_r_hi = w_r.astype(BF16)
    w_r_lo = (w_r - w_r_hi.astype(F32)).astype(BF16)
    b_r = jnp.pad(b_router, (0, LANES - N_EXPERTS))[None, :]

    h, gates = _norm_gates(x2d, g_norm_mix[None, :], w_if, b_if)
    zm = _in_proj(h, w_in_t, 0, b_in[None, :], N_MAIN, 2 * D_MODEL, 3 * D_MODEL, "in_proj_main")
    zg = _in_proj(h, w_in_t, GATE_OFF, b_in[None, GATE_OFF:], 2 * D_MODEL, 0, 0, "in_proj_gate")
    a = _sgu(zm, zg, w_sgu, bsb, g_sgu_ln[None, :], b_sgu_ln[None, :], w_proj_sgu.astype(BF16))
    ym = _mlstm(zm, gates, w_conv, b_conv[None, :], g_mlstm_norm[None, :], bsz, seq)
    x1, h2, logits = _merge(ym, a, zg, x2d, w_proj_mlstm.astype(BF16), w_out.astype(BF16),
                            g_norm_moe[None, :], w_r_hi, w_r_lo, b_r)
    info, gate, cnt = _route(logits)

    counts = cnt[0, :N_EXPERTS].astype(I32)
    (n_tiles, n_xrows, tile_e, tile_occ, tile_row, x_start, y_start,
     zero_rows) = _routing_tables(counts, t)
    rows = _rows(info, x_start, y_start, n_xrows // MOE_SUB)
    dest_x = rows[:, :TOP_K].reshape(-1)
    dest_y = rows[:, TOP_K:2 * TOP_K].reshape(-1)

    xs = _dispatch(dest_x, zero_rows, h2, n_xrows)
    ys = _experts(tile_e, tile_occ, tile_row, xs, w_gate_up, b_gate_up[:, None, :],
                  w_down, b_down[:, None, :], n_tiles)
    return _combine(dest_y, ys, x1, gate, g_out[None, :])


def kernel(x, g_norm_mix, w_in, b_in, w_conv, b_conv, w_sgu, b_sgu, g_sgu_ln, b_sgu_ln, g_mlstm_norm,
           w_proj_sgu, w_proj_mlstm, w_out, g_norm_moe, w_router, b_router, w_gate_up, b_gate_up,
           w_down, b_down, g_final):
    bsz, seq, d = x.shape
    assert d == D_MODEL and w_in.shape[0] == 1, "single-layer block with d_model 2048"
    out = _layer(x.reshape(bsz * seq, d), bsz, seq, g_norm_mix[0], w_in[0], b_in[0], w_conv[0],
                 b_conv[0], w_sgu[0], b_sgu[0], g_sgu_ln[0], b_sgu_ln[0], g_mlstm_norm[0],
                 w_proj_sgu[0], w_proj_mlstm[0], w_out[0], g_norm_moe[0], w_router[0], b_router[0],
                 w_gate_up[0], b_gate_up[0], w_down[0], b_down[0], g_final)
    return out.reshape(bsz, seq, d)
```

```python
import functools

import jax
import jax.numpy as jnp
from jax import lax
from jax.experimental import pallas as pl
from jax.experimental.pallas import tpu as pltpu

F32 = jnp.float32
BF16 = jnp.bfloat16
I32 = jnp.int32

D_MODEL = 2048
CHUNK = 64
SGU_BLOCK = 128
SGU_GROUPS = 8
SGU_GROUP_DIM = 256
HEADS = 4
HEAD_DIM = 512
CONV_WIDTH = 4
N_EXPERTS = 32
TOP_K = 4
D_FF = 2048
SWIGLU_LIMIT = 7.0
SWIGLU_ALPHA = 1.702
EPS = 1e-6

LANES = 128
SUBLANES = 8
VMEM_LIMIT = 56 * 1024 * 1024

ZB_U, ZB_V, ZB_Q, ZB_K, ZB_VM, ZB_O = range(6)
ZG_A, ZG_M = range(2)
N_MAIN = 6 * D_MODEL
GATE_OFF = N_MAIN + 2 * HEADS

NORM_TM = 512
INPROJ_TM = 1024
INPROJ_TN = 1024
INPROJ_RC = 512
SGU_TM = 256
MLSTM_L = 256
MERGE_TM = 256
ROUTE_TM = 256
ROWS_TM = 1024
DISPATCH_TM = 256
MOE_SUB = 128
MOE_CAP = 1280
MOE_SUBS = MOE_CAP // MOE_SUB
MOE_TF = 512
MOE_NF = D_FF // MOE_TF
MOE_TN = 512
MOE_NN = D_MODEL // MOE_TN
COMBINE_TM = 128
ROWS_PACK = 512


def _cparams(sem):
    return pltpu.CompilerParams(dimension_semantics=sem, vmem_limit_bytes=VMEM_LIMIT)


def _gelu(x):
    return 0.5 * x * (1.0 + lax.erf(x * (2.0 ** -0.5)))


def _sigmoid(x):
    return 1.0 / (1.0 + jnp.exp(-x))


def _norm_gates_kernel(x_ref, g_ref, wif_ref, bif_ref, h_ref, gates_ref):
    x = x_ref[...]
    ms = jnp.mean(x * x, axis=-1, keepdims=True)
    h = (x * lax.rsqrt(ms + EPS) * g_ref[...]).astype(BF16)
    h_ref[...] = h
    gates_ref[...] = jnp.dot(h, wif_ref[...], preferred_element_type=F32) + bif_ref[...]


def _norm_gates(x, g, wif, bif):
    t = x.shape[0]
    tm = min(NORM_TM, t)
    row = lambda i: (i, 0)
    const = lambda i: (0, 0)
    return pl.pallas_call(
        _norm_gates_kernel,
        grid=(t // tm,),
        in_specs=[
            pl.BlockSpec((tm, D_MODEL), row),
            pl.BlockSpec((1, D_MODEL), const),
            pl.BlockSpec((D_MODEL, LANES), const),
            pl.BlockSpec((1, LANES), const),
        ],
        out_specs=[pl.BlockSpec((tm, D_MODEL), row), pl.BlockSpec((tm, LANES), row)],
        out_shape=[jax.ShapeDtypeStruct((t, D_MODEL), BF16), jax.ShapeDtypeStruct((t, LANES), F32)],
        compiler_params=_cparams(("arbitrary",)),
        name="norm_gates",
    )(x, g, wif, bif)


def _inproj_kernel(h_ref, w_ref, b_ref, z_ref, w_scr, *, n_gelu, n_plain):
    j = pl.program_id(0)

    @pl.when(pl.program_id(1) == 0)
    def _():
        w_scr[...] = w_ref[...].astype(BF16)

    def project(activation):
        tm = h_ref.shape[0]
        rc = min(INPROJ_RC, tm)
        for c in range(tm // rc):
            rows = slice(c * rc, (c + 1) * rc)
            acc = lax.dot_general(h_ref[rows, :], w_scr[...], (((1,), (1,)), ((), ())),
                                  preferred_element_type=F32) + b_ref[...]
            z_ref[rows, :] = activation(acc).astype(z_ref.dtype)

    @pl.when(j < n_gelu)
    def _():
        project(_gelu)

    @pl.when((j >= n_gelu) & (j < n_gelu + n_plain))
    def _():
        project(lambda acc: acc)

    @pl.when(j >= n_gelu + n_plain)
    def _():
        project(_sigmoid)


def _in_proj(h, w_t, row_off, b, n_cols, n_gelu_cols, n_plain_cols, name):
    t = h.shape[0]
    tm, tn = min(INPROJ_TM, t), INPROJ_TN
    assert row_off % SUBLANES == 0
    return pl.pallas_call(
        functools.partial(_inproj_kernel, n_gelu=n_gelu_cols // tn, n_plain=n_plain_cols // tn),
        grid=(n_cols // tn, t // tm),
        in_specs=[
            pl.BlockSpec((tm, D_MODEL), lambda j, i: (i, 0)),
            pl.BlockSpec((pl.Element(tn), pl.Element(D_MODEL)),
                         lambda j, i: (pl.multiple_of(row_off + j * tn, SUBLANES), 0)),
            pl.BlockSpec((1, tn), lambda j, i: (0, j)),
        ],
        out_specs=pl.BlockSpec((tm, tn), lambda j, i: (i, j)),
        out_shape=jax.ShapeDtypeStruct((t, n_cols), BF16),
        scratch_shapes=[pltpu.VMEM((tn, D_MODEL), BF16)],
        compiler_params=_cparams(("arbitrary", "arbitrary")),
        name=name,
    )(h, w_t, b)


def _sgu_kernel(gu_ref, gv_ref, sga_ref, ws_ref, bsb_ref, lng_ref, lnb_ref, pa_ref, out_ref,
                ya_scr):
    tm = gu_ref.shape[0]
    v = gv_ref[...].astype(F32)
    mu = jnp.mean(v, axis=-1, keepdims=True)
    vc = v - mu
    var = jnp.mean(vc * vc, axis=-1, keepdims=True)
    vln = (vc * lax.rsqrt(var + EPS) * lng_ref[...] + lnb_ref[...]).astype(BF16)

    t_id = lax.broadcasted_iota(I32, (SGU_BLOCK, SGU_BLOCK), 0) // CHUNK
    s_id = lax.broadcasted_iota(I32, (SGU_BLOCK, SGU_BLOCK), 1) // CHUNK
    causal = s_id <= t_id
    for g in range(SGU_GROUPS):
        w = jnp.where(causal, ws_ref[g], 0.0).astype(BF16)
        cols = slice(g * SGU_GROUP_DIM, (g + 1) * SGU_GROUP_DIM)
        bias = bsb_ref[:, cols]
        for blk in range(tm // SGU_BLOCK):
            rows = slice(blk * SGU_BLOCK, (blk + 1) * SGU_BLOCK)
            mixed = jnp.dot(w, vln[rows, cols], preferred_element_type=F32) + bias
            ya_scr[rows, cols] = (gu_ref[rows, cols].astype(F32) * mixed).astype(BF16)

    proj = jnp.dot(ya_scr[...], pa_ref[...], preferred_element_type=F32)
    out_ref[...] = (sga_ref[...].astype(F32) * proj).astype(out_ref.dtype)


def _sgu(zm, zg, ws, bsb, lng, lnb, pa):
    t = zm.shape[0]
    tm = SGU_TM
    const2 = lambda i: (0, 0)
    return pl.pallas_call(
        _sgu_kernel,
        grid=(t // tm,),
        in_specs=[
            pl.BlockSpec((tm, D_MODEL), lambda i: (i, ZB_U)),
            pl.BlockSpec((tm, D_MODEL), lambda i: (i, ZB_V)),
            pl.BlockSpec((tm, D_MODEL), lambda i: (i, ZG_A)),
            pl.BlockSpec((SGU_GROUPS, SGU_BLOCK, SGU_BLOCK), lambda i: (0, 0, 0)),
            pl.BlockSpec((SGU_BLOCK, D_MODEL), const2),
            pl.BlockSpec((1, D_MODEL), const2),
            pl.BlockSpec((1, D_MODEL), const2),
            pl.BlockSpec((D_MODEL, D_MODEL), const2),
        ],
        out_specs=pl.BlockSpec((tm, D_MODEL), lambda i: (i, 0)),
        out_shape=jax.ShapeDtypeStruct((t, D_MODEL), BF16),
        scratch_shapes=[pltpu.VMEM((tm, D_MODEL), BF16)],
        compiler_params=_cparams(("arbitrary",)),
        name="sgu",
    )(zm, zm, zg, ws, bsb, lng, lnb, pa)


def _mlstm_kernel(q_ref, k_ref, v_ref, so_ref, gates_ref, wc_ref, bc_ref, gn_ref, out_ref,
                  hist, ct_scr, n_scr, m_scr):
    L = q_ref.shape[0]
    c = pl.program_id(1)

    @pl.when(c == 0)
    def _():
        hist[0:SUBLANES, :] = jnp.zeros((SUBLANES, 2 * D_MODEL), F32)
        ct_scr[...] = jnp.zeros_like(ct_scr)
        n_scr[...] = jnp.zeros_like(n_scr)
        m_scr[...] = jnp.zeros_like(m_scr)

    hist[SUBLANES:SUBLANES + L, 0:D_MODEL] = q_ref[...].astype(F32)
    hist[SUBLANES:SUBLANES + L, D_MODEL:2 * D_MODEL] = k_ref[...].astype(F32)
    qk = bc_ref[...] + wc_ref[CONV_WIDTH - 1:CONV_WIDTH, :] * hist[SUBLANES:SUBLANES + L, :]
    for j in range(CONV_WIDTH - 1):
        off = SUBLANES - (CONV_WIDTH - 1) + j
        qk = qk + wc_ref[j:j + 1, :] * hist[off:off + L, :]
    qk = qk * _sigmoid(qk)
    hist[0:SUBLANES, :] = hist[L:L + SUBLANES, :]

    gts = gates_ref[...]
    lf = jnp.minimum(gts, 0.0) - jnp.log(1.0 + jnp.exp(-jnp.abs(gts)))
    row = lax.broadcasted_iota(I32, (L, L), 0)
    col = lax.broadcasted_iota(I32, (L, L), 1)
    tril = col <= row
    tri = jnp.where(tril, 1.0, 0.0).astype(BF16)
    lf_hi = lf.astype(BF16)
    lf_lo = (lf - lf_hi.astype(F32)).astype(BF16)
    bcum = (jnp.dot(tri, lf_hi, preferred_element_type=F32)
            + jnp.dot(tri, lf_lo, preferred_element_type=F32))
    bcum_t = bcum.T
    gts_t = gts.T

    scale = HEAD_DIM ** -0.5
    for h in range(HEADS):
        cols = slice(h * HEAD_DIM, (h + 1) * HEAD_DIM)
        q = qk[:, h * HEAD_DIM:(h + 1) * HEAD_DIM] * scale
        k = qk[:, D_MODEL + h * HEAD_DIM:D_MODEL + (h + 1) * HEAD_DIM]
        qb = q.astype(BF16)
        kb = k.astype(BF16)
        vb = v_ref[:, cols]
        bc = bcum[:, HEADS + h:HEADS + h + 1]
        bc_r = bcum_t[HEADS + h:HEADS + h + 1, :]
        ig_c = gts[:, h:h + 1]
        ig_r = gts_t[h:h + 1, :]
        m_st = m_scr[h]
        n_st = n_scr[h]

        dlog = jnp.where(tril, bc - bc_r + ig_r, -jnp.inf)
        m_inter = bc + m_st
        m_t = jnp.maximum(m_inter, jnp.max(dlog, axis=-1, keepdims=True))
        p = jnp.exp(dlog - m_t)
        s = lax.dot_general(qb, kb, (((1,), (1,)), ((), ())), preferred_element_type=F32)
        sc = s * p
        inter = jnp.exp(m_inter - m_t)
        ctb = ct_scr[h].astype(BF16)
        num = (jnp.dot(sc.astype(BF16), vb, preferred_element_type=F32)
               + inter * jnp.dot(qb, ctb, preferred_element_type=F32))
        den = (jnp.sum(sc, axis=-1, keepdims=True)
               + inter * jnp.sum(q * n_st, axis=-1, keepdims=True))
        hh = num / jnp.maximum(jnp.abs(den), jnp.exp(-m_t))
        hn = hh * lax.rsqrt(jnp.mean(hh * hh, axis=-1, keepdims=True) + EPS) * gn_ref[:, cols]
        out_ref[:, cols] = (hn * so_ref[:, cols].astype(F32)).astype(out_ref.dtype)

        g_tot = bc[L - 1:L, :]
        a = g_tot - bc + ig_c
        m_new = jnp.maximum(g_tot + m_st, jnp.max(a, axis=0, keepdims=True))
        wa = jnp.exp(a - m_new)
        decay = jnp.exp(g_tot + m_st - m_new)
        wv = (wa * vb.astype(F32)).astype(BF16)
        upd = jnp.dot(k.T.astype(BF16), wv, preferred_element_type=F32)
        ct_scr[h] = decay * ct_scr[h] + upd
        n_scr[h] = decay * n_st + jnp.sum(wa * k, axis=0, keepdims=True)
        m_scr[h] = m_new


def _mlstm(zm, gates, wc, bc, gn, bsz, seq):
    t = zm.shape[0]
    L = MLSTM_L
    nc = seq // L
    zspec = lambda blk: pl.BlockSpec((L, D_MODEL), lambda b, c: (b * nc + c, blk))
    const2 = lambda b, c: (0, 0)
    return pl.pallas_call(
        _mlstm_kernel,
        grid=(bsz, nc),
        in_specs=[
            zspec(ZB_Q), zspec(ZB_K), zspec(ZB_VM), zspec(ZB_O),
            pl.BlockSpec((L, LANES), lambda b, c: (b * nc + c, 0)),
            pl.BlockSpec((CONV_WIDTH, 2 * D_MODEL), const2),
            pl.BlockSpec((1, 2 * D_MODEL), const2),
            pl.BlockSpec((1, D_MODEL), const2),
        ],
        out_specs=pl.BlockSpec((L, D_MODEL), lambda b, c: (b * nc + c, 0)),
        out_shape=jax.ShapeDtypeStruct((t, D_MODEL), BF16),
        scratch_shapes=[
            pltpu.VMEM((L + 2 * SUBLANES, 2 * D_MODEL), F32),
            pltpu.VMEM((HEADS, HEAD_DIM, HEAD_DIM), F32),
            pltpu.VMEM((HEADS, 1, HEAD_DIM), F32),
            pltpu.VMEM((HEADS, 1, 1), F32),
        ],
        compiler_params=_cparams(("arbitrary", "arbitrary")),
        name="mlstm",
    )(zm, zm, zm, zm, gates, wc, bc, gn)


def _merge_kernel(ym_ref, a_ref, sgm_ref, x_ref, pm_ref, wo_ref, g2_ref, wrh_ref, wrl_ref, br_ref,
                  x1_ref, h2_ref, lg_ref):
    proj = jnp.dot(ym_ref[...], pm_ref[...], preferred_element_type=F32)
    merged = a_ref[...].astype(F32) + sgm_ref[...].astype(F32) * proj
    x1 = x_ref[...] + jnp.dot(merged.astype(BF16), wo_ref[...], preferred_element_type=F32)
    x1_ref[...] = x1
    h2 = x1 * lax.rsqrt(jnp.mean(x1 * x1, axis=-1, keepdims=True) + EPS) * g2_ref[...]
    h2_ref[...] = h2
    h_hi = h2.astype(BF16)
    h_lo = (h2 - h_hi.astype(F32)).astype(BF16)
    lg_ref[...] = (jnp.dot(h_hi, wrh_ref[...], preferred_element_type=F32)
                   + (jnp.dot(h_lo, wrh_ref[...], preferred_element_type=F32)
                      + jnp.dot(h_hi, wrl_ref[...], preferred_element_type=F32))
                   + br_ref[...])


def _merge(ym, a, zg, x, pm, wo, g2, wr_hi, wr_lo, br):
    t = x.shape[0]
    tm = MERGE_TM
    const2 = lambda i: (0, 0)
    row = lambda i: (i, 0)
    return pl.pallas_call(
        _merge_kernel,
        grid=(t // tm,),
        in_specs=[
            pl.BlockSpec((tm, D_MODEL), row),
            pl.BlockSpec((tm, D_MODEL), row),
            pl.BlockSpec((tm, D_MODEL), lambda i: (i, ZG_M)),
            pl.BlockSpec((tm, D_MODEL), row),
            pl.BlockSpec((D_MODEL, D_MODEL), const2),
            pl.BlockSpec((D_MODEL, D_MODEL), const2),
            pl.BlockSpec((1, D_MODEL), const2),
            pl.BlockSpec((D_MODEL, LANES), const2),
            pl.BlockSpec((D_MODEL, LANES), const2),
            pl.BlockSpec((1, LANES), const2),
        ],
        out_specs=[
            pl.BlockSpec((tm, D_MODEL), row),
            pl.BlockSpec((tm, D_MODEL), row),
            pl.BlockSpec((tm, LANES), row),
        ],
        out_shape=[
            jax.ShapeDtypeStruct((t, D_MODEL), F32),
            jax.ShapeDtypeStruct((t, D_MODEL), F32),
            jax.ShapeDtypeStruct((t, LANES), F32),
        ],
        compiler_params=_cparams(("arbitrary",)),
        name="merge",
    )(ym, a, zg, x, pm, wo, g2, wr_hi, wr_lo, br)


def _route_kernel(lg_ref, info_ref, gate_ref, cnt_ref, cnt_scr):
    tm = lg_ref.shape[0]
    i = pl.program_id(0)

    @pl.when(i == 0)
    def _():
        cnt_scr[...] = jnp.zeros_like(cnt_scr)

    lane = lax.broadcasted_iota(I32, (tm, LANES), 1)
    lane_f = lane.astype(F32)
    vals = jnp.where(lane < N_EXPERTS, lg_ref[...], -jnp.inf)
    tops, ids, hots = [], [], []
    for _ in range(TOP_K):
        m = jnp.max(vals, axis=-1, keepdims=True)
        idx_f = jnp.min(jnp.where(vals == m, lane_f, float(LANES)), axis=-1, keepdims=True)
        idx = idx_f.astype(I32)
        hot = lane == idx
        tops.append(m)
        ids.append(idx)
        hots.append(hot)
        vals = jnp.where(hot, -jnp.inf, vals)

    exps = [jnp.exp(tv - tops[0]) for tv in tops]
    denom = exps[0] + exps[1] + exps[2] + exps[3]

    sel = jnp.zeros((tm, LANES), F32)
    for hot in hots:
        sel = sel + jnp.where(hot, 1.0, 0.0)
    r = lax.broadcasted_iota(I32, (tm, tm), 0)
    c = lax.broadcasted_iota(I32, (tm, tm), 1)
    strict = jnp.where(c < r, 1.0, 0.0).astype(BF16)
    ahead = jnp.dot(strict, sel.astype(BF16), preferred_element_type=F32) + cnt_scr[...]

    info = jnp.zeros((tm, LANES), I32)
    gate = jnp.zeros((tm, LANES), F32)
    for k in range(TOP_K):
        pos = jnp.sum(jnp.where(hots[k], ahead, 0.0), axis=-1, keepdims=True).astype(I32)
        info = jnp.where(lane == k, ids[k], info)
        info = jnp.where(lane == TOP_K + k, pos, info)
        gate = jnp.where(lane == k, exps[k] / denom, gate)
    info_ref[...] = info
    gate_ref[...] = gate
    cnt_scr[...] = cnt_scr[...] + jnp.sum(sel, axis=0, keepdims=True)
    cnt_ref[...] = jnp.broadcast_to(cnt_scr[...], cnt_ref.shape)


def _route(logits):
    t = logits.shape[0]
    tm = ROUTE_TM
    row = lambda i: (i, 0)
    return pl.pallas_call(
        _route_kernel,
        grid=(t // tm,),
        in_specs=[pl.BlockSpec((tm, LANES), row)],
        out_specs=[
            pl.BlockSpec((tm, LANES), row),
            pl.BlockSpec((tm, LANES), row),
            pl.BlockSpec((SUBLANES, LANES), lambda i: (0, 0)),
        ],
        out_shape=[
            jax.ShapeDtypeStruct((t, LANES), I32),
            jax.ShapeDtypeStruct((t, LANES), F32),
            jax.ShapeDtypeStruct((SUBLANES, LANES), F32),
        ],
        scratch_shapes=[pltpu.VMEM((1, LANES), F32)],
        compiler_params=_cparams(("arbitrary",)),
        name="route",
    )(logits)


def _rows_kernel(info_ref, start_ref, rows_ref):
    tm = info_ref.shape[0]
    lane = lax.broadcasted_iota(I32, (tm, LANES), 1)
    info = info_ref[...]
    rows = jnp.zeros((tm, LANES), I32)
    for k in range(TOP_K):
        hot = lane == info[:, k:k + 1]
        pos = info[:, TOP_K + k:TOP_K + k + 1]
        packed = jnp.sum(jnp.where(hot, start_ref[...], 0.0), axis=-1, keepdims=True).astype(I32)
        x_row = jnp.bitwise_and(packed, ROWS_PACK - 1) * MOE_SUB + pos
        y_row = jnp.right_shift(packed, ROWS_PACK.bit_length() - 1) * MOE_CAP + pos
        rows = jnp.where(lane == k, x_row, rows)
        rows = jnp.where(lane == TOP_K + k, y_row, rows)
    rows_ref[...] = rows


def _rows(info, x_start, y_start, n_subs):
    t = info.shape[0]
    tm = min(ROWS_TM, t)
    assert n_subs < ROWS_PACK, "sub-tile indices must fit below the packing factor"
    packed = (x_start // MOE_SUB + ROWS_PACK * (y_start // MOE_CAP)).astype(F32)
    packed = jnp.pad(packed, (0, LANES - N_EXPERTS))[None, :]
    row = lambda i: (i, 0)
    return pl.pallas_call(
        _rows_kernel,
        grid=(t // tm,),
        in_specs=[pl.BlockSpec((tm, LANES), row), pl.BlockSpec((1, LANES), lambda i: (0, 0))],
        out_specs=pl.BlockSpec((tm, LANES), row),
        out_shape=jax.ShapeDtypeStruct((t, LANES), I32),
        compiler_params=_cparams(("arbitrary",)),
        name="rows",
    )(info, packed)


def _dispatch_kernel(dest_ref, zero_row_ref, h2_ref, xs_hbm, zbuf, sem, zsem, *, n_zero):
    tm = h2_ref.shape[0]
    i = pl.program_id(0)

    def zero_copy(j):
        row0 = pl.multiple_of(zero_row_ref[j], MOE_SUB)
        return pltpu.make_async_copy(zbuf, xs_hbm.at[pl.ds(row0, MOE_SUB)], zsem)

    @pl.when(i == 0)
    def _():
        zbuf[...] = jnp.zeros_like(zbuf)

        def start(j, carry):
            @pl.when(zero_row_ref[j] >= 0)
            def _():
                zero_copy(j).start()
            return carry

        def wait(j, carry):
            @pl.when(zero_row_ref[j] >= 0)
            def _():
                zero_copy(j).wait()
            return carry

        lax.fori_loop(0, n_zero, start, 0)
        lax.fori_loop(0, n_zero, wait, 0)

    for r in range(tm):
        for k in range(TOP_K):
            d = dest_ref[(i * tm + r) * TOP_K + k]
            pltpu.make_async_copy(h2_ref.at[pl.ds(r, 1)], xs_hbm.at[pl.ds(d, 1)],
                                  sem).start(priority=k % 2)

    for _ in range(TOP_K):
        pltpu.make_async_copy(h2_ref, xs_hbm.at[pl.ds(0, tm)], sem).wait()


def _dispatch(dest, zero_rows, h2, n_rows):
    t = h2.shape[0]
    tm = DISPATCH_TM
    grid_spec = pltpu.PrefetchScalarGridSpec(
        num_scalar_prefetch=2,
        grid=(t // tm,),
        in_specs=[pl.BlockSpec((tm, D_MODEL), lambda i, d, z: (i, 0))],
        out_specs=pl.BlockSpec(memory_space=pl.ANY),
        scratch_shapes=[
            pltpu.VMEM((MOE_SUB, D_MODEL), F32),
            pltpu.SemaphoreType.DMA,
            pltpu.SemaphoreType.DMA,
        ],
    )
    return pl.pallas_call(
        functools.partial(_dispatch_kernel, n_zero=zero_rows.shape[0]),
        grid_spec=grid_spec,
        out_shape=jax.ShapeDtypeStruct((n_rows, D_MODEL), F32),
        compiler_params=_cparams(("arbitrary",)),
        name="dispatch",
    )(dest, zero_rows, h2)


def _experts_kernel(tile_e_ref, tile_occ_ref, tile_row_ref,
                    xs_hbm, wg_ref, wu_ref, bg_ref, bu_ref, wd_ref, bd_ref, out_ref,
                    xbuf, xb_scr, act_scr, sem, *, n_tiles):
    del tile_e_ref
    i = pl.program_id(0)
    j = pl.program_id(1)
    occ = tile_occ_ref[i]

    def sub_rows(sub):
        return pl.ds(pl.multiple_of(sub * MOE_SUB, MOE_SUB), MOE_SUB)

    def sub_copy(tile, sub):
        row0 = pl.multiple_of(tile_row_ref[tile] + sub * MOE_SUB, MOE_SUB)
        return pltpu.make_async_copy(xs_hbm.at[pl.ds(row0, MOE_SUB)], xbuf.at[sub_rows(sub)], sem)

    def start_load(tile):
        def body(sub, carry):
            sub_copy(tile, sub).start()
            return carry
        lax.fori_loop(0, tile_occ_ref[tile], body, 0)

    @pl.when(j == 0)
    def _():
        @pl.when(i == 0)
        def _():
            start_load(0)

        def wait_sub(sub, carry):
            sub_copy(i, sub).wait()
            return carry

        def round_sub(sub, carry):
            xb_scr[sub_rows(sub), :] = xbuf[sub_rows(sub), :].astype(BF16)
            return carry

        lax.fori_loop(0, occ, wait_sub, 0)
        lax.fori_loop(0, occ, round_sub, 0)

        @pl.when(i + 1 < n_tiles)
        def _():
            start_load(i + 1)

    def for_each_sub(weights, fn):
        def quad(p, carry):
            w = weights()
            for u in range(4):
                fn(4 * p + u, *w)
            return carry

        lax.fori_loop(0, occ // 4, quad, 0)
        done = (occ // 4) * 4

        @pl.when(occ % 4 >= 2)
        def _():
            w = weights()
            fn(done, *w)
            fn(done + 1, *w)

        @pl.when(occ % 2 == 1)
        def _():
            fn(occ - 1, *weights())

    @pl.when(j < MOE_NF)
    def _():
        def weights():
            return wg_ref[...].astype(BF16), wu_ref[...].astype(BF16)

        def gate_up(sub, wg, wu):
            rows = sub_rows(sub)
            xb = xb_scr[rows, :]
            gate = jnp.dot(xb, wg, preferred_element_type=F32) + bg_ref[...]
            up = jnp.dot(xb, wu, preferred_element_type=F32) + bu_ref[...]
            gate = jnp.minimum(gate, SWIGLU_LIMIT)
            up = jnp.clip(up, -SWIGLU_LIMIT, SWIGLU_LIMIT)
            act = (up + 1.0) * (gate * _sigmoid(SWIGLU_ALPHA * gate))
            act_scr[j, rows, :] = act.astype(BF16)

        for_each_sub(weights, gate_up)

    @pl.when(j >= MOE_NF)
    def _():
        def weights():
            return (wd_ref[...].astype(BF16),)

        def down(sub, wd):
            rows = sub_rows(sub)
            y = bd_ref[...] + jnp.dot(act_scr[0, rows, :], wd[0:MOE_TF, :],
                                      preferred_element_type=F32)
            for c in range(1, MOE_NF):
                y = y + jnp.dot(act_scr[c, rows, :], wd[c * MOE_TF:(c + 1) * MOE_TF, :],
                                preferred_element_type=F32)
            out_ref[rows, :] = y

        for_each_sub(weights, down)

    @pl.when(j >= MOE_NF)
    def _():
        def zero_sub(sub, carry):
            out_ref[sub_rows(sub), :] = jnp.zeros((MOE_SUB, MOE_TN), F32)
            return carry

        lax.fori_loop(occ, MOE_SUBS, zero_sub, 0)


def _experts(tile_e, tile_occ, tile_row, xs, wgu, bgu, wd, bd, n_tiles):
    up_off = D_FF // MOE_TF

    def fa(i, j, to):
        return jnp.where(to[i] > 0, jnp.minimum(j, MOE_NF - 1), MOE_NF - 1)

    def fb(i, j, to):
        return jnp.where(to[i] > 0, jnp.maximum(j - MOE_NF, 0), MOE_NN - 1)

    def gate_up_block(i, j, te, to):
        nxt = jnp.minimum(i + 1, n_tiles - 1)
        ahead = j >= MOE_NF
        tile = jnp.where(ahead, nxt, i)
        return te[tile], fa(tile, jnp.where(ahead, 0, j), to)

    def gate_map(i, j, te, to, tr):
        e, chunk = gate_up_block(i, j, te, to)
        return (e, 0, chunk)

    def up_map(i, j, te, to, tr):
        e, chunk = gate_up_block(i, j, te, to)
        return (e, 0, up_off + chunk)

    def down_map(i, j, te, to, tr):
        prev = jnp.maximum(i - 1, 0)
        hold = (j < MOE_NF) & (i > 0) & (to[i] > 0)
        return (jnp.where(hold, te[prev], te[i]), 0, jnp.where(hold, MOE_NN - 1, fb(i, j, to)))

    def out_map(i, j, te, to, tr):
        return (i, jnp.maximum(j - MOE_NF, 0))

    grid_spec = pltpu.PrefetchScalarGridSpec(
        num_scalar_prefetch=3,
        grid=(n_tiles, MOE_NF + MOE_NN),
        in_specs=[
            pl.BlockSpec(memory_space=pl.ANY),
            pl.BlockSpec((None, D_MODEL, MOE_TF), gate_map),
            pl.BlockSpec((None, D_MODEL, MOE_TF), up_map),
            pl.BlockSpec((None, 1, MOE_TF), gate_map),
            pl.BlockSpec((None, 1, MOE_TF), up_map),
            pl.BlockSpec((None, D_FF, MOE_TN), down_map),
            pl.BlockSpec((None, 1, MOE_TN), down_map),
        ],
        out_specs=pl.BlockSpec((MOE_CAP, MOE_TN), out_map),
        scratch_shapes=[
            pltpu.VMEM((MOE_CAP, D_MODEL), F32),
            pltpu.VMEM((MOE_CAP, D_MODEL), BF16),
            pltpu.VMEM((MOE_NF, MOE_CAP, MOE_TF), BF16),
            pltpu.SemaphoreType.DMA,
        ],
    )
    return pl.pallas_call(
        functools.partial(_experts_kernel, n_tiles=n_tiles),
        grid_spec=grid_spec,
        out_shape=jax.ShapeDtypeStruct((n_tiles * MOE_CAP, D_MODEL), F32),
        compiler_params=_cparams(("arbitrary", "arbitrary")),
        name="experts",
    )(tile_e, tile_occ, tile_row, xs, wgu, wgu, bgu, bgu, wd, bd)


def _combine_kernel(dest_ref, ys_hbm, x1_ref, gate_ref, gf_ref, out_ref, buf, sem, *, n_tiles):
    tm = x1_ref.shape[0]
    i = pl.program_id(0)
    slot = i % 2

    def start_gather(tile, s):
        for r in range(tm):
            for k in range(TOP_K):
                d = dest_ref[(tile * tm + r) * TOP_K + k]
                pltpu.make_async_copy(ys_hbm.at[pl.ds(d, 1)], buf.at[s, k, pl.ds(r, 1)],
                                      sem.at[s]).start(priority=k % 2)

    @pl.when(i == 0)
    def _():
        start_gather(0, 0)

    for k in range(TOP_K):
        pltpu.make_async_copy(ys_hbm.at[pl.ds(0, tm)], buf.at[slot, k], sem.at[slot]).wait()

    for s in range(2):
        @pl.when((i + 1 < n_tiles) & (slot == 1 - s))
        def _():
            start_gather(i + 1, s)

    x2 = x1_ref[...]
    for k in range(TOP_K):
        x2 = x2 + gate_ref[:, k:k + 1] * buf[slot, k]
    out_ref[...] = x2 * lax.rsqrt(jnp.mean(x2 * x2, axis=-1, keepdims=True) + EPS) * gf_ref[...]


def _combine(dest, ys, x1, gate, gf):
    t = x1.shape[0]
    tm = COMBINE_TM
    n_tiles = t // tm
    grid_spec = pltpu.PrefetchScalarGridSpec(
        num_scalar_prefetch=1,
        grid=(n_tiles,),
        in_specs=[
            pl.BlockSpec(memory_space=pl.ANY),
            pl.BlockSpec((tm, D_MODEL), lambda i, d: (i, 0)),
            pl.BlockSpec((tm, LANES), lambda i, d: (i, 0)),
            pl.BlockSpec((1, D_MODEL), lambda i, d: (0, 0)),
        ],
        out_specs=pl.BlockSpec((tm, D_MODEL), lambda i, d: (i, 0)),
        scratch_shapes=[
            pltpu.VMEM((2, TOP_K, tm, D_MODEL), F32),
            pltpu.SemaphoreType.DMA((2,)),
        ],
    )
    return pl.pallas_call(
        functools.partial(_combine_kernel, n_tiles=n_tiles),
        grid_spec=grid_spec,
        out_shape=jax.ShapeDtypeStruct((t, D_MODEL), F32),
        compiler_params=_cparams(("arbitrary",)),
        name="combine",
    )(dest, ys, x1, gate, gf)


def _routing_tables(counts, n_tokens):
    n_tiles = N_EXPERTS + (n_tokens * TOP_K) // MOE_CAP
    n_subs = N_EXPERTS + (n_tokens * TOP_K) // MOE_SUB
    subs_per = (counts + MOE_SUB - 1) // MOE_SUB
    sub_end = jnp.cumsum(subs_per)
    x_start = (sub_end - subs_per) * MOE_SUB
    tiles_per = (counts + MOE_CAP - 1) // MOE_CAP
    tile_end = jnp.cumsum(tiles_per)
    tile_start = tile_end - tiles_per
    y_start = tile_start * MOE_CAP

    tile_ids = jnp.arange(n_tiles, dtype=I32)
    n_active = tile_end[-1]
    last = jnp.maximum(n_active - 1, 0)
    active = tile_ids < n_active
    tile_eff = jnp.minimum(tile_ids, last)
    tile_e = jnp.sum((tile_eff[:, None] >= tile_end[None, :]).astype(I32), axis=1)
    tile_e = jnp.minimum(tile_e, N_EXPERTS - 1)
    in_expert = tile_eff - tile_start[tile_e]
    rows_in = counts[tile_e] - in_expert * MOE_CAP
    occ = jnp.clip((rows_in + MOE_SUB - 1) // MOE_SUB, 0, MOE_SUBS)
    tile_occ = jnp.where(active, occ, 0).astype(I32)
    tile_row = (x_start[tile_e] + in_expert * MOE_CAP).astype(I32)

    last_sub = jnp.where(counts > 0, x_start + (subs_per - 1) * MOE_SUB, -1)
    tail_ids = sub_end[-1] + jnp.arange(N_EXPERTS, dtype=I32)
    tail = jnp.where(tail_ids < n_subs, tail_ids * MOE_SUB, -1)
    zero_rows = jnp.concatenate([last_sub, tail]).astype(I32)
    return (n_tiles, n_subs * MOE_SUB, tile_e.astype(I32), tile_occ, tile_row,
            x_start, y_start, zero_rows)


def _layer(x2d, bsz, seq, g_norm_mix, w_in, b_in, w_conv, b_conv, w_sgu, b_sgu, g_sgu_ln, b_sgu_ln,
           g_mlstm_norm, w_proj_sgu, w_proj_mlstm, w_out, g_norm_moe, w_router, b_router,
           w_gate_up, b_gate_up, w_down, b_down, g_out):
    t = x2d.shape[0]

    pad = LANES - 2 * HEADS
    w_if = jnp.pad(w_in[:, N_MAIN:GATE_OFF], ((0, 0), (0, pad))).astype(BF16)
    b_if = jnp.pad(b_in[N_MAIN:GATE_OFF], (0, pad))[None, :]
    w_in_t = w_in.T
    bsb = jnp.broadcast_to(b_sgu.T[:, :, None], (SGU_BLOCK, SGU_GROUPS, SGU_GROUP_DIM))
    bsb = bsb.reshape(SGU_BLOCK, D_MODEL)
    w_r = jnp.pad(w_router, ((0, 0), (0, LANES - N_EXPERTS)))
    w_r_hi = w_r.astype(BF16)
    w_r_lo = (w_r - w_r_hi.astype(F32)).astype(BF16)
    b_r = jnp.pad(b_router, (0, LANES - N_EXPERTS))[None, :]

    h, gates = _norm_gates(x2d, g_norm_mix[None, :], w_if, b_if)
    zm = _in_proj(h, w_in_t, 0, b_in[None, :], N_MAIN, 2 * D_MODEL, 3 * D_MODEL, "in_proj_main")
    zg = _in_proj(h, w_in_t, GATE_OFF, b_in[None, GATE_OFF:], 2 * D_MODEL, 0, 0, "in_proj_gate")
    a = _sgu(zm, zg, w_sgu, bsb, g_sgu_ln[None, :], b_sgu_ln[None, :], w_proj_sgu.astype(BF16))
    ym = _mlstm(zm, gates, w_conv, b_conv[None, :], g_mlstm_norm[None, :], bsz, seq)
    x1, h2, logits = _merge(ym, a, zg, x2d, w_proj_mlstm.astype(BF16), w_out.astype(BF16),
                            g_norm_moe[None, :], w_r_hi, w_r_lo, b_r)
    info, gate, cnt = _route(logits)

    counts = cnt[0, :N_EXPERTS].astype(I32)
    (n_tiles, n_xrows, tile_e, tile_occ, tile_row, x_start, y_start,
     zero_rows) = _routing_tables(counts, t)
    rows = _rows(info, x_start, y_start, n_xrows // MOE_SUB)
    dest_x = rows[:, :TOP_K].reshape(-1)
    dest_y = rows[:, TOP_K:2 * TOP_K].reshape(-1)

    xs = _dispatch(dest_x, zero_rows, h2, n_xrows)
    ys = _experts(tile_e, tile_occ, tile_row, xs, w_gate_up, b_gate_up[:, None, :],
                  w_down, b_down[:, None, :], n_tiles)
    return _combine(dest_y, ys, x1, gate, g_out[None, :])


def kernel(x, g_norm_mix, w_in, b_in, w_conv, b_conv, w_sgu, b_sgu, g_sgu_ln, b_sgu_ln, g_mlstm_norm,
           w_proj_sgu, w_proj_mlstm, w_out, g_norm_moe, w_router, b_router, w_gate_up, b_gate_up,
           w_down, b_down, g_final):
    bsz, seq, d = x.shape
    assert d == D_MODEL and w_in.shape[0] == 1, "single-layer block with d_model 2048"
    out = _layer(x.reshape(bsz * seq, d), bsz, seq, g_norm_mix[0], w_in[0], b_in[0], w_conv[0],
                 b_conv[0], w_sgu[0], b_sgu[0], g_sgu_ln[0], b_sgu_ln[0], g_mlstm_norm[0],
                 w_proj_sgu[0], w_proj_mlstm[0], w_out[0], g_norm_moe[0], w_router[0], b_router[0],
                 w_gate_up[0], b_gate_up[0], w_down[0], b_down[0], g_final)
    return out.reshape(bsz, seq, d)
```

```python
import functools

import jax
import jax.numpy as jnp
from jax import lax
from jax.experimental import pallas as pl
from jax.experimental.pallas import tpu as pltpu

F32 = jnp.float32
BF16 = jnp.bfloat16
I32 = jnp.int32

D_MODEL = 2048
CHUNK = 64
SGU_BLOCK = 128
SGU_GROUPS = 8
SGU_GROUP_DIM = 256
HEADS = 4
HEAD_DIM = 512
CONV_WIDTH = 4
N_EXPERTS = 32
TOP_K = 4
D_FF = 2048
SWIGLU_LIMIT = 7.0
SWIGLU_ALPHA = 1.702
EPS = 1e-6

LANES = 128
SUBLANES = 8
VMEM_LIMIT = 56 * 1024 * 1024

ZB_U, ZB_V, ZB_Q, ZB_K, ZB_VM, ZB_O = range(6)
ZG_A, ZG_M = range(2)
N_MAIN = 6 * D_MODEL
GATE_OFF = N_MAIN + 2 * HEADS

NORM_TM = 1024
INPROJ_TM = 1024
INPROJ_TN = 1024
INPROJ_RC = 512
SGU_TM = 256
MLSTM_L = 256
MERGE_TM = 256
ROUTE_TM = 512
ROWS_TM = 1024
DISPATCH_TM = 512
MOE_SUB = 128
MOE_CAP = 1280
MOE_SUBS = MOE_CAP // MOE_SUB
MOE_TF = 512
MOE_NF = D_FF // MOE_TF
MOE_TN = 512
MOE_NN = D_MODEL // MOE_TN
COMBINE_TM = 256
ROWS_PACK = 512


def _cparams(sem):
    return pltpu.CompilerParams(dimension_semantics=sem, vmem_limit_bytes=VMEM_LIMIT)


def _gelu(x):
    return 0.5 * x * (1.0 + lax.erf(x * (2.0 ** -0.5)))


def _sigmoid(x):
    return 1.0 / (1.0 + jnp.exp(-x))


def _norm_gates_kernel(x_ref, g_ref, wif_ref, bif_ref, h_ref, gates_ref):
    x = x_ref[...]
    ms = jnp.mean(x * x, axis=-1, keepdims=True)
    h = (x * lax.rsqrt(ms + EPS) * g_ref[...]).astype(BF16)
    h_ref[...] = h
    gates_ref[...] = jnp.dot(h, wif_ref[...], preferred_element_type=F32) + bif_ref[...]


def _norm_gates(x, g, wif, bif):
    t = x.shape[0]
    tm = min(NORM_TM, t)
    row = lambda i: (i, 0)
    const = lambda i: (0, 0)
    return pl.pallas_call(
        _norm_gates_kernel,
        grid=(t // tm,),
        in_specs=[
            pl.BlockSpec((tm, D_MODEL), row),
            pl.BlockSpec((1, D_MODEL), const),
            pl.BlockSpec((D_MODEL, LANES), const),
            pl.BlockSpec((1, LANES), const),
        ],
        out_specs=[pl.BlockSpec((tm, D_MODEL), row), pl.BlockSpec((tm, LANES), row)],
        out_shape=[jax.ShapeDtypeStruct((t, D_MODEL), BF16), jax.ShapeDtypeStruct((t, LANES), F32)],
        compiler_params=_cparams(("arbitrary",)),
        name="norm_gates",
    )(x, g, wif, bif)


def _inproj_kernel(h_ref, w_ref, b_ref, z_ref, w_scr, *, n_gelu, n_plain):
    j = pl.program_id(0)

    @pl.when(pl.program_id(1) == 0)
    def _():
        w_scr[...] = w_ref[...].astype(BF16)

    def project(activation):
        tm = h_ref.shape[0]
        rc = min(INPROJ_RC, tm)
        for c in range(tm // rc):
            rows = slice(c * rc, (c + 1) * rc)
            acc = lax.dot_general(h_ref[rows, :], w_scr[...], (((1,), (1,)), ((), ())),
                                  preferred_element_type=F32) + b_ref[...]
            z_ref[rows, :] = activation(acc).astype(z_ref.dtype)

    @pl.when(j < n_gelu)
    def _():
        project(_gelu)

    @pl.when((j >= n_gelu) & (j < n_gelu + n_plain))
    def _():
        project(lambda acc: acc)

    @pl.when(j >= n_gelu + n_plain)
    def _():
        project(_sigmoid)


def _in_proj(h, w_t, row_off, b, n_cols, n_gelu_cols, n_plain_cols, name):
    t = h.shape[0]
    tm, tn = min(INPROJ_TM, t), INPROJ_TN
    assert row_off % SUBLANES == 0
    return pl.pallas_call(
        functools.partial(_inproj_kernel, n_gelu=n_gelu_cols // tn, n_plain=n_plain_cols // tn),
        grid=(n_cols // tn, t // tm),
        in_specs=[
            pl.BlockSpec((tm, D_MODEL), lambda j, i: (i, 0)),
            pl.BlockSpec((pl.Element(tn), pl.Element(D_MODEL)),
                         lambda j, i: (pl.multiple_of(row_off + j * tn, SUBLANES), 0)),
            pl.BlockSpec((1, tn), lambda j, i: (0, j)),
        ],
        out_specs=pl.BlockSpec((tm, tn), lambda j, i: (i, j)),
        out_shape=jax.ShapeDtypeStruct((t, n_cols), BF16),
        scratch_shapes=[pltpu.VMEM((tn, D_MODEL), BF16)],
        compiler_params=_cparams(("arbitrary", "arbitrary")),
        name=name,
    )(h, w_t, b)


def _sgu_kernel(gu_ref, gv_ref, sga_ref, ws_ref, bsb_ref, lng_ref, lnb_ref, pa_ref, out_ref,
                ya_scr):
    tm = gu_ref.shape[0]
    v = gv_ref[...].astype(F32)
    mu = jnp.mean(v, axis=-1, keepdims=True)
    vc = v - mu
    var = jnp.mean(vc * vc, axis=-1, keepdims=True)
    vln = (vc * lax.rsqrt(var + EPS) * lng_ref[...] + lnb_ref[...]).astype(BF16)

    t_id = lax.broadcasted_iota(I32, (SGU_BLOCK, SGU_BLOCK), 0) // CHUNK
    s_id = lax.broadcasted_iota(I32, (SGU_BLOCK, SGU_BLOCK), 1) // CHUNK
    causal = s_id <= t_id
    for g in range(SGU_GROUPS):
        w = jnp.where(causal, ws_ref[g], 0.0).astype(BF16)
        cols = slice(g * SGU_GROUP_DIM, (g + 1) * SGU_GROUP_DIM)
        bias = bsb_ref[:, cols]
        for blk in range(tm // SGU_BLOCK):
            rows = slice(blk * SGU_BLOCK, (blk + 1) * SGU_BLOCK)
            mixed = jnp.dot(w, vln[rows, cols], preferred_element_type=F32) + bias
            ya_scr[rows, cols] = (gu_ref[rows, cols].astype(F32) * mixed).astype(BF16)

    proj = jnp.dot(ya_scr[...], pa_ref[...], preferred_element_type=F32)
    out_ref[...] = (sga_ref[...].astype(F32) * proj).astype(out_ref.dtype)


def _sgu(zm, zg, ws, bsb, lng, lnb, pa):
    t = zm.shape[0]
    tm = SGU_TM
    const2 = lambda i: (0, 0)
    return pl.pallas_call(
        _sgu_kernel,
        grid=(t // tm,),
        in_specs=[
            pl.BlockSpec((tm, D_MODEL), lambda i: (i, ZB_U)),
            pl.BlockSpec((tm, D_MODEL), lambda i: (i, ZB_V)),
            pl.BlockSpec((tm, D_MODEL), lambda i: (i, ZG_A)),
            pl.BlockSpec((SGU_GROUPS, SGU_BLOCK, SGU_BLOCK), lambda i: (0, 0, 0)),
            pl.BlockSpec((SGU_BLOCK, D_MODEL), const2),
            pl.BlockSpec((1, D_MODEL), const2),
            pl.BlockSpec((1, D_MODEL), const2),
            pl.BlockSpec((D_MODEL, D_MODEL), const2),
        ],
        out_specs=pl.BlockSpec((tm, D_MODEL), lambda i: (i, 0)),
        out_shape=jax.ShapeDtypeStruct((t, D_MODEL), BF16),
        scratch_shapes=[pltpu.VMEM((tm, D_MODEL), BF16)],
        compiler_params=_cparams(("arbitrary",)),
        name="sgu",
    )(zm, zm, zg, ws, bsb, lng, lnb, pa)


def _mlstm_kernel(q_ref, k_ref, v_ref, so_ref, gates_ref, wc_ref, bc_ref, gn_ref, out_ref,
                  hist, ct_scr, n_scr, m_scr):
    L = q_ref.shape[0]
    c = pl.program_id(1)

    @pl.when(c == 0)
    def _():
        hist[0:SUBLANES, :] = jnp.zeros((SUBLANES, 2 * D_MODEL), F32)
        ct_scr[...] = jnp.zeros_like(ct_scr)
        n_scr[...] = jnp.zeros_like(n_scr)
        m_scr[...] = jnp.zeros_like(m_scr)

    hist[SUBLANES:SUBLANES + L, 0:D_MODEL] = q_ref[...].astype(F32)
    hist[SUBLANES:SUBLANES + L, D_MODEL:2 * D_MODEL] = k_ref[...].astype(F32)
    qk = bc_ref[...] + wc_ref[CONV_WIDTH - 1:CONV_WIDTH, :] * hist[SUBLANES:SUBLANES + L, :]
    for j in range(CONV_WIDTH - 1):
        off = SUBLANES - (CONV_WIDTH - 1) + j
        qk = qk + wc_ref[j:j + 1, :] * hist[off:off + L, :]
    qk = qk * _sigmoid(qk)
    hist[0:SUBLANES, :] = hist[L:L + SUBLANES, :]

    gts = gates_ref[...]
    lf = jnp.minimum(gts, 0.0) - jnp.log(1.0 + jnp.exp(-jnp.abs(gts)))
    row = lax.broadcasted_iota(I32, (L, L), 0)
    col = lax.broadcasted_iota(I32, (L, L), 1)
    tril = col <= row
    tri = jnp.where(tril, 1.0, 0.0).astype(BF16)
    lf_hi = lf.astype(BF16)
    lf_lo = (lf - lf_hi.astype(F32)).astype(BF16)
    bcum = (jnp.dot(tri, lf_hi, preferred_element_type=F32)
            + jnp.dot(tri, lf_lo, preferred_element_type=F32))
    bcum_t = bcum.T
    gts_t = gts.T

    scale = HEAD_DIM ** -0.5
    for h in range(HEADS):
        cols = slice(h * HEAD_DIM, (h + 1) * HEAD_DIM)
        q = qk[:, h * HEAD_DIM:(h + 1) * HEAD_DIM] * scale
        k = qk[:, D_MODEL + h * HEAD_DIM:D_MODEL + (h + 1) * HEAD_DIM]
        qb = q.astype(BF16)
        kb = k.astype(BF16)
        vb = v_ref[:, cols]
        bc = bcum[:, HEADS + h:HEADS + h + 1]
        bc_r = bcum_t[HEADS + h:HEADS + h + 1, :]
        ig_c = gts[:, h:h + 1]
        ig_r = gts_t[h:h + 1, :]
        m_st = m_scr[h]
        n_st = n_scr[h]

        dlog = jnp.where(tril, bc - bc_r + ig_r, -jnp.inf)
        m_inter = bc + m_st
        m_t = jnp.maximum(m_inter, jnp.max(dlog, axis=-1, keepdims=True))
        p = jnp.exp(dlog - m_t)
        s = lax.dot_general(qb, kb, (((1,), (1,)), ((), ())), preferred_element_type=F32)
        sc = s * p
        inter = jnp.exp(m_inter - m_t)
        ctb = ct_scr[h].astype(BF16)
        num = (jnp.dot(sc.astype(BF16), vb, preferred_element_type=F32)
               + inter * jnp.dot(qb, ctb, preferred_element_type=F32))
        den = (jnp.sum(sc, axis=-1, keepdims=True)
               + inter * jnp.sum(q * n_st, axis=-1, keepdims=True))
        hh = num / jnp.maximum(jnp.abs(den), jnp.exp(-m_t))
        hn = hh * lax.rsqrt(jnp.mean(hh * hh, axis=-1, keepdims=True) + EPS) * gn_ref[:, cols]
        out_ref[:, cols] = (hn * so_ref[:, cols].astype(F32)).astype(out_ref.dtype)

        g_tot = bc[L - 1:L, :]
        a = g_tot - bc + ig_c
        m_new = jnp.maximum(g_tot + m_st, jnp.max(a, axis=0, keepdims=True))
        wa = jnp.exp(a - m_new)
        decay = jnp.exp(g_tot + m_st - m_new)
        wv = (wa * vb.astype(F32)).astype(BF16)
        upd = jnp.dot(k.T.astype(BF16), wv, preferred_element_type=F32)
        ct_scr[h] = decay * ct_scr[h] + upd
        n_scr[h] = decay * n_st + jnp.sum(wa * k, axis=0, keepdims=True)
        m_scr[h] = m_new


def _mlstm(zm, gates, wc, bc, gn, bsz, seq):
    t = zm.shape[0]
    L = MLSTM_L
    nc = seq // L
    zspec = lambda blk: pl.BlockSpec((L, D_MODEL), lambda b, c: (b * nc + c, blk))
    const2 = lambda b, c: (0, 0)
    return pl.pallas_call(
        _mlstm_kernel,
        grid=(bsz, nc),
        in_specs=[
            zspec(ZB_Q), zspec(ZB_K), zspec(ZB_VM), zspec(ZB_O),
            pl.BlockSpec((L, LANES), lambda b, c: (b * nc + c, 0)),
            pl.BlockSpec((CONV_WIDTH, 2 * D_MODEL), const2),
            pl.BlockSpec((1, 2 * D_MODEL), const2),
            pl.BlockSpec((1, D_MODEL), const2),
        ],
        out_specs=pl.BlockSpec((L, D_MODEL), lambda b, c: (b * nc + c, 0)),
        out_shape=jax.ShapeDtypeStruct((t, D_MODEL), BF16),
        scratch_shapes=[
            pltpu.VMEM((L + 2 * SUBLANES, 2 * D_MODEL), F32),
            pltpu.VMEM((HEADS, HEAD_DIM, HEAD_DIM), F32),
            pltpu.VMEM((HEADS, 1, HEAD_DIM), F32),
            pltpu.VMEM((HEADS, 1, 1), F32),
        ],
        compiler_params=_cparams(("arbitrary", "arbitrary")),
        name="mlstm",
    )(zm, zm, zm, zm, gates, wc, bc, gn)


def _merge_kernel(ym_ref, a_ref, sgm_ref, x_ref, pm_ref, wo_ref, g2_ref, wrh_ref, wrl_ref, br_ref,
                  x1_ref, h2_ref, lg_ref):
    proj = jnp.dot(ym_ref[...], pm_ref[...], preferred_element_type=F32)
    merged = a_ref[...].astype(F32) + sgm_ref[...].astype(F32) * proj
    x1 = x_ref[...] + jnp.dot(merged.astype(BF16), wo_ref[...], preferred_element_type=F32)
    x1_ref[...] = x1
    h2 = x1 * lax.rsqrt(jnp.mean(x1 * x1, axis=-1, keepdims=True) + EPS) * g2_ref[...]
    h2_ref[...] = h2
    h_hi = h2.astype(BF16)
    h_lo = (h2 - h_hi.astype(F32)).astype(BF16)
    lg_ref[...] = (jnp.dot(h_hi, wrh_ref[...], preferred_element_type=F32)
                   + (jnp.dot(h_lo, wrh_ref[...], preferred_element_type=F32)
                      + jnp.dot(h_hi, wrl_ref[...], preferred_element_type=F32))
                   + br_ref[...])


def _merge(ym, a, zg, x, pm, wo, g2, wr_hi, wr_lo, br):
    t = x.shape[0]
    tm = MERGE_TM
    const2 = lambda i: (0, 0)
    row = lambda i: (i, 0)
    return pl.pallas_call(
        _merge_kernel,
        grid=(t // tm,),
        in_specs=[
            pl.BlockSpec((tm, D_MODEL), row),
            pl.BlockSpec((tm, D_MODEL), row),
            pl.BlockSpec((tm, D_MODEL), lambda i: (i, ZG_M)),
            pl.BlockSpec((tm, D_MODEL), row),
            pl.BlockSpec((D_MODEL, D_MODEL), const2),
            pl.BlockSpec((D_MODEL, D_MODEL), const2),
            pl.BlockSpec((1, D_MODEL), const2),
            pl.BlockSpec((D_MODEL, LANES), const2),
            pl.BlockSpec((D_MODEL, LANES), const2),
            pl.BlockSpec((1, LANES), const2),
        ],
        out_specs=[
            pl.BlockSpec((tm, D_MODEL), row),
            pl.BlockSpec((tm, D_MODEL), row),
            pl.BlockSpec((tm, LANES), row),
        ],
        out_shape=[
            jax.ShapeDtypeStruct((t, D_MODEL), F32),
            jax.ShapeDtypeStruct((t, D_MODEL), F32),
            jax.ShapeDtypeStruct((t, LANES), F32),
        ],
        compiler_params=_cparams(("arbitrary",)),
        name="merge",
    )(ym, a, zg, x, pm, wo, g2, wr_hi, wr_lo, br)


def _route_kernel(lg_ref, info_ref, gate_ref, cnt_ref, cnt_scr):
    tm = lg_ref.shape[0]
    i = pl.program_id(0)

    @pl.when(i == 0)
    def _():
        cnt_scr[...] = jnp.zeros_like(cnt_scr)

    lane = lax.broadcasted_iota(I32, (tm, LANES), 1)
    lane_f = lane.astype(F32)
    vals = jnp.where(lane < N_EXPERTS, lg_ref[...], -jnp.inf)
    tops, ids, hots = [], [], []
    for _ in range(TOP_K):
        m = jnp.max(vals, axis=-1, keepdims=True)
        idx_f = jnp.min(jnp.where(vals == m, lane_f, float(LANES)), axis=-1, keepdims=True)
        idx = idx_f.astype(I32)
        hot = lane == idx
        tops.append(m)
        ids.append(idx)
        hots.append(hot)
        vals = jnp.where(hot, -jnp.inf, vals)

    exps = [jnp.exp(tv - tops[0]) for tv in tops]
    denom = exps[0] + exps[1] + exps[2] + exps[3]

    sel = jnp.zeros((tm, LANES), F32)
    for hot in hots:
        sel = sel + jnp.where(hot, 1.0, 0.0)
    r = lax.broadcasted_iota(I32, (tm, tm), 0)
    c = lax.broadcasted_iota(I32, (tm, tm), 1)
    strict = jnp.where(c < r, 1.0, 0.0).astype(BF16)
    ahead = jnp.dot(strict, sel.astype(BF16), preferred_element_type=F32) + cnt_scr[...]

    info = jnp.zeros((tm, LANES), I32)
    gate = jnp.zeros((tm, LANES), F32)
    for k in range(TOP_K):
        pos = jnp.sum(jnp.where(hots[k], ahead, 0.0), axis=-1, keepdims=True).astype(I32)
        info = jnp.where(lane == k, ids[k], info)
        info = jnp.where(lane == TOP_K + k, pos, info)
        gate = jnp.where(lane == k, exps[k] / denom, gate)
    info_ref[...] = info
    gate_ref[...] = gate
    cnt_scr[...] = cnt_scr[...] + jnp.sum(sel, axis=0, keepdims=True)
    cnt_ref[...] = jnp.broadcast_to(cnt_scr[...], cnt_ref.shape)


def _route(logits):
    t = logits.shape[0]
    tm = ROUTE_TM
    row = lambda i: (i, 0)
    return pl.pallas_call(
        _route_kernel,
        grid=(t // tm,),
        in_specs=[pl.BlockSpec((tm, LANES), row)],
        out_specs=[
            pl.BlockSpec((tm, LANES), row),
            pl.BlockSpec((tm, LANES), row),
            pl.BlockSpec((SUBLANES, LANES), lambda i: (0, 0)),
        ],
        out_shape=[
            jax.ShapeDtypeStruct((t, LANES), I32),
            jax.ShapeDtypeStruct((t, LANES), F32),
            jax.ShapeDtypeStruct((SUBLANES, LANES), F32),
        ],
        scratch_shapes=[pltpu.VMEM((1, LANES), F32)],
        compiler_params=_cparams(("arbitrary",)),
        name="route",
    )(logits)


def _rows_kernel(info_ref, start_ref, rows_ref):
    tm = info_ref.shape[0]
    lane = lax.broadcasted_iota(I32, (tm, LANES), 1)
    info = info_ref[...]
    rows = jnp.zeros((tm, LANES), I32)
    for k in range(TOP_K):
        hot = lane == info[:, k:k + 1]
        pos = info[:, TOP_K + k:TOP_K + k + 1]
        packed = jnp.sum(jnp.where(hot, start_ref[...], 0.0), axis=-1, keepdims=True).astype(I32)
        x_row = jnp.bitwise_and(packed, ROWS_PACK - 1) * MOE_SUB + pos
        y_row = jnp.right_shift(packed, ROWS_PACK.bit_length() - 1) * MOE_CAP + pos
        rows = jnp.where(lane == k, x_row, rows)
        rows = jnp.where(lane == TOP_K + k, y_row, rows)
    rows_ref[...] = rows


def _rows(info, x_start, y_start, n_subs):
    t = info.shape[0]
    tm = min(ROWS_TM, t)
    assert n_subs < ROWS_PACK, "sub-tile indices must fit below the packing factor"
    packed = (x_start // MOE_SUB + ROWS_PACK * (y_start // MOE_CAP)).astype(F32)
    packed = jnp.pad(packed, (0, LANES - N_EXPERTS))[None, :]
    row = lambda i: (i, 0)
    return pl.pallas_call(
        _rows_kernel,
        grid=(t // tm,),
        in_specs=[pl.BlockSpec((tm, LANES), row), pl.BlockSpec((1, LANES), lambda i: (0, 0))],
        out_specs=pl.BlockSpec((tm, LANES), row),
        out_shape=jax.ShapeDtypeStruct((t, LANES), I32),
        compiler_params=_cparams(("arbitrary",)),
        name="rows",
    )(info, packed)


def _dispatch_kernel(dest_ref, zero_row_ref, h2_ref, xs_hbm, zbuf, sem, zsem, *, n_zero):
    tm = h2_ref.shape[0]
    i = pl.program_id(0)

    def zero_copy(j):
        row0 = pl.multiple_of(zero_row_ref[j], MOE_SUB)
        return pltpu.make_async_copy(zbuf, xs_hbm.at[pl.ds(row0, MOE_SUB)], zsem)

    @pl.when(i == 0)
    def _():
        zbuf[...] = jnp.zeros_like(zbuf)

        def start(j, carry):
            @pl.when(zero_row_ref[j] >= 0)
            def _():
                zero_copy(j).start()
            return carry

        def wait(j, carry):
            @pl.when(zero_row_ref[j] >= 0)
            def _():
                zero_copy(j).wait()
            return carry

        lax.fori_loop(0, n_zero, start, 0)
        lax.fori_loop(0, n_zero, wait, 0)

    for r in range(tm):
        for k in range(TOP_K):
            d = dest_ref[(i * tm + r) * TOP_K + k]
            pltpu.make_async_copy(h2_ref.at[pl.ds(r, 1)], xs_hbm.at[pl.ds(d, 1)],
                                  sem).start(priority=k % 2)

    for _ in range(TOP_K):
        pltpu.make_async_copy(h2_ref, xs_hbm.at[pl.ds(0, tm)], sem).wait()


def _dispatch(dest, zero_rows, h2, n_rows):
    t = h2.shape[0]
    tm = DISPATCH_TM
    grid_spec = pltpu.PrefetchScalarGridSpec(
        num_scalar_prefetch=2,
        grid=(t // tm,),
        in_specs=[pl.BlockSpec((tm, D_MODEL), lambda i, d, z: (i, 0))],
        out_specs=pl.BlockSpec(memory_space=pl.ANY),
        scratch_shapes=[
            pltpu.VMEM((MOE_SUB, D_MODEL), F32),
            pltpu.SemaphoreType.DMA,
            pltpu.SemaphoreType.DMA,
        ],
    )
    return pl.pallas_call(
        functools.partial(_dispatch_kernel, n_zero=zero_rows.shape[0]),
        grid_spec=grid_spec,
        out_shape=jax.ShapeDtypeStruct((n_rows, D_MODEL), F32),
        compiler_params=_cparams(("arbitrary",)),
        name="dispatch",
    )(dest, zero_rows, h2)


def _experts_kernel(tile_e_ref, tile_occ_ref, tile_row_ref,
                    xs_hbm, wg_ref, wu_ref, bg_ref, bu_ref, wd_ref, bd_ref, out_ref,
                    xbuf, xb_scr, act_scr, sem, *, n_tiles):
    del tile_e_ref
    i = pl.program_id(0)
    j = pl.program_id(1)
    occ = tile_occ_ref[i]

    def sub_rows(sub):
        return pl.ds(pl.multiple_of(sub * MOE_SUB, MOE_SUB), MOE_SUB)

    def sub_copy(tile, sub):
        row0 = pl.multiple_of(tile_row_ref[tile] + sub * MOE_SUB, MOE_SUB)
        return pltpu.make_async_copy(xs_hbm.at[pl.ds(row0, MOE_SUB)], xbuf.at[sub_rows(sub)], sem)

    def start_load(tile):
        def body(sub, carry):
            sub_copy(tile, sub).start()
            return carry
        lax.fori_loop(0, tile_occ_ref[tile], body, 0)

    @pl.when(j == 0)
    def _():
        @pl.when(i == 0)
        def _():
            start_load(0)

        def wait_sub(sub, carry):
            sub_copy(i, sub).wait()
            return carry

        def round_sub(sub, carry):
            xb_scr[sub_rows(sub), :] = xbuf[sub_rows(sub), :].astype(BF16)
            return carry

        lax.fori_loop(0, occ, wait_sub, 0)
        lax.fori_loop(0, occ, round_sub, 0)

        @pl.when(i + 1 < n_tiles)
        def _():
            start_load(i + 1)

    def for_each_sub(weights, fn):
        def quad(p, carry):
            w = weights()
            for u in range(4):
                fn(4 * p + u, *w)
            return carry

        lax.fori_loop(0, occ // 4, quad, 0)
        done = (occ // 4) * 4

        @pl.when(occ % 4 >= 2)
        def _():
            w = weights()
            fn(done, *w)
            fn(done + 1, *w)

        @pl.when(occ % 2 == 1)
        def _():
            fn(occ - 1, *weights())

    @pl.when(j < MOE_NF)
    def _():
        def weights():
            return wg_ref[...].astype(BF16), wu_ref[...].astype(BF16)

        def gate_up(sub, wg, wu):
            rows = sub_rows(sub)
            xb = xb_scr[rows, :]
            gate = jnp.dot(xb, wg, preferred_element_type=F32) + bg_ref[...]
            up = jnp.dot(xb, wu, preferred_element_type=F32) + bu_ref[...]
            gate = jnp.minimum(gate, SWIGLU_LIMIT)
            up = jnp.clip(up, -SWIGLU_LIMIT, SWIGLU_LIMIT)
            act = (up + 1.0) * (gate * _sigmoid(SWIGLU_ALPHA * gate))
            act_scr[j, rows, :] = act.astype(BF16)

        for_each_sub(weights, gate_up)

    @pl.when(j >= MOE_NF)
    def _():
        def weights():
            return (wd_ref[...].astype(BF16),)

        def down(sub, wd):
            rows = sub_rows(sub)
            y = bd_ref[...] + jnp.dot(act_scr[0, rows, :], wd[0:MOE_TF, :],
                                      preferred_element_type=F32)
            for c in range(1, MOE_NF):
                y = y + jnp.dot(act_scr[c, rows, :], wd[c * MOE_TF:(c + 1) * MOE_TF, :],
                                preferred_element_type=F32)
            out_ref[rows, :] = y

        for_each_sub(weights, down)

    @pl.when(j >= MOE_NF)
    def _():
        def zero_sub(sub, carry):
            out_ref[sub_rows(sub), :] = jnp.zeros((MOE_SUB, MOE_TN), F32)
            return carry

        lax.fori_loop(occ, MOE_SUBS, zero_sub, 0)


def _experts(tile_e, tile_occ, tile_row, xs, wgu, bgu, wd, bd, n_tiles):
    up_off = D_FF // MOE_TF

    def fa(i, j, to):
        return jnp.where(to[i] > 0, jnp.minimum(j, MOE_NF - 1), MOE_NF - 1)

    def fb(i, j, to):
        return jnp.where(to[i] > 0, jnp.maximum(j - MOE_NF, 0), MOE_NN - 1)

    def gate_up_block(i, j, te, to):
        nxt = jnp.minimum(i + 1, n_tiles - 1)
        ahead = j >= MOE_NF
        tile = jnp.where(ahead, nxt, i)
        return te[tile], fa(tile, jnp.where(ahead, 0, j), to)

    def gate_map(i, j, te, to, tr):
        e, chunk = gate_up_block(i, j, te, to)
        return (e, 0, chunk)

    def up_map(i, j, te, to, tr):
        e, chunk = gate_up_block(i, j, te, to)
        return (e, 0, up_off + chunk)

    def down_map(i, j, te, to, tr):
        prev = jnp.maximum(i - 1, 0)
        hold = (j < MOE_NF) & (i > 0) & (to[i] > 0)
        return (jnp.where(hold, te[prev], te[i]), 0, jnp.where(hold, MOE_NN - 1, fb(i, j, to)))

    def out_map(i, j, te, to, tr):
        return (i, jnp.maximum(j - MOE_NF, 0))

    grid_spec = pltpu.PrefetchScalarGridSpec(
        num_scalar_prefetch=3,
        grid=(n_tiles, MOE_NF + MOE_NN),
        in_specs=[
            pl.BlockSpec(memory_space=pl.ANY),
            pl.BlockSpec((None, D_MODEL, MOE_TF), gate_map),
            pl.BlockSpec((None, D_MODEL, MOE_TF), up_map),
            pl.BlockSpec((None, 1, MOE_TF), gate_map),
            pl.BlockSpec((None, 1, MOE_TF), up_map),
            pl.BlockSpec((None, D_FF, MOE_TN), down_map),
            pl.BlockSpec((None, 1, MOE_TN), down_map),
        ],
        out_specs=pl.BlockSpec((MOE_CAP, MOE_TN), out_map),
        scratch_shapes=[
            pltpu.VMEM((MOE_CAP, D_MODEL), F32),
            pltpu.VMEM((MOE_CAP, D_MODEL), BF16),
            pltpu.VMEM((MOE_NF, MOE_CAP, MOE_TF), BF16),
            pltpu.SemaphoreType.DMA,
        ],
    )
    return pl.pallas_call(
        functools.partial(_experts_kernel, n_tiles=n_tiles),
        grid_spec=grid_spec,
        out_shape=jax.ShapeDtypeStruct((n_tiles * MOE_CAP, D_MODEL), F32),
        compiler_params=_cparams(("arbitrary", "arbitrary")),
        name="experts",
    )(tile_e, tile_occ, tile_row, xs, wgu, wgu, bgu, bgu, wd, bd)


def _combine_kernel(dest_ref, ys_hbm, x1_ref, gate_ref, gf_ref, out_ref, buf, sem, *, n_tiles):
    tm = x1_ref.shape[0]
    i = pl.program_id(0)
    slot = i % 2

    def start_gather(tile, s):
        for r in range(tm):
            for k in range(TOP_K):
                d = dest_ref[(tile * tm + r) * TOP_K + k]
                pltpu.make_async_copy(ys_hbm.at[pl.ds(d, 1)], buf.at[s, k, pl.ds(r, 1)],
                                      sem.at[s]).start(priority=k % 2)

    @pl.when(i == 0)
    def _():
        start_gather(0, 0)

    for k in range(TOP_K):
        pltpu.make_async_copy(ys_hbm.at[pl.ds(0, tm)], buf.at[slot, k], sem.at[slot]).wait()

    for s in range(2):
        @pl.when((i + 1 < n_tiles) & (slot == 1 - s))
        def _():
            start_gather(i + 1, s)

    x2 = x1_ref[...]
    for k in range(TOP_K):
        x2 = x2 + gate_ref[:, k:k + 1] * buf[slot, k]
    out_ref[...] = x2 * lax.rsqrt(jnp.mean(x2 * x2, axis=-1, keepdims=True) + EPS) * gf_ref[...]


def _combine(dest, ys, x1, gate, gf):
    t = x1.shape[0]
    tm = COMBINE_TM
    n_tiles = t // tm
    grid_spec = pltpu.PrefetchScalarGridSpec(
        num_scalar_prefetch=1,
        grid=(n_tiles,),
        in_specs=[
            pl.BlockSpec(memory_space=pl.ANY),
            pl.BlockSpec((tm, D_MODEL), lambda i, d: (i, 0)),
            pl.BlockSpec((tm, LANES), lambda i, d: (i, 0)),
            pl.BlockSpec((1, D_MODEL), lambda i, d: (0, 0)),
        ],
        out_specs=pl.BlockSpec((tm, D_MODEL), lambda i, d: (i, 0)),
        scratch_shapes=[
            pltpu.VMEM((2, TOP_K, tm, D_MODEL), F32),
            pltpu.SemaphoreType.DMA((2,)),
        ],
    )
    return pl.pallas_call(
        functools.partial(_combine_kernel, n_tiles=n_tiles),
        grid_spec=grid_spec,
        out_shape=jax.ShapeDtypeStruct((t, D_MODEL), F32),
        compiler_params=_cparams(("arbitrary",)),
        name="combine",
    )(dest, ys, x1, gate, gf)


def _routing_tables(counts, n_tokens):
    n_tiles = N_EXPERTS + (n_tokens * TOP_K) // MOE_CAP
    n_subs = N_EXPERTS + (n_tokens * TOP_K) // MOE_SUB
    subs_per = (counts + MOE_SUB - 1) // MOE_SUB
    sub_end = jnp.cumsum(subs_per)
    x_start = (sub_end - subs_per) * MOE_SUB
    tiles_per = (counts + MOE_CAP - 1) // MOE_CAP
    tile_end = jnp.cumsum(tiles_per)
    tile_start = tile_end - tiles_per
    y_start = tile_start * MOE_CAP

    tile_ids = jnp.arange(n_tiles, dtype=I32)
    n_active = tile_end[-1]
    last = jnp.maximum(n_active - 1, 0)
    active = tile_ids < n_active
    tile_eff = jnp.minimum(tile_ids, last)
    tile_e = jnp.sum((tile_eff[:, None] >= tile_end[None, :]).astype(I32), axis=1)
    tile_e = jnp.minimum(tile_e, N_EXPERTS - 1)
    in_expert = tile_eff - tile_start[tile_e]
    rows_in = counts[tile_e] - in_expert * MOE_CAP
    occ = jnp.clip((rows_in + MOE_SUB - 1) // MOE_SUB, 0, MOE_SUBS)
    tile_occ = jnp.where(active, occ, 0).astype(I32)
    tile_row = (x_start[tile_e] + in_expert * MOE_CAP).astype(I32)

    last_sub = jnp.where(counts > 0, x_start + (subs_per - 1) * MOE_SUB, -1)
    tail_ids = sub_end[-1] + jnp.arange(N_EXPERTS, dtype=I32)
    tail = jnp.where(tail_ids < n_subs, tail_ids * MOE_SUB, -1)
    zero_rows = jnp.concatenate([last_sub, tail]).astype(I32)
    return (n_tiles, n_subs * MOE_SUB, tile_e.astype(I32), tile_occ, tile_row,
            x_start, y_start, zero_rows)


def _layer(x2d, bsz, seq, g_norm_mix, w_in, b_in, w_conv, b_conv, w_sgu, b_sgu, g_sgu_ln, b_sgu_ln,
           g_mlstm_norm, w_proj_sgu, w_proj_mlstm, w_out, g_norm_moe, w_router, b_router,
           w_gate_up, b_gate_up, w_down, b_down, g_out):
    t = x2d.shape[0]

    pad = LANES - 2 * HEADS
    w_if = jnp.pad(w_in[:, N_MAIN:GATE_OFF], ((0, 0), (0, pad))).astype(BF16)
    b_if = jnp.pad(b_in[N_MAIN:GATE_OFF], (0, pad))[None, :]
    w_in_t = w_in.T
    bsb = jnp.broadcast_to(b_sgu.T[:, :, None], (SGU_BLOCK, SGU_GROUPS, SGU_GROUP_DIM))
    bsb = bsb.reshape(SGU_BLOCK, D_MODEL)
    w_r = jnp.pad(w_router, ((0, 0), (0, LANES - N_EXPERTS)))
    w_r_hi = w_r.astype(BF16)
    w_r_lo = (w_r - w_r_hi.astype(F32)).astype(BF16)
    b_r = jnp.pad(b_router, (0, LANES - N_EXPERTS))[None, :]

    h, gates = _norm_gates(x2d, g_norm_mix[None, :], w_if, b_if)
    zm = _in_proj(h, w_in_t, 0, b_in[None, :], N_MAIN, 2 * D_MODEL, 3 * D_MODEL, "in_proj_main")
    zg = _in_proj(h, w_in_t, GATE_OFF, b_in[None, GATE_OFF:], 2 * D_MODEL, 0, 0, "in_proj_gate")
    a = _sgu(zm, zg, w_sgu, bsb, g_sgu_ln[None, :], b_sgu_ln[None, :], w_proj_sgu.astype(BF16))
    ym = _mlstm(zm, gates, w_conv, b_conv[None, :], g_mlstm_norm[None, :], bsz, seq)
    x1, h2, logits = _merge(ym, a, zg, x2d, w_proj_mlstm.astype(BF16), w_out.astype(BF16),
                            g_norm_moe[None, :], w_r_hi, w_r_lo, b_r)
    info, gate, cnt = _route(logits)

    counts = cnt[0, :N_EXPERTS].astype(I32)
    (n_tiles, n_xrows, tile_e, tile_occ, tile_row, x_start, y_start,
     zero_rows) = _routing_tables(counts, t)
    rows = _rows(info, x_start, y_start, n_xrows // MOE_SUB)
    dest_x = rows[:, :TOP_K].reshape(-1)
    dest_y = rows[:, TOP_K:2 * TOP_K].reshape(-1)

    xs = _dispatch(dest_x, zero_rows, h2, n_xrows)
    ys = _experts(tile_e, tile_occ, tile_row, xs, w_gate_up, b_gate_up[:, None, :],
                  w_down, b_down[:, None, :], n_tiles)
    return _combine(dest_y, ys, x1, gate, g_out[None, :])


def kernel(x, g_norm_mix, w_in, b_in, w_conv, b_conv, w_sgu, b_sgu, g_sgu_ln, b_sgu_ln, g_mlstm_norm,
           w_proj_sgu, w_proj_mlstm, w_out, g_norm_moe, w_router, b_router, w_gate_up, b_gate_up,
           w_down, b_down, g_final):
    bsz, seq, d = x.shape
    assert d == D_MODEL and w_in.shape[0] == 1, "single-layer block with d_model 2048"
    out = _layer(x.reshape(bsz * seq, d), bsz, seq, g_norm_mix[0], w_in[0], b_in[0], w_conv[0],
                 b_conv[0], w_sgu[0], b_sgu[0], g_sgu_ln[0], b_sgu_ln[0], g_mlstm_norm[0],
                 w_proj_sgu[0], w_proj_mlstm[0], w_out[0], g_norm_moe[0], w_router[0], b_router[0],
                 w_gate_up[0], b_gate_up[0], w_down[0], b_down[0], g_final)
    return out.reshape(bsz, seq, d)
```
